```python
import math
import jax, jax.numpy as jnp
from jax import lax
import numpy as np

D_MODEL = 2048
BATCH = 8
SEQ = 8192
DEPTH = 4

GRID_W = 64
CTX_LEN = 256
N_MIXERS = 2
SSM_EXPAND = 2
D_INNER = SSM_EXPAND * D_MODEL
SSM_HEAD_DIM = 64
SSM_HEADS = D_INNER // SSM_HEAD_DIM
SSM_GROUPS = 8
SSM_STATE = 128
SSM_CONV = 5
SSM_CHUNK = 128
XBC_DIM = D_INNER + 2 * SSM_GROUPS * SSM_STATE
SSM_IN_DIM = D_INNER + XBC_DIM + 2 * SSM_HEADS
CONF_KERNEL = 31
D_FF = 4 * D_MODEL
EPS = 1e-6

kernel_name = "hybrid_ssd_conformer_dit_trunk"


def rmsnorm(x, g):
    xf = x.astype(jnp.float32)
    y = xf * lax.rsqrt(jnp.mean(xf * xf, axis=-1, keepdims=True) + EPS)
    return (y * g.astype(jnp.float32)).astype(x.dtype)


def layernorm(x, g, b):
    xf = x.astype(jnp.float32)
    mu = jnp.mean(xf, axis=-1, keepdims=True)
    xc = xf - mu
    var = jnp.mean(xc * xc, axis=-1, keepdims=True)
    y = xc * lax.rsqrt(var + EPS) * g.astype(jnp.float32) + b.astype(jnp.float32)
    return y.astype(x.dtype)


def modulate(h, shift, scale):
    return h * (1 + scale) + shift


def dwconv(u, w, b, seg_len):
    bsz, l, ch = u.shape
    k = w.shape[0]
    us = u.reshape(bsz * (l // seg_len), seg_len, ch)
    y = lax.conv_general_dilated(
        us, w[:, None, :].astype(u.dtype), window_strides=(1,),
        padding=[(k // 2, k // 2)], dimension_numbers=("NWC", "WIO", "NWC"),
        feature_group_count=ch)
    return y.reshape(bsz, l, ch) + b.astype(u.dtype)


def to_scan_order(u, rows, col_major):
    if not col_major:
        return u
    b, l, d = u.shape
    return u.reshape(b, rows, GRID_W, d).swapaxes(1, 2).reshape(b, l, d)


def from_scan_order(u, rows, col_major):
    if not col_major:
        return u
    b, l, d = u.shape
    return u.reshape(b, GRID_W, rows, d).swapaxes(1, 2).reshape(b, l, d)


def ssd_scan(xh, dt, a, bm, cm, h0):
    b, l, h, p = xh.shape
    g, n = bm.shape[-2:]
    q = SSM_CHUNK
    nc = l // q
    hg = h // g
    f32 = jnp.float32
    x_c = xh.astype(f32).reshape(b, nc, q, g, hg, p)
    dt_c = dt.reshape(b, nc, q, g, hg)
    b_c = bm.astype(f32).reshape(b, nc, q, g, n)
    c_c = cm.astype(f32).reshape(b, nc, q, g, n)
    xdt = x_c * dt_c[..., None]
    cs = jnp.cumsum(dt_c * a.reshape(g, hg), axis=2)
    mask = jnp.tril(jnp.ones((q, q), dtype=bool))[None, None, :, :, None, None]
    seg = cs[:, :, :, None] - cs[:, :, None, :]
    decay = jnp.exp(jnp.where(mask, seg, -jnp.inf))
    scores = jnp.einsum("bclgn,bcsgn->bclsg", c_c, b_c)
    y_diag = jnp.einsum("bclsgk,bcsgkp->bclgkp", scores[..., None] * decay, xdt)
    decay_to_end = jnp.exp(cs[:, :, -1:] - cs)
    states = jnp.einsum("bcsgn,bcsgkp->bcgkpn", b_c, xdt * decay_to_end[..., None])
    chunk_decay = jnp.exp(cs[:, :, -1])

    def step(hc, inp):
        st, dec = inp
        return dec[..., None, None] * hc + st, hc

    h_final, h_in = lax.scan(
        step, h0.astype(f32).reshape(b, g, hg, p, n),
        (jnp.moveaxis(states, 1, 0), jnp.moveaxis(chunk_decay, 1, 0)))
    h_in = jnp.moveaxis(h_in, 0, 1)
    y_off = jnp.einsum("bclgn,bcgkpn->bclgkp", c_c, h_in) * jnp.exp(cs)[..., None]
    y = (y_diag + y_off).reshape(b, l, h, p)
    return y, h_final.reshape(b, h, p, n)


def ssd_direction(xs, dt_raw, bm, cm, a_log, dt_bias, d_skip, h0, reverse):
    if reverse:
        xs, dt_raw, bm, cm = (jnp.flip(t, axis=1) for t in (xs, dt_raw, bm, cm))
    dt = jax.nn.softplus(dt_raw.astype(jnp.float32) + dt_bias.astype(jnp.float32))
    a = -jnp.exp(a_log.astype(jnp.float32))
    y, hf = ssd_scan(xs, dt, a, bm, cm, h0)
    y = y + d_skip.astype(jnp.float32)[:, None] * xs.astype(jnp.float32)
    if reverse:
        y = jnp.flip(y, axis=1)
    return y, hf


def mamba_mix(u, w_in, conv_w, conv_b, a_log_f, dt_bias_f, d_f, a_log_b, dt_bias_b,
              d_b, norm_g, w_out, h0_f, h0_b, with_output=True):
    b, l, _ = u.shape
    proj = u @ w_in
    z = proj[..., :D_INNER]
    xbc = proj[..., D_INNER:D_INNER + XBC_DIM]
    dt_raw = proj[..., D_INNER + XBC_DIM:]
    xbc = jax.nn.silu(dwconv(xbc, conv_w, conv_b, l))
    gn = SSM_GROUPS * SSM_STATE
    xs = xbc[..., :D_INNER].reshape(b, l, SSM_HEADS, SSM_HEAD_DIM)
    bm = xbc[..., D_INNER:D_INNER + gn].reshape(b, l, SSM_GROUPS, SSM_STATE)
    cm = xbc[..., D_INNER + gn:].reshape(b, l, SSM_GROUPS, SSM_STATE)
    y_f, hf = ssd_direction(xs, dt_raw[..., :SSM_HEADS], bm, cm, a_log_f, dt_bias_f, d_f,
                            h0_f, reverse=False)
    y_b, hb = ssd_direction(xs, dt_raw[..., SSM_HEADS:], bm, cm, a_log_b, dt_bias_b, d_b,
                            h0_b, reverse=True)
    if not with_output:
        return None, hf, hb
    y = (y_f + y_b).reshape(b, l, D_INNER)
    y = rmsnorm(y * jax.nn.silu(z.astype(jnp.float32)), norm_g)
    return y.astype(u.dtype) @ w_out, hf, hb


def conformer_mix(u, pw1_w, pw1_b, dw_w, dw_b, ln_g, ln_b, pw2_w, pw2_b, seg_len):
    a = u @ pw1_w + pw1_b
    v = a[..., :D_MODEL] * jax.nn.sigmoid(a[..., D_MODEL:])
    v = dwconv(v, dw_w, dw_b, seg_len)
    v = layernorm(v, ln_g, ln_b)
    v = v * jax.nn.sigmoid(v)
    return v @ pw2_w + pw2_b


def sq_relu_mlp(u, w1, w2):
    return jnp.square(jax.nn.relu(u @ w1)) @ w2


def _fwd_setup_inputs(seed: int = 0) -> dict:
    key = jax.random.key(seed)
    ks = jax.random.split(key, 40)
    f32 = jnp.float32
    n_ssm = (DEPTH + 1) // 2
    n_conf = DEPTH // 2

    def nrm(k, shape, scale):
        return jax.random.normal(k, shape, f32) * scale

    def gain(k, shape):
        return 1.0 + 0.05 * jax.random.normal(k, shape, f32)

    def dt_bias(k):
        dt = jnp.exp(jax.random.uniform(k, (n_ssm, SSM_HEADS), f32,
                                        math.log(1e-3), math.log(1e-1)))
        return dt + jnp.log(-jnp.expm1(-dt))

    return {
        "x": nrm(ks[0], (BATCH, SEQ, D_MODEL), 1.0),
        "c": nrm(ks[1], (BATCH, D_MODEL), 1.0),
        "ctx": nrm(ks[2], (BATCH, CTX_LEN, D_MODEL), 1.0),
        "c_ctx": nrm(ks[3], (D_MODEL,), 1.0),
        "mod_w": nrm(ks[4], (DEPTH, D_MODEL, 6 * D_MODEL), 0.5 * D_MODEL ** -0.5),
        "mod_b": nrm(ks[5], (DEPTH, 6 * D_MODEL), 0.02),
        "pre_mix_g": gain(ks[6], (DEPTH, D_MODEL)),
        "post_mix_g": gain(ks[7], (DEPTH, D_MODEL)),
        "pre_mlp_g": gain(ks[8], (DEPTH, D_MODEL)),
        "post_mlp_g": gain(ks[9], (DEPTH, D_MODEL)),
        "mlp_w1": nrm(ks[10], (DEPTH, D_MODEL, D_FF), D_MODEL ** -0.5),
        "mlp_w2": nrm(ks[11], (DEPTH, D_FF, D_MODEL), D_FF ** -0.5),
        "ssm_in_w": nrm(ks[12], (n_ssm, D_MODEL, SSM_IN_DIM), D_MODEL ** -0.5),
        "ssm_conv_w": nrm(ks[13], (n_ssm, SSM_CONV, XBC_DIM), SSM_CONV ** -0.5),
        "ssm_conv_b": nrm(ks[14], (n_ssm, XBC_DIM), 0.02),
        "ssm_a_log_f": jnp.log(jax.random.uniform(ks[15], (n_ssm, SSM_HEADS), f32, 1.0, 16.0)),
        "ssm_dt_bias_f": dt_bias(ks[16]),
        "ssm_d_f": gain(ks[17], (n_ssm, SSM_HEADS)),
        "ssm_a_log_b": jnp.log(jax.random.uniform(ks[18], (n_ssm, SSM_HEADS), f32, 1.0, 16.0)),
        "ssm_dt_bias_b": dt_bias(ks[19]),
        "ssm_d_b": gain(ks[20], (n_ssm, SSM_HEADS)),
        "ssm_norm_g": gain(ks[21], (n_ssm, D_INNER)),
        "ssm_out_w": nrm(ks[22], (n_ssm, D_INNER, D_MODEL), D_INNER ** -0.5),
        "conf_pw1_w": nrm(ks[23], (n_conf, D_MODEL, 2 * D_MODEL), D_MODEL ** -0.5),
        "conf_pw1_b": nrm(ks[24], (n_conf, 2 * D_MODEL), 0.02),
        "conf_dw_w": nrm(ks[25], (n_conf, CONF_KERNEL, D_MODEL), CONF_KERNEL ** -0.5),
        "conf_dw_b": nrm(ks[26], (n_conf, D_MODEL), 0.02),
        "conf_ln_g": gain(ks[27], (n_conf, D_MODEL)),
        "conf_ln_b": nrm(ks[28], (n_conf, D_MODEL), 0.02),
        "conf_pw2_w": nrm(ks[29], (n_conf, D_MODEL, D_MODEL), D_MODEL ** -0.5),
        "conf_pw2_b": nrm(ks[30], (n_conf, D_MODEL), 0.02),
    }


def _fwd_reference(x, c, ctx, c_ctx, mod_w, mod_b, pre_mix_g, post_mix_g, pre_mlp_g, post_mlp_g,
              mlp_w1, mlp_w2, ssm_in_w, ssm_conv_w, ssm_conv_b, ssm_a_log_f, ssm_dt_bias_f,
              ssm_d_f, ssm_a_log_b, ssm_dt_bias_b, ssm_d_b, ssm_norm_g, ssm_out_w,
              conf_pw1_w, conf_pw1_b, conf_dw_w, conf_dw_b, conf_ln_g, conf_ln_b,
              conf_pw2_w, conf_pw2_b):
    bsz, seq_len, _ = x.shape
    rows = seq_len // GRID_W
    sc = jax.nn.silu(c)
    scc = jax.nn.silu(c_ctx)
    h, hc = x, ctx
    for i in range(DEPTH):
        last = i == DEPTH - 1
        kind = i % N_MIXERS
        j = i // N_MIXERS
        col_major = (j % 2) == 1
        m = sc @ mod_w[i] + mod_b[i]
        sh1, sc1, g1, sh2, sc2, g2 = jnp.split(m[:, None, :], 6, axis=-1)
        mc = scc @ mod_w[i] + mod_b[i]
        csh1, csc1, cg1, csh2, csc2, cg2 = jnp.split(mc, 6)

        u = modulate(rmsnorm(h, pre_mix_g[i]), sh1, sc1)
        uc = modulate(rmsnorm(hc, pre_mix_g[i]), csh1, csc1)
        u = to_scan_order(u, rows, col_major)
        if kind == 0:
            p = (ssm_in_w[j], ssm_conv_w[j], ssm_conv_b[j], ssm_a_log_f[j], ssm_dt_bias_f[j],
                 ssm_d_f[j], ssm_a_log_b[j], ssm_dt_bias_b[j], ssm_d_b[j], ssm_norm_g[j],
                 ssm_out_w[j])
            zeros = jnp.zeros((bsz, SSM_HEADS, SSM_HEAD_DIM, SSM_STATE), jnp.float32)
            yc, s_f, s_b = mamba_mix(uc, *p, zeros, zeros, with_output=not last)
            y, _, _ = mamba_mix(u, *p, s_f, s_b)
        else:
            p = (conf_pw1_w[j], conf_pw1_b[j], conf_dw_w[j], conf_dw_b[j], conf_ln_g[j],
                 conf_ln_b[j], conf_pw2_w[j], conf_pw2_b[j])
            seg = rows if col_major else GRID_W
            y = conformer_mix(u, *p, seg)
            yc = None if last else conformer_mix(uc, *p, uc.shape[1])
        y = from_scan_order(y, rows, col_major)
        h = h + (g1 * rmsnorm(y, post_mix_g[i])).astype(h.dtype)

        f = sq_relu_mlp(modulate(rmsnorm(h, pre_mlp_g[i]), sh2, sc2), mlp_w1[i], mlp_w2[i])
        h = h + (g2 * rmsnorm(f, post_mlp_g[i])).astype(h.dtype)

        if not last:
            hc = hc + (cg1 * rmsnorm(yc, post_mix_g[i])).astype(hc.dtype)
            fc = sq_relu_mlp(modulate(rmsnorm(hc, pre_mlp_g[i]), csh2, csc2),
                             mlp_w1[i], mlp_w2[i])
            hc = hc + (cg2 * rmsnorm(fc, post_mlp_g[i])).astype(hc.dtype)
    return h


import jax as _jax
import jax.numpy as _jnp

TWIN_FORMAT = 'train_step'
FWD_PARAMS = ['x', 'c', 'ctx', 'c_ctx', 'mod_w', 'mod_b', 'pre_mix_g', 'post_mix_g', 'pre_mlp_g', 'post_mlp_g', 'mlp_w1', 'mlp_w2', 'ssm_in_w', 'ssm_conv_w', 'ssm_conv_b', 'ssm_a_log_f', 'ssm_dt_bias_f', 'ssm_d_f', 'ssm_a_log_b', 'ssm_dt_bias_b', 'ssm_d_b', 'ssm_norm_g', 'ssm_out_w', 'conf_pw1_w', 'conf_pw1_b', 'conf_dw_w', 'conf_dw_b', 'conf_ln_g', 'conf_ln_b', 'conf_pw2_w', 'conf_pw2_b']
TWIN_WEIGHTS = ['c_ctx', 'mod_w', 'mod_b', 'pre_mix_g', 'post_mix_g', 'pre_mlp_g', 'post_mlp_g', 'mlp_w1', 'mlp_w2', 'ssm_in_w', 'ssm_conv_w', 'ssm_conv_b', 'ssm_a_log_f', 'ssm_dt_bias_f', 'ssm_d_f', 'ssm_a_log_b', 'ssm_dt_bias_b', 'ssm_d_b', 'ssm_norm_g', 'ssm_out_w', 'conf_pw1_w', 'conf_pw1_b', 'conf_dw_w', 'conf_dw_b', 'conf_ln_g', 'conf_ln_b', 'conf_pw2_w', 'conf_pw2_b']
TWIN_DIFF_INPUT = 'x'
TWIN_INPUTS = ['x', 'c', 'ctx', 'c_ctx', 'mod_w', 'mod_b', 'pre_mix_g', 'post_mix_g', 'pre_mlp_g', 'post_mlp_g', 'mlp_w1', 'mlp_w2', 'ssm_in_w', 'ssm_conv_w', 'ssm_conv_b', 'ssm_a_log_f', 'ssm_dt_bias_f', 'ssm_d_f', 'ssm_a_log_b', 'ssm_dt_bias_b', 'ssm_d_b', 'ssm_norm_g', 'ssm_out_w', 'conf_pw1_w', 'conf_pw1_b', 'conf_dw_w', 'conf_dw_b', 'conf_ln_g', 'conf_ln_b', 'conf_pw2_w', 'conf_pw2_b', 'loss_target', 'm_c_ctx', 'm_mod_w', 'm_mod_b', 'm_pre_mix_g', 'm_post_mix_g', 'm_pre_mlp_g', 'm_post_mlp_g', 'm_mlp_w1', 'm_mlp_w2', 'm_ssm_in_w', 'm_ssm_conv_w', 'm_ssm_conv_b', 'm_ssm_a_log_f', 'm_ssm_dt_bias_f', 'm_ssm_d_f', 'm_ssm_a_log_b', 'm_ssm_dt_bias_b', 'm_ssm_d_b', 'm_ssm_norm_g', 'm_ssm_out_w', 'm_conf_pw1_w', 'm_conf_pw1_b', 'm_conf_dw_w', 'm_conf_dw_b', 'm_conf_ln_g', 'm_conf_ln_b', 'm_conf_pw2_w', 'm_conf_pw2_b', 'v_c_ctx', 'v_mod_w', 'v_mod_b', 'v_pre_mix_g', 'v_post_mix_g', 'v_pre_mlp_g', 'v_post_mlp_g', 'v_mlp_w1', 'v_mlp_w2', 'v_ssm_in_w', 'v_ssm_conv_w', 'v_ssm_conv_b', 'v_ssm_a_log_f', 'v_ssm_dt_bias_f', 'v_ssm_d_f', 'v_ssm_a_log_b', 'v_ssm_dt_bias_b', 'v_ssm_d_b', 'v_ssm_norm_g', 'v_ssm_out_w', 'v_conf_pw1_w', 'v_conf_pw1_b', 'v_conf_dw_w', 'v_conf_dw_b', 'v_conf_ln_g', 'v_conf_ln_b', 'v_conf_pw2_w', 'v_conf_pw2_b']
TWIN_OUTPUTS = ['loss', 'grad_x', 'grad_c_ctx', 'grad_mod_w', 'grad_mod_b', 'grad_pre_mix_g', 'grad_post_mix_g', 'grad_pre_mlp_g', 'grad_post_mlp_g', 'grad_mlp_w1', 'grad_mlp_w2', 'grad_ssm_in_w', 'grad_ssm_conv_w', 'grad_ssm_conv_b', 'grad_ssm_a_log_f', 'grad_ssm_dt_bias_f', 'grad_ssm_d_f', 'grad_ssm_a_log_b', 'grad_ssm_dt_bias_b', 'grad_ssm_d_b', 'grad_ssm_norm_g', 'grad_ssm_out_w', 'grad_conf_pw1_w', 'grad_conf_pw1_b', 'grad_conf_dw_w', 'grad_conf_dw_b', 'grad_conf_ln_g', 'grad_conf_ln_b', 'grad_conf_pw2_w', 'grad_conf_pw2_b', 'delta_c_ctx', 'delta_mod_w', 'delta_mod_b', 'delta_pre_mix_g', 'delta_post_mix_g', 'delta_pre_mlp_g', 'delta_post_mlp_g', 'delta_mlp_w1', 'delta_mlp_w2', 'delta_ssm_in_w', 'delta_ssm_conv_w', 'delta_ssm_conv_b', 'delta_ssm_a_log_f', 'delta_ssm_dt_bias_f', 'delta_ssm_d_f', 'delta_ssm_a_log_b', 'delta_ssm_dt_bias_b', 'delta_ssm_d_b', 'delta_ssm_norm_g', 'delta_ssm_out_w', 'delta_conf_pw1_w', 'delta_conf_pw1_b', 'delta_conf_dw_w', 'delta_conf_dw_b', 'delta_conf_ln_g', 'delta_conf_ln_b', 'delta_conf_pw2_w', 'delta_conf_pw2_b', 'new_m_c_ctx', 'new_m_mod_w', 'new_m_mod_b', 'new_m_pre_mix_g', 'new_m_post_mix_g', 'new_m_pre_mlp_g', 'new_m_post_mlp_g', 'new_m_mlp_w1', 'new_m_mlp_w2', 'new_m_ssm_in_w', 'new_m_ssm_conv_w', 'new_m_ssm_conv_b', 'new_m_ssm_a_log_f', 'new_m_ssm_dt_bias_f', 'new_m_ssm_d_f', 'new_m_ssm_a_log_b', 'new_m_ssm_dt_bias_b', 'new_m_ssm_d_b', 'new_m_ssm_norm_g', 'new_m_ssm_out_w', 'new_m_conf_pw1_w', 'new_m_conf_pw1_b', 'new_m_conf_dw_w', 'new_m_conf_dw_b', 'new_m_conf_ln_g', 'new_m_conf_ln_b', 'new_m_conf_pw2_w', 'new_m_conf_pw2_b', 'new_v_c_ctx', 'new_v_mod_w', 'new_v_mod_b', 'new_v_pre_mix_g', 'new_v_post_mix_g', 'new_v_pre_mlp_g', 'new_v_post_mlp_g', 'new_v_mlp_w1', 'new_v_mlp_w2', 'new_v_ssm_in_w', 'new_v_ssm_conv_w', 'new_v_ssm_conv_b', 'new_v_ssm_a_log_f', 'new_v_ssm_dt_bias_f', 'new_v_ssm_d_f', 'new_v_ssm_a_log_b', 'new_v_ssm_dt_bias_b', 'new_v_ssm_d_b', 'new_v_ssm_norm_g', 'new_v_ssm_out_w', 'new_v_conf_pw1_w', 'new_v_conf_pw1_b', 'new_v_conf_dw_w', 'new_v_conf_dw_b', 'new_v_conf_ln_g', 'new_v_conf_ln_b', 'new_v_conf_pw2_w', 'new_v_conf_pw2_b']
TWIN_LEAF_KINDS = {'loss': 'loss', 'grad_x': 'grad_x', 'grad_c_ctx': 'grad_w', 'grad_mod_w': 'grad_w', 'grad_mod_b': 'grad_w', 'grad_pre_mix_g': 'grad_w', 'grad_post_mix_g': 'grad_w', 'grad_pre_mlp_g': 'grad_w', 'grad_post_mlp_g': 'grad_w', 'grad_mlp_w1': 'grad_w', 'grad_mlp_w2': 'grad_w', 'grad_ssm_in_w': 'grad_w', 'grad_ssm_conv_w': 'grad_w', 'grad_ssm_conv_b': 'grad_w', 'grad_ssm_a_log_f': 'grad_w', 'grad_ssm_dt_bias_f': 'grad_w', 'grad_ssm_d_f': 'grad_w', 'grad_ssm_a_log_b': 'grad_w', 'grad_ssm_dt_bias_b': 'grad_w', 'grad_ssm_d_b': 'grad_w', 'grad_ssm_norm_g': 'grad_w', 'grad_ssm_out_w': 'grad_w', 'grad_conf_pw1_w': 'grad_w', 'grad_conf_pw1_b': 'grad_w', 'grad_conf_dw_w': 'grad_w', 'grad_conf_dw_b': 'grad_w', 'grad_conf_ln_g': 'grad_w', 'grad_conf_ln_b': 'grad_w', 'grad_conf_pw2_w': 'grad_w', 'grad_conf_pw2_b': 'grad_w', 'delta_c_ctx': 'delta_w', 'delta_mod_w': 'delta_w', 'delta_mod_b': 'delta_w', 'delta_pre_mix_g': 'delta_w', 'delta_post_mix_g': 'delta_w', 'delta_pre_mlp_g': 'delta_w', 'delta_post_mlp_g': 'delta_w', 'delta_mlp_w1': 'delta_w', 'delta_mlp_w2': 'delta_w', 'delta_ssm_in_w': 'delta_w', 'delta_ssm_conv_w': 'delta_w', 'delta_ssm_conv_b': 'delta_w', 'delta_ssm_a_log_f': 'delta_w', 'delta_ssm_dt_bias_f': 'delta_w', 'delta_ssm_d_f': 'delta_w', 'delta_ssm_a_log_b': 'delta_w', 'delta_ssm_dt_bias_b': 'delta_w', 'delta_ssm_d_b': 'delta_w', 'delta_ssm_norm_g': 'delta_w', 'delta_ssm_out_w': 'delta_w', 'delta_conf_pw1_w': 'delta_w', 'delta_conf_pw1_b': 'delta_w', 'delta_conf_dw_w': 'delta_w', 'delta_conf_dw_b': 'delta_w', 'delta_conf_ln_g': 'delta_w', 'delta_conf_ln_b': 'delta_w', 'delta_conf_pw2_w': 'delta_w', 'delta_conf_pw2_b': 'delta_w', 'new_m_c_ctx': 'new_m', 'new_m_mod_w': 'new_m', 'new_m_mod_b': 'new_m', 'new_m_pre_mix_g': 'new_m', 'new_m_post_mix_g': 'new_m', 'new_m_pre_mlp_g': 'new_m', 'new_m_post_mlp_g': 'new_m', 'new_m_mlp_w1': 'new_m', 'new_m_mlp_w2': 'new_m', 'new_m_ssm_in_w': 'new_m', 'new_m_ssm_conv_w': 'new_m', 'new_m_ssm_conv_b': 'new_m', 'new_m_ssm_a_log_f': 'new_m', 'new_m_ssm_dt_bias_f': 'new_m', 'new_m_ssm_d_f': 'new_m', 'new_m_ssm_a_log_b': 'new_m', 'new_m_ssm_dt_bias_b': 'new_m', 'new_m_ssm_d_b': 'new_m', 'new_m_ssm_norm_g': 'new_m', 'new_m_ssm_out_w': 'new_m', 'new_m_conf_pw1_w': 'new_m', 'new_m_conf_pw1_b': 'new_m', 'new_m_conf_dw_w': 'new_m', 'new_m_conf_dw_b': 'new_m', 'new_m_conf_ln_g': 'new_m', 'new_m_conf_ln_b': 'new_m', 'new_m_conf_pw2_w': 'new_m', 'new_m_conf_pw2_b': 'new_m', 'new_v_c_ctx': 'new_v', 'new_v_mod_w': 'new_v', 'new_v_mod_b': 'new_v', 'new_v_pre_mix_g': 'new_v', 'new_v_post_mix_g': 'new_v', 'new_v_pre_mlp_g': 'new_v', 'new_v_post_mlp_g': 'new_v', 'new_v_mlp_w1': 'new_v', 'new_v_mlp_w2': 'new_v', 'new_v_ssm_in_w': 'new_v', 'new_v_ssm_conv_w': 'new_v', 'new_v_ssm_conv_b': 'new_v', 'new_v_ssm_a_log_f': 'new_v', 'new_v_ssm_dt_bias_f': 'new_v', 'new_v_ssm_d_f': 'new_v', 'new_v_ssm_a_log_b': 'new_v', 'new_v_ssm_dt_bias_b': 'new_v', 'new_v_ssm_d_b': 'new_v', 'new_v_ssm_norm_g': 'new_v', 'new_v_ssm_out_w': 'new_v', 'new_v_conf_pw1_w': 'new_v', 'new_v_conf_pw1_b': 'new_v', 'new_v_conf_dw_w': 'new_v', 'new_v_conf_dw_b': 'new_v', 'new_v_conf_ln_g': 'new_v', 'new_v_conf_ln_b': 'new_v', 'new_v_conf_pw2_w': 'new_v', 'new_v_conf_pw2_b': 'new_v'}


def _forward(args):
    return _fwd_reference(*[args[k] for k in FWD_PARAMS])


def _output_shape():
    def fwd():
        inp = _fwd_setup_inputs(0)
        return _fwd_reference(*[inp[k] for k in FWD_PARAMS])
    out = _jax.eval_shape(fwd)
    return out.shape, out.dtype

N_MICROBATCH = 1
ADAM_LR = 0.001
ADAM_B1 = 0.9
ADAM_B2 = 0.999
ADAM_EPS = 1e-08
ADAM_WD = 0.01
ADAM_STEP = 10
PER_EXAMPLE_BATCH_AXIS = {'x': 0, 'c': 0, 'ctx': 0, 'loss_target': 0}
SHARED_INPUTS = []
_WEIGHT_DTYPES = {'c_ctx': _jnp.float32, 'mod_w': _jnp.float32, 'mod_b': _jnp.float32, 'pre_mix_g': _jnp.float32, 'post_mix_g': _jnp.float32, 'pre_mlp_g': _jnp.float32, 'post_mlp_g': _jnp.float32, 'mlp_w1': _jnp.float32, 'mlp_w2': _jnp.float32, 'ssm_in_w': _jnp.float32, 'ssm_conv_w': _jnp.float32, 'ssm_conv_b': _jnp.float32, 'ssm_a_log_f': _jnp.float32, 'ssm_dt_bias_f': _jnp.float32, 'ssm_d_f': _jnp.float32, 'ssm_a_log_b': _jnp.float32, 'ssm_dt_bias_b': _jnp.float32, 'ssm_d_b': _jnp.float32, 'ssm_norm_g': _jnp.float32, 'ssm_out_w': _jnp.float32, 'conf_pw1_w': _jnp.float32, 'conf_pw1_b': _jnp.float32, 'conf_dw_w': _jnp.float32, 'conf_dw_b': _jnp.float32, 'conf_ln_g': _jnp.float32, 'conf_ln_b': _jnp.float32, 'conf_pw2_w': _jnp.float32, 'conf_pw2_b': _jnp.float32}
MOMENT_SCALE = {'c_ctx': 1.177211e-02, 'mod_w': 1.380932e+00, 'mod_b': 2.918433e+00, 'pre_mix_g': 1.407661e-01, 'post_mix_g': 3.303851e+00, 'pre_mlp_g': 1.505322e-01, 'post_mlp_g': 3.408116e+00, 'mlp_w1': 1.257972e-01, 'mlp_w2': 5.984583e-01, 'ssm_in_w': 1.127417e-01, 'ssm_conv_w': 1.396657e-01, 'ssm_conv_b': 3.295079e-01, 'ssm_a_log_f': 5.956066e-01, 'ssm_dt_bias_f': 1.315399e-01, 'ssm_d_f': 3.336893e-01, 'ssm_a_log_b': 6.275351e-01, 'ssm_dt_bias_b': 1.200137e-01, 'ssm_d_b': 3.336894e-01, 'ssm_norm_g': 2.390201e-01, 'ssm_out_w': 3.386834e-01, 'conf_pw1_w': 1.800652e-01, 'conf_pw1_b': 6.451683e-01, 'conf_dw_w': 2.582180e-01, 'conf_dw_b': 1.101535e+00, 'conf_ln_g': 7.940386e-01, 'conf_ln_b': 1.099487e+00, 'conf_pw2_w': 4.994213e-01, 'conf_pw2_b': 2.227325e+00}


def _to_microbatches(a, axis):
    t = _jnp.moveaxis(a, axis, 0)
    t = t.reshape((N_MICROBATCH, t.shape[0] // N_MICROBATCH) + t.shape[1:])
    return _jnp.moveaxis(t, 1, axis + 1)


def setup_inputs(seed: int = 0) -> dict:
    inp = _fwd_setup_inputs(seed)
    key = _jax.random.fold_in(_jax.random.key(seed), 7919)
    shape, _ = _output_shape()
    out = dict(inp)
    out["loss_target"] = _jax.random.normal(_jax.random.fold_in(key, 0), shape, _jnp.float32)
    for i, name in enumerate(TWIN_WEIGHTS):
        w = inp[name].astype(_jnp.float32)
        if MOMENT_SCALE is None:
            s = _jnp.sqrt(_jnp.mean(_jnp.square(w)) + 1e-30)
        else:
            s = MOMENT_SCALE[name]
        km, kv = _jax.random.split(_jax.random.fold_in(key, i + 1))
        out[name] = w
        out["m_" + name] = s * _jax.random.normal(km, w.shape, _jnp.float32)
        out["v_" + name] = (s * s) * _jax.random.uniform(kv, w.shape, _jnp.float32, 0.5, 1.5)
    if N_MICROBATCH > 1:
        for name, axis in PER_EXAMPLE_BATCH_AXIS.items():
            out[name] = _to_microbatches(out[name], axis)
    return {'x': out['x'], 'c': out['c'], 'ctx': out['ctx'], 'c_ctx': out['c_ctx'], 'mod_w': out['mod_w'], 'mod_b': out['mod_b'], 'pre_mix_g': out['pre_mix_g'], 'post_mix_g': out['post_mix_g'], 'pre_mlp_g': out['pre_mlp_g'], 'post_mlp_g': out['post_mlp_g'], 'mlp_w1': out['mlp_w1'], 'mlp_w2': out['mlp_w2'], 'ssm_in_w': out['ssm_in_w'], 'ssm_conv_w': out['ssm_conv_w'], 'ssm_conv_b': out['ssm_conv_b'], 'ssm_a_log_f': out['ssm_a_log_f'], 'ssm_dt_bias_f': out['ssm_dt_bias_f'], 'ssm_d_f': out['ssm_d_f'], 'ssm_a_log_b': out['ssm_a_log_b'], 'ssm_dt_bias_b': out['ssm_dt_bias_b'], 'ssm_d_b': out['ssm_d_b'], 'ssm_norm_g': out['ssm_norm_g'], 'ssm_out_w': out['ssm_out_w'], 'conf_pw1_w': out['conf_pw1_w'], 'conf_pw1_b': out['conf_pw1_b'], 'conf_dw_w': out['conf_dw_w'], 'conf_dw_b': out['conf_dw_b'], 'conf_ln_g': out['conf_ln_g'], 'conf_ln_b': out['conf_ln_b'], 'conf_pw2_w': out['conf_pw2_w'], 'conf_pw2_b': out['conf_pw2_b'], 'loss_target': out['loss_target'], 'm_c_ctx': out['m_c_ctx'], 'm_mod_w': out['m_mod_w'], 'm_mod_b': out['m_mod_b'], 'm_pre_mix_g': out['m_pre_mix_g'], 'm_post_mix_g': out['m_post_mix_g'], 'm_pre_mlp_g': out['m_pre_mlp_g'], 'm_post_mlp_g': out['m_post_mlp_g'], 'm_mlp_w1': out['m_mlp_w1'], 'm_mlp_w2': out['m_mlp_w2'], 'm_ssm_in_w': out['m_ssm_in_w'], 'm_ssm_conv_w': out['m_ssm_conv_w'], 'm_ssm_conv_b': out['m_ssm_conv_b'], 'm_ssm_a_log_f': out['m_ssm_a_log_f'], 'm_ssm_dt_bias_f': out['m_ssm_dt_bias_f'], 'm_ssm_d_f': out['m_ssm_d_f'], 'm_ssm_a_log_b': out['m_ssm_a_log_b'], 'm_ssm_dt_bias_b': out['m_ssm_dt_bias_b'], 'm_ssm_d_b': out['m_ssm_d_b'], 'm_ssm_norm_g': out['m_ssm_norm_g'], 'm_ssm_out_w': out['m_ssm_out_w'], 'm_conf_pw1_w': out['m_conf_pw1_w'], 'm_conf_pw1_b': out['m_conf_pw1_b'], 'm_conf_dw_w': out['m_conf_dw_w'], 'm_conf_dw_b': out['m_conf_dw_b'], 'm_conf_ln_g': out['m_conf_ln_g'], 'm_conf_ln_b': out['m_conf_ln_b'], 'm_conf_pw2_w': out['m_conf_pw2_w'], 'm_conf_pw2_b': out['m_conf_pw2_b'], 'v_c_ctx': out['v_c_ctx'], 'v_mod_w': out['v_mod_w'], 'v_mod_b': out['v_mod_b'], 'v_pre_mix_g': out['v_pre_mix_g'], 'v_post_mix_g': out['v_post_mix_g'], 'v_pre_mlp_g': out['v_pre_mlp_g'], 'v_post_mlp_g': out['v_post_mlp_g'], 'v_mlp_w1': out['v_mlp_w1'], 'v_mlp_w2': out['v_mlp_w2'], 'v_ssm_in_w': out['v_ssm_in_w'], 'v_ssm_conv_w': out['v_ssm_conv_w'], 'v_ssm_conv_b': out['v_ssm_conv_b'], 'v_ssm_a_log_f': out['v_ssm_a_log_f'], 'v_ssm_dt_bias_f': out['v_ssm_dt_bias_f'], 'v_ssm_d_f': out['v_ssm_d_f'], 'v_ssm_a_log_b': out['v_ssm_a_log_b'], 'v_ssm_dt_bias_b': out['v_ssm_dt_bias_b'], 'v_ssm_d_b': out['v_ssm_d_b'], 'v_ssm_norm_g': out['v_ssm_norm_g'], 'v_ssm_out_w': out['v_ssm_out_w'], 'v_conf_pw1_w': out['v_conf_pw1_w'], 'v_conf_pw1_b': out['v_conf_pw1_b'], 'v_conf_dw_w': out['v_conf_dw_w'], 'v_conf_dw_b': out['v_conf_dw_b'], 'v_conf_ln_g': out['v_conf_ln_g'], 'v_conf_ln_b': out['v_conf_ln_b'], 'v_conf_pw2_w': out['v_conf_pw2_w'], 'v_conf_pw2_b': out['v_conf_pw2_b']}


def _loss(weights, diff, rest, loss_target):
    with _jax.named_scope("forward"):
        args = {**rest, TWIN_DIFF_INPUT: diff, **{k: w.astype(_WEIGHT_DTYPES[k]) for k, w in weights.items()}}
        y = _forward(args)
    with _jax.named_scope("loss_head"):
        err = _jnp.square(y.astype(_jnp.float32) - loss_target)
        return 0.5 * _jnp.sum(_jnp.mean(err, axis=-1)) if err.ndim else 0.5 * err


def _adamw(w, g, m, v):
    m = ADAM_B1 * m + (1.0 - ADAM_B1) * g
    v = ADAM_B2 * v + (1.0 - ADAM_B2) * _jnp.square(g)
    m_hat = m / (1.0 - ADAM_B1 ** ADAM_STEP)
    v_hat = v / (1.0 - ADAM_B2 ** ADAM_STEP)
    delta = -ADAM_LR * (m_hat / (_jnp.sqrt(v_hat) + ADAM_EPS) + ADAM_WD * w)
    return delta, m, v


def reference(x, c, ctx, c_ctx, mod_w, mod_b, pre_mix_g, post_mix_g, pre_mlp_g, post_mlp_g, mlp_w1, mlp_w2, ssm_in_w, ssm_conv_w, ssm_conv_b, ssm_a_log_f, ssm_dt_bias_f, ssm_d_f, ssm_a_log_b, ssm_dt_bias_b, ssm_d_b, ssm_norm_g, ssm_out_w, conf_pw1_w, conf_pw1_b, conf_dw_w, conf_dw_b, conf_ln_g, conf_ln_b, conf_pw2_w, conf_pw2_b, loss_target, m_c_ctx, m_mod_w, m_mod_b, m_pre_mix_g, m_post_mix_g, m_pre_mlp_g, m_post_mlp_g, m_mlp_w1, m_mlp_w2, m_ssm_in_w, m_ssm_conv_w, m_ssm_conv_b, m_ssm_a_log_f, m_ssm_dt_bias_f, m_ssm_d_f, m_ssm_a_log_b, m_ssm_dt_bias_b, m_ssm_d_b, m_ssm_norm_g, m_ssm_out_w, m_conf_pw1_w, m_conf_pw1_b, m_conf_dw_w, m_conf_dw_b, m_conf_ln_g, m_conf_ln_b, m_conf_pw2_w, m_conf_pw2_b, v_c_ctx, v_mod_w, v_mod_b, v_pre_mix_g, v_post_mix_g, v_pre_mlp_g, v_post_mlp_g, v_mlp_w1, v_mlp_w2, v_ssm_in_w, v_ssm_conv_w, v_ssm_conv_b, v_ssm_a_log_f, v_ssm_dt_bias_f, v_ssm_d_f, v_ssm_a_log_b, v_ssm_dt_bias_b, v_ssm_d_b, v_ssm_norm_g, v_ssm_out_w, v_conf_pw1_w, v_conf_pw1_b, v_conf_dw_w, v_conf_dw_b, v_conf_ln_g, v_conf_ln_b, v_conf_pw2_w, v_conf_pw2_b):
    given = dict(x=x, c=c, ctx=ctx, c_ctx=c_ctx, mod_w=mod_w, mod_b=mod_b, pre_mix_g=pre_mix_g, post_mix_g=post_mix_g, pre_mlp_g=pre_mlp_g, post_mlp_g=post_mlp_g, mlp_w1=mlp_w1, mlp_w2=mlp_w2, ssm_in_w=ssm_in_w, ssm_conv_w=ssm_conv_w, ssm_conv_b=ssm_conv_b, ssm_a_log_f=ssm_a_log_f, ssm_dt_bias_f=ssm_dt_bias_f, ssm_d_f=ssm_d_f, ssm_a_log_b=ssm_a_log_b, ssm_dt_bias_b=ssm_dt_bias_b, ssm_d_b=ssm_d_b, ssm_norm_g=ssm_norm_g, ssm_out_w=ssm_out_w, conf_pw1_w=conf_pw1_w, conf_pw1_b=conf_pw1_b, conf_dw_w=conf_dw_w, conf_dw_b=conf_dw_b, conf_ln_g=conf_ln_g, conf_ln_b=conf_ln_b, conf_pw2_w=conf_pw2_w, conf_pw2_b=conf_pw2_b, loss_target=loss_target, m_c_ctx=m_c_ctx, m_mod_w=m_mod_w, m_mod_b=m_mod_b, m_pre_mix_g=m_pre_mix_g, m_post_mix_g=m_post_mix_g, m_pre_mlp_g=m_pre_mlp_g, m_post_mlp_g=m_post_mlp_g, m_mlp_w1=m_mlp_w1, m_mlp_w2=m_mlp_w2, m_ssm_in_w=m_ssm_in_w, m_ssm_conv_w=m_ssm_conv_w, m_ssm_conv_b=m_ssm_conv_b, m_ssm_a_log_f=m_ssm_a_log_f, m_ssm_dt_bias_f=m_ssm_dt_bias_f, m_ssm_d_f=m_ssm_d_f, m_ssm_a_log_b=m_ssm_a_log_b, m_ssm_dt_bias_b=m_ssm_dt_bias_b, m_ssm_d_b=m_ssm_d_b, m_ssm_norm_g=m_ssm_norm_g, m_ssm_out_w=m_ssm_out_w, m_conf_pw1_w=m_conf_pw1_w, m_conf_pw1_b=m_conf_pw1_b, m_conf_dw_w=m_conf_dw_w, m_conf_dw_b=m_conf_dw_b, m_conf_ln_g=m_conf_ln_g, m_conf_ln_b=m_conf_ln_b, m_conf_pw2_w=m_conf_pw2_w, m_conf_pw2_b=m_conf_pw2_b, v_c_ctx=v_c_ctx, v_mod_w=v_mod_w, v_mod_b=v_mod_b, v_pre_mix_g=v_pre_mix_g, v_post_mix_g=v_post_mix_g, v_pre_mlp_g=v_pre_mlp_g, v_post_mlp_g=v_post_mlp_g, v_mlp_w1=v_mlp_w1, v_mlp_w2=v_mlp_w2, v_ssm_in_w=v_ssm_in_w, v_ssm_conv_w=v_ssm_conv_w, v_ssm_conv_b=v_ssm_conv_b, v_ssm_a_log_f=v_ssm_a_log_f, v_ssm_dt_bias_f=v_ssm_dt_bias_f, v_ssm_d_f=v_ssm_d_f, v_ssm_a_log_b=v_ssm_a_log_b, v_ssm_dt_bias_b=v_ssm_dt_bias_b, v_ssm_d_b=v_ssm_d_b, v_ssm_norm_g=v_ssm_norm_g, v_ssm_out_w=v_ssm_out_w, v_conf_pw1_w=v_conf_pw1_w, v_conf_pw1_b=v_conf_pw1_b, v_conf_dw_w=v_conf_dw_w, v_conf_dw_b=v_conf_dw_b, v_conf_ln_g=v_conf_ln_g, v_conf_ln_b=v_conf_ln_b, v_conf_pw2_w=v_conf_pw2_w, v_conf_pw2_b=v_conf_pw2_b)
    weights = {n: given[n] for n in TWIN_WEIGHTS}
    shared = {n: given[n] for n in SHARED_INPUTS}
    per_example = {n: given[n] for n in ['x', 'c', 'ctx']}
    grad_fn = _jax.value_and_grad(_loss, argnums=(0, 1))

    def one_microbatch(ex, loss_target):
        ex = dict(ex)
        diff = ex.pop(TWIN_DIFF_INPUT)
        return grad_fn(weights, diff, {**shared, **ex}, loss_target)

    if N_MICROBATCH == 1:
        loss, (grad_w, grad_x) = one_microbatch(per_example, given["loss_target"])
    else:
        def body(carry, xs):
            loss_sum, grad_sum = carry
            l_k, (gw_k, gx_k) = one_microbatch(xs[0], xs[1])
            with _jax.named_scope("update"):
                return (loss_sum + l_k, _jax.tree.map(_jnp.add, grad_sum, gw_k)), gx_k

        init = (_jnp.zeros((), _jnp.float32), _jax.tree.map(_jnp.zeros_like, weights))
        (loss, grad_w), grad_x = _jax.lax.scan(body, init, (per_example, given["loss_target"]))
    with _jax.named_scope("update"):
        delta_w, new_m, new_v = {}, {}, {}
        for n in TWIN_WEIGHTS:
            delta_w[n], new_m[n], new_v[n] = _adamw(weights[n], grad_w[n], given["m_" + n], given["v_" + n])
    return (loss, grad_x, *[grad_w[n] for n in TWIN_WEIGHTS], *[delta_w[n] for n in TWIN_WEIGHTS],
            *[new_m[n] for n in TWIN_WEIGHTS], *[new_v[n] for n in TWIN_WEIGHTS])
```

```python
import functools

import jax
import jax.numpy as jnp
from jax import lax
from jax.experimental import pallas as pl
from jax.experimental.pallas import tpu as pltpu

GRID_W = 64
CHUNK = 128
N_STATE = 128
EPS = 1e-6
ADAM_LR, ADAM_B1, ADAM_B2, ADAM_EPS, ADAM_WD, ADAM_STEP = 0.001, 0.9, 0.999, 1e-08, 0.01, 10
VMEM_LIMIT = 56 * 1024 * 1024
F32, BF16 = jnp.float32, jnp.bfloat16
HI = lax.Precision.HIGHEST
MESH = pl.DeviceIdType.MESH


def _cp(n_grid):
    return pltpu.CompilerParams(dimension_semantics=("arbitrary",) * n_grid, vmem_limit_bytes=VMEM_LIMIT)


def _sds(shape, dtype):
    return jax.ShapeDtypeStruct(tuple(shape), dtype)


def _div_tile(n, target, unit=128):
    best = None
    t = unit
    while t <= min(n, target):
        if n % t == 0:
            best = t
        t += unit
    return best if best is not None else n


def _rms(x, g):
    return x * lax.rsqrt(jnp.mean(x * x, axis=-1, keepdims=True) + EPS) * g


def _silu(x):
    return x * jax.nn.sigmoid(x)


def mm(a, b, mode, out_dtype, *, bias=None, relu2=False, mul_relu=None, name, tm=768, tn=1152, tk=1024):
    if mode == "nn":
        (M, C), (_, N) = a.shape, b.shape
    elif mode == "nt":
        (M, C), (N, _) = a.shape, b.shape
    else:
        (C, M), (_, N) = a.shape, b.shape
    tm, tn, tk = _div_tile(M, tm, 128 if mode == "tn" else 8), _div_tile(N, tn), _div_tile(C, tk, 128 if mode != "tn" else 8)
    nk = C // tk
    a_spec = {"nn": pl.BlockSpec((tm, tk), lambda i, j, k: (i, k)), "nt": pl.BlockSpec((tm, tk), lambda i, j, k: (i, k)),
              "tn": pl.BlockSpec((tk, tm), lambda i, j, k: (k, i))}[mode]
    b_spec = {"nn": pl.BlockSpec((tk, tn), lambda i, j, k: (k, j)), "nt": pl.BlockSpec((tn, tk), lambda i, j, k: (j, k)),
              "tn": pl.BlockSpec((tk, tn), lambda i, j, k: (k, j))}[mode]
    dims = {"nn": (((1,), (0,)), ((), ())), "nt": (((1,), (1,)), ((), ())), "tn": (((0,), (0,)), ((), ()))}[mode]
    ins, specs = [a, b], [a_spec, b_spec]
    if bias is not None:
        ins.append(bias)
        specs.append(pl.BlockSpec((1, tn), lambda i, j, k: (0, j)))
    if mul_relu is not None:
        ins.append(mul_relu)
        specs.append(pl.BlockSpec((tm, tn), lambda i, j, k: (i, j)))
    o_spec = pl.BlockSpec((tm, tn), lambda i, j, k: (i, j))
    outs, out_specs = [_sds((M, N), out_dtype)], [o_spec]
    if relu2:
        outs.append(_sds((M, N), BF16))
        out_specs.append(o_spec)

    def body(*refs):
        a_ref, b_ref = refs[0], refs[1]
        pos = 2
        bias_ref = mr_ref = None
        if bias is not None:
            bias_ref = refs[pos]
            pos += 1
        if mul_relu is not None:
            mr_ref = refs[pos]
            pos += 1
        o_ref = refs[pos]
        o2_ref = refs[pos + 1] if relu2 else None
        acc_ref = refs[-1]
        k = pl.program_id(2)

        @pl.when(k == 0)
        def _():
            acc_ref[...] = jnp.zeros_like(acc_ref)

        acc_ref[...] += lax.dot_general(a_ref[...].astype(BF16), b_ref[...].astype(BF16), dims, preferred_element_type=F32)

        @pl.when(k == nk - 1)
        def _():
            r = acc_ref[...]
            if bias_ref is not None:
                r = r + bias_ref[...]
            if mr_ref is not None:
                r = r * (2.0 * jnp.maximum(mr_ref[...], 0.0))
            o_ref[...] = r.astype(o_ref.dtype)
            if o2_ref is not None:
                q = jnp.maximum(r, 0.0)
                o2_ref[...] = (q * q).astype(BF16)

    res = pl.pallas_call(body, out_shape=outs, grid=(M // tm, N // tn, nk), in_specs=specs, out_specs=out_specs,
                         scratch_shapes=[pltpu.VMEM((tm, tn), F32)], compiler_params=_cp(3), name=name)(*ins)
    return res if relu2 else res[0]


def _seg_spec(D):
    return pl.BlockSpec((None, 1, D), lambda i: (jnp.minimum(i, 1), 0, 0))


def _prenorm_fn(h, g, sh, sc):
    return _rms(h, g) * (1.0 + sc) + sh


def prenorm_fwd(h, g, sh, sc, tr, name):
    T, D = h.shape
    row = pl.BlockSpec((tr, D), lambda i: (i, 0))
    vec = pl.BlockSpec((1, D), lambda i: (0, 0))

    def body(h_ref, g_ref, sh_ref, sc_ref, u_ref):
        u_ref[...] = _prenorm_fn(h_ref[...], g_ref[...], sh_ref[...], sc_ref[...]).astype(BF16)

    return pl.pallas_call(body, out_shape=_sds((T, D), BF16), grid=(T // tr,), in_specs=[row, vec, _seg_spec(D), _seg_spec(D)],
                          out_specs=row, compiler_params=_cp(1), name=name)(h, g, sh, sc)


def _acc(ref, val, first):
    @pl.when(first)
    def _():
        ref[...] = val

    @pl.when(jnp.logical_not(first))
    def _():
        ref[...] += val


def prenorm_bwd(h, g, sh, sc, du, G, tr, name):
    T, D = h.shape
    row = pl.BlockSpec((tr, D), lambda i: (i, 0))
    vec = pl.BlockSpec((1, D), lambda i: (0, 0))

    def body(h_ref, g_ref, sh_ref, sc_ref, du_ref, G_ref, Go_ref, dg_ref, dsh_ref, dsc_ref):
        i = pl.program_id(0)
        _, vjp = jax.vjp(_prenorm_fn, h_ref[...], g_ref[...], sh_ref[...], sc_ref[...])
        dh, dg, dsh, dsc = vjp(du_ref[...].astype(F32))
        Go_ref[...] = G_ref[...] + dh
        _acc(dg_ref, dg, i == 0)
        _acc(dsh_ref, dsh, i <= 1)
        _acc(dsc_ref, dsc, i <= 1)

    return pl.pallas_call(
        body, out_shape=[_sds((T, D), F32), _sds((1, D), F32), _sds((2, 1, D), F32), _sds((2, 1, D), F32)], grid=(T // tr,),
        in_specs=[row, vec, _seg_spec(D), _seg_spec(D), row, row], out_specs=[row, vec, _seg_spec(D), _seg_spec(D)],
        compiler_params=_cp(1), name=name)(h, g, sh, sc, du, G)


def _post_fn(y, gp, gate):
    return gate * _rms(y, gp)


def post_fwd(h, y, gp, gate, tr, name):
    T, D = h.shape
    row = pl.BlockSpec((tr, D), lambda i: (i, 0))
    vec = pl.BlockSpec((1, D), lambda i: (0, 0))

    def body(h_ref, y_ref, gp_ref, gate_ref, o_ref):
        o_ref[...] = h_ref[...] + _post_fn(y_ref[...], gp_ref[...], gate_ref[...])

    return pl.pallas_call(body, out_shape=_sds((T, D), F32), grid=(T // tr,), in_specs=[row, row, vec, _seg_spec(D)],
                          out_specs=row, compiler_params=_cp(1), name=name)(h, y, gp, gate)


def post_bwd(y, gp, gate, G, tr, name):
    T, D = y.shape
    row = pl.BlockSpec((tr, D), lambda i: (i, 0))
    vec = pl.BlockSpec((1, D), lambda i: (0, 0))

    def body(y_ref, gp_ref, gate_ref, G_ref, dy_ref, dgp_ref, dgate_ref, dsum_ref):
        i = pl.program_id(0)
        _, vjp = jax.vjp(_post_fn, y_ref[...], gp_ref[...], gate_ref[...])
        dy, dgp, dgate = vjp(G_ref[...])
        dy_ref[...] = dy.astype(BF16)
        _acc(dgp_ref, dgp, i == 0)
        _acc(dgate_ref, dgate, i <= 1)
        _acc(dsum_ref, jnp.sum(dy, axis=0, keepdims=True), i == 0)

    return pl.pallas_call(
        body, out_shape=[_sds((T, D), BF16), _sds((1, D), F32), _sds((2, 1, D), F32), _sds((1, D), F32)], grid=(T // tr,),
        in_specs=[row, vec, _seg_spec(D), row], out_specs=[row, vec, _seg_spec(D), vec], compiler_params=_cp(1), name=name)(y, gp, gate, G)


def loss_head(h, target, tr, name):
    T, D = h.shape
    row = pl.BlockSpec((tr, D), lambda i: (i, 0))
    trow = pl.BlockSpec((tr, D), lambda i: (jnp.maximum(i - 1, 0), 0))

    def body(h_ref, t_ref, loss_ref, G_ref):
        i = pl.program_id(0)

        @pl.when(i == 0)
        def _():
            loss_ref[...] = jnp.zeros_like(loss_ref)
            G_ref[...] = jnp.zeros_like(G_ref)

        @pl.when(i > 0)
        def _():
            e = h_ref[...] - t_ref[...]
            G_ref[...] = e * (1.0 / D)
            loss_ref[...] += jnp.sum(e * e) * (0.5 / D)

    return pl.pallas_call(body, out_shape=[_sds((8, 128), F32), _sds((T, D), F32)], grid=(T // tr,), in_specs=[row, trow],
                          out_specs=[pl.BlockSpec((8, 128), lambda i: (0, 0)), row], compiler_params=_cp(1), name=name)(h, target)


def _halo_specs(tr, tc, T, col0, lead=()):
    n8 = tr // 8
    nl = len(lead)
    cur = pl.BlockSpec(lead + (tr, tc), lambda j, i: (0,) * nl + (i, col0 + j))
    prev = pl.BlockSpec(lead + (8, tc), lambda j, i: (0,) * nl + (jnp.maximum(i * n8 - 1, 0), col0 + j))
    nxt = pl.BlockSpec(lead + (8, tc), lambda j, i: (0,) * nl + (jnp.minimum((i + 1) * n8, T // 8 - 1), col0 + j))
    return [cur, prev, nxt]


def _with_halo(cur, prev, nxt, i, nt):
    keep_prev = (i >= 2).astype(cur.dtype)
    keep_next = jnp.logical_and(i >= 1, i < nt - 1).astype(cur.dtype)
    return jnp.concatenate([prev * keep_prev, cur, nxt * keep_next], axis=0)


def _shift_rows(ext, o, tr):
    n = ext.shape[0]
    return pltpu.roll(ext, (-o) % n, 0)[8:8 + tr]


def conv5_fwd(proj, w, b, col0, tr, tc, name):
    T = proj.shape[0]
    K, C = w.shape
    nt = T // tr

    def body(x_ref, xp_ref, xn_ref, w_ref, b_ref, pre_ref, act_ref):
        i = pl.program_id(1)
        ext = _with_halo(x_ref[...], xp_ref[...], xn_ref[...], i, nt)
        wv = w_ref[...]
        acc = jnp.zeros((tr, tc), F32) + b_ref[...]
        for k in range(K):
            acc = acc + wv[k:k + 1, :] * _shift_rows(ext, k - K // 2, tr)
        pre_ref[...] = acc
        act_ref[...] = _silu(acc)

    out = pl.BlockSpec((tr, tc), lambda j, i: (i, j))
    return pl.pallas_call(
        body, out_shape=[_sds((T, C), F32), _sds((T, C), F32)], grid=(C // tc, nt),
        in_specs=_halo_specs(tr, tc, T, col0) + [pl.BlockSpec((K, tc), lambda j, i: (0, j)), pl.BlockSpec((1, tc), lambda j, i: (0, j))],
        out_specs=[out, out], compiler_params=_cp(2), name=name)(proj, proj, proj, w, b)


def conv5_bwd(dact, pre, proj, w, colp, colx, tr, tc, name):
    _, T, Cp = dact.shape
    K = w.shape[0]
    nt = T // tr

    def body(d_ref, dp_ref, dn_ref, p_ref, pp_ref, pn_ref, x_ref, xp_ref, xn_ref, w_ref, dx_ref, dw_ref, db_ref):
        i = pl.program_id(1)

        def dpre_of(d, p):
            s = jax.nn.sigmoid(p)
            return (d[0] + d[1]) * (s * (1.0 + p * (1.0 - s)))

        dext = _with_halo(dpre_of(d_ref[...], p_ref[...]), dpre_of(dp_ref[...], pp_ref[...]), dpre_of(dn_ref[...], pn_ref[...]), i, nt)
        xext = _with_halo(x_ref[...], xp_ref[...], xn_ref[...], i, nt)
        dcur = dext[8:8 + tr]
        wv = w_ref[...]
        dx = jnp.zeros((tr, tc), F32)
        for k in range(K):
            o = k - K // 2
            dx = dx + wv[k:k + 1, :] * _shift_rows(dext, -o, tr)
            _acc(dw_ref.at[k:k + 1, :], jnp.sum(dcur * _shift_rows(xext, o, tr), axis=0, keepdims=True), i == 0)
        dx_ref[...] = dx.astype(BF16)
        _acc(db_ref, jnp.sum(dcur, axis=0, keepdims=True), i == 0)

    out = pl.BlockSpec((tr, tc), lambda j, i: (i, j))
    return pl.pallas_call(
        body, out_shape=[_sds((T, Cp), BF16), _sds((K, Cp), F32), _sds((1, Cp), F32)], grid=(Cp // tc, nt),
        in_specs=_halo_specs(tr, tc, T, 0, lead=(2,)) + _halo_specs(tr, tc, T, colp) + _halo_specs(tr, tc, T, colx)
        + [pl.BlockSpec((K, tc), lambda j, i: (0, colp + j))],
        out_specs=[out, pl.BlockSpec((K, tc), lambda j, i: (0, j)), pl.BlockSpec((1, tc), lambda j, i: (0, j))],
        compiler_params=_cp(2), name=name)(dact, dact, dact, pre, pre, pre, proj, proj, proj, w)


def _ssd_chunk(xg, bg, cg, dtr, hin, bias, alog, dsk, rev):
    Q = xg.shape[0]
    nh = dtr.shape[1]
    P = xg.shape[1] // nh
    dt = jax.nn.softplus(dtr + bias)
    da = dt * (-jnp.exp(alog))
    r_i = lax.broadcasted_iota(jnp.int32, (Q, Q), 0)
    c_i = lax.broadcasted_iota(jnp.int32, (Q, Q), 1)
    mask = jnp.where(rev, c_i - r_i, r_i - c_i) >= 0
    cs = jnp.dot(mask.astype(F32), da, precision=HI, preferred_element_type=F32)
    cs_t = cs.T
    tot = jnp.where(rev, cs[0:1, :], cs[Q - 1:Q, :])
    scores = lax.dot_general(cg.astype(BF16), bg.astype(BF16), (((1,), (1,)), ((), ())), preferred_element_type=F32)
    ys, hs = [], []
    for j in range(nh):
        csj = cs[:, j:j + 1]
        xj = xg[:, j * P:(j + 1) * P]
        hj = hin[j * P:(j + 1) * P, :]
        xdt = xj * dt[:, j:j + 1]
        decay = jnp.exp(jnp.where(mask, csj - cs_t[j:j + 1, :], -jnp.inf))
        y = jnp.dot((scores * decay).astype(BF16), xdt.astype(BF16), preferred_element_type=F32)
        y = y + lax.dot_general(cg.astype(BF16), hj.astype(BF16), (((1,), (1,)), ((), ())), preferred_element_type=F32) * jnp.exp(csj)
        ys.append(y + dsk[:, j:j + 1] * xj)
        totj = tot[:, j:j + 1]
        st = lax.dot_general((xdt * jnp.exp(totj - csj)).astype(BF16), bg.astype(BF16), (((0,), (0,)), ((), ())),
                             preferred_element_type=F32)
        hs.append(jnp.exp(totj) * hj + st)
    return jnp.concatenate(ys, axis=1), jnp.concatenate(hs, axis=0)


def _ssd_specs(Q, P8, N, ncc, NC, xcol_b, xcol_c, back):
    def chunk(d, s):
        s = (NC - 1 - s) if back else s
        return jnp.where(d == 0, s, jnp.where(s < ncc, ncc - 1 - s, ncc + NC - 1 - s))

    def step(s):
        return (NC - 1 - s) if back else s

    x = pl.BlockSpec((Q, P8), lambda d, g, s: (chunk(d, s), g))
    bsp = pl.BlockSpec((Q, N), lambda d, g, s: (chunk(d, s), xcol_b + g))
    csp = pl.BlockSpec((Q, N), lambda d, g, s: (chunk(d, s), xcol_c + g))
    dt = pl.BlockSpec((None, None, Q, 8), lambda d, g, s: (d, g, chunk(d, s), 0))
    par = pl.BlockSpec((None, None, 1, 8), lambda d, g, s: (d, g, 0, 0))
    hst = pl.BlockSpec((None, None, None, P8, N), lambda d, g, s: (d, g, step(s), 0, 0))
    yd = pl.BlockSpec((None, Q, P8), lambda d, g, s: (d, chunk(d, s), g))
    bd = pl.BlockSpec((None, Q, N), lambda d, g, s: (d, chunk(d, s), g))
    return x, bsp, csp, dt, par, hst, yd, bd


def ssd_fwd(act, dtr, bias, alog, dsk, d_inner, ncc, name):
    T = act.shape[0]
    G = dtr.shape[1]
    Q, N = CHUNK, N_STATE
    NC = T // Q
    P8 = d_inner // G
    x, bsp, csp, dt, par, hst, yd, _ = _ssd_specs(Q, P8, N, ncc, NC, d_inner // N, d_inner // N + G, False)

    def body(x_ref, b_ref, c_ref, dt_ref, bias_ref, alog_ref, dsk_ref, y_ref, h_ref, st_ref):
        d, s = pl.program_id(0), pl.program_id(2)

        @pl.when(s == 0)
        def _():
            st_ref[...] = jnp.zeros_like(st_ref)

        hin = st_ref[...]
        h_ref[...] = hin
        y, ho = _ssd_chunk(x_ref[...], b_ref[...], c_ref[...], dt_ref[...], hin, bias_ref[...], alog_ref[...], dsk_ref[...], d == 1)
        y_ref[...] = y
        st_ref[...] = ho

    return pl.pallas_call(
        body, out_shape=[_sds((2, T, d_inner), F32), _sds((2, G, NC, P8, N), F32)], grid=(2, G, NC),
        in_specs=[x, bsp, csp, dt, par, par, par], out_specs=[yd, hst], scratch_shapes=[pltpu.VMEM((P8, N), F32)],
        compiler_params=_cp(3), name=name)(act, act, act, dtr, bias, alog, dsk)


def ssd_bwd(act, dtr, bias, alog, dsk, hsave, dy, d_inner, ncc, name):
    T = act.shape[0]
    G = dtr.shape[1]
    Q, N = CHUNK, N_STATE
    NC = T // Q
    P8 = d_inner // G
    x, bsp, csp, dt, par, hst, yd, bd = _ssd_specs(Q, P8, N, ncc, NC, d_inner // N, d_inner // N + G, True)
    dysp = pl.BlockSpec((Q, P8), x.index_map)

    def body(x_ref, b_ref, c_ref, dt_ref, bias_ref, alog_ref, dsk_ref, h_ref, dy_ref,
             dx_ref, db_ref, dc_ref, ddt_ref, dbias_ref, dalog_ref, ddsk_ref, dh_ref):
        d, s = pl.program_id(0), pl.program_id(2)

        @pl.when(s == 0)
        def _():
            dh_ref[...] = jnp.zeros_like(dh_ref)

        args = (x_ref[...], b_ref[...], c_ref[...], dt_ref[...], h_ref[...], bias_ref[...], alog_ref[...], dsk_ref[...])

        _, vjp = jax.vjp(functools.partial(_ssd_chunk, rev=d == 1), *args)
        dx, db, dc, ddt, dhin, dbias, dalog, ddsk = vjp((dy_ref[...], dh_ref[...]))
        dx_ref[...] = dx
        db_ref[...] = db
        dc_ref[...] = dc
        ddt_ref[...] = ddt
        dh_ref[...] = dhin
        _acc(dbias_ref, dbias, s == 0)
        _acc(dalog_ref, dalog, s == 0)
        _acc(ddsk_ref, ddsk, s == 0)

    GN = G * N
    return pl.pallas_call(
        body,
        out_shape=[_sds((2, T, d_inner), F32), _sds((2, T, GN), F32), _sds((2, T, GN), F32), _sds(dtr.shape, F32),
                   _sds(bias.shape, F32), _sds(bias.shape, F32), _sds(bias.shape, F32)],
        grid=(2, G, NC), in_specs=[x, bsp, csp, dt, par, par, par, hst, dysp], out_specs=[yd, bd, bd, dt, par, par, par],
        scratch_shapes=[pltpu.VMEM((P8, N), F32)], compiler_params=_cp(3), name=name)(act, act, act, dtr, bias, alog, dsk, hsave, dy)


def _gnorm_fn(yf, yb, z, g):
    return _rms((yf + yb) * _silu(z), g)


def gnorm_fwd(y2, proj, g, tr, name):
    _, T, C = y2.shape
    yf = pl.BlockSpec((None, tr, C), lambda i: (0, i, 0))
    yb = pl.BlockSpec((None, tr, C), lambda i: (1, i, 0))
    row = pl.BlockSpec((tr, C), lambda i: (i, 0))
    vec = pl.BlockSpec((1, C), lambda i: (0, 0))

    def body(yf_ref, yb_ref, z_ref, g_ref, o_ref):
        o_ref[...] = _gnorm_fn(yf_ref[...], yb_ref[...], z_ref[...], g_ref[...]).astype(BF16)

    return pl.pallas_call(body, out_shape=_sds((T, C), BF16), grid=(T // tr,), in_specs=[yf, yb, row, vec], out_specs=row,
                          compiler_params=_cp(1), name=name)(y2, y2, proj, g)


def gnorm_bwd(y2, proj, g, dyn, tr, name):
    _, T, C = y2.shape
    yf = pl.BlockSpec((None, tr, C), lambda i: (0, i, 0))
    yb = pl.BlockSpec((None, tr, C), lambda i: (1, i, 0))
    row = pl.BlockSpec((tr, C), lambda i: (i, 0))
    vec = pl.BlockSpec((1, C), lambda i: (0, 0))

    def body(yf_ref, yb_ref, z_ref, g_ref, d_ref, dy_ref, dz_ref, dg_ref):
        i = pl.program_id(0)
        _, vjp = jax.vjp(_gnorm_fn, yf_ref[...], yb_ref[...], z_ref[...], g_ref[...])
        dyf, _, dz, dg = vjp(d_ref[...].astype(F32))
        dy_ref[...] = dyf
        dz_ref[...] = dz.astype(BF16)
        _acc(dg_ref, dg, i == 0)

    return pl.pallas_call(body, out_shape=[_sds((T, C), F32), _sds((T, C), BF16), _sds((1, C), F32)], grid=(T // tr,),
                          in_specs=[yf, yb, row, vec, row], out_specs=[row, row, vec], compiler_params=_cp(1), name=name)(y2, y2, proj, g, dyn)


def _glu_fn(a):
    D = a.shape[1] // 2
    return a[:, :D] * jax.nn.sigmoid(a[:, D:])


def _ln_swish_fn(v, g, b):
    mu = jnp.mean(v, axis=-1, keepdims=True)
    xc = v - mu
    var = jnp.mean(xc * xc, axis=-1, keepdims=True)
    y = xc * lax.rsqrt(var + EPS) * g + b
    return y * jax.nn.sigmoid(y)


def _seg_pos(tr, seg, i):
    p = lax.broadcasted_iota(jnp.int32, (tr, 1), 0)
    s = jnp.where(i == 0, tr, seg)
    return p & (s - 1), s


def _dw_taps(v, w, pos, s, sign):
    tr = v.shape[0]
    K = w.shape[0]
    acc = jnp.zeros_like(v)
    for k in range(K):
        o = sign * (k - K // 2)
        q = pos + o
        ok = jnp.logical_and(q >= 0, q < s).astype(v.dtype)
        acc = acc + w[k:k + 1, :] * (pltpu.roll(v, (-o) % tr, 0) * ok)
    return acc


def _lane_blocks(v, ref):
    for c in range(v.shape[1] // 128):
        ref[c] = v[:, c * 128:(c + 1) * 128]


def _from_lane_blocks(ref):
    return jnp.concatenate([ref[c] for c in range(ref.shape[0])], axis=1)


def confmid_fwd(a, w3, b, lg, lb, seg, tr, name):
    T, D2 = a.shape
    D = D2 // 2
    nb, K, _ = w3.shape
    vec = pl.BlockSpec((1, D), lambda i: (0, 0))

    def body(a_ref, w_ref, b_ref, lg_ref, lb_ref, o_ref, s0_ref, s1_ref):
        i = pl.program_id(0)
        pos, s = _seg_pos(tr, seg, i)
        _lane_blocks(_glu_fn(a_ref[...]), s0_ref)

        def blk(c, carry):
            s1_ref[c] = _dw_taps(s0_ref[c], w_ref[c], pos, s, 1)
            return carry

        lax.fori_loop(0, nb, blk, 0)
        o_ref[...] = _ln_swish_fn(_from_lane_blocks(s1_ref) + b_ref[...], lg_ref[...], lb_ref[...]).astype(BF16)

    return pl.pallas_call(
        body, out_shape=_sds((T, D), BF16), grid=(T // tr,),
        in_specs=[pl.BlockSpec((tr, D2), lambda i: (i, 0)), pl.BlockSpec((nb, K, 128), lambda i: (0, 0, 0)), vec, vec, vec],
        out_specs=pl.BlockSpec((tr, D), lambda i: (i, 0)), scratch_shapes=[pltpu.VMEM((nb, tr, 128), F32)] * 2,
        compiler_params=_cp(1), name=name)(a, w3, b, lg, lb)


def confmid_bwd(a, w3, b, lg, lb, dv, seg, tr, name):
    T, D2 = a.shape
    D = D2 // 2
    nb, K, _ = w3.shape
    vec = pl.BlockSpec((1, D), lambda i: (0, 0))
    vec2 = pl.BlockSpec((1, D2), lambda i: (0, 0))
    wsp = pl.BlockSpec((nb, K, 128), lambda i: (0, 0, 0))

    def body(a_ref, w_ref, b_ref, lg_ref, lb_ref, dv_ref, da_ref, dsum_ref, dw_ref, db_ref, dlg_ref, dlb_ref, s0_ref, s1_ref, s2_ref):
        i = pl.program_id(0)
        first = i == 0
        pos, s = _seg_pos(tr, seg, i)
        v0, glu_vjp = jax.vjp(_glu_fn, a_ref[...])
        _lane_blocks(v0, s0_ref)

        def conv(c, carry):
            s1_ref[c] = _dw_taps(s0_ref[c], w_ref[c], pos, s, 1)
            return carry

        lax.fori_loop(0, nb, conv, 0)
        _, ln_vjp = jax.vjp(_ln_swish_fn, _from_lane_blocks(s1_ref) + b_ref[...], lg_ref[...], lb_ref[...])
        dv1, dlg, dlb = ln_vjp(dv_ref[...].astype(F32))
        _acc(db_ref, jnp.sum(dv1, axis=0, keepdims=True), first)
        _acc(dlg_ref, dlg, first)
        _acc(dlb_ref, dlb, first)
        _lane_blocks(dv1, s2_ref)

        @pl.when(first)
        def _():
            dw_ref[...] = jnp.zeros_like(dw_ref)

        def conv_t(c, carry):
            d1, v0c = s2_ref[c], s0_ref[c]
            s1_ref[c] = _dw_taps(d1, w_ref[c], pos, s, -1)
            for k in range(K):
                o = k - K // 2
                q = pos + o
                ok = jnp.logical_and(q >= 0, q < s).astype(F32)
                dw_ref[c, k:k + 1, :] += jnp.sum(d1 * (pltpu.roll(v0c, (-o) % tr, 0) * ok), axis=0, keepdims=True)
            return carry

        lax.fori_loop(0, nb, conv_t, 0)
        (da,) = glu_vjp(_from_lane_blocks(s1_ref))
        da_ref[...] = da.astype(BF16)
        _acc(dsum_ref, jnp.sum(da, axis=0, keepdims=True), first)

    return pl.pallas_call(
        body, out_shape=[_sds((T, D2), BF16), _sds((1, D2), F32), _sds((nb, K, 128), F32), _sds((1, D), F32), _sds((1, D), F32), _sds((1, D), F32)],
        grid=(T // tr,), in_specs=[pl.BlockSpec((tr, D2), lambda i: (i, 0)), wsp, vec, vec, vec, pl.BlockSpec((tr, D), lambda i: (i, 0))],
        out_specs=[pl.BlockSpec((tr, D2), lambda i: (i, 0)), vec2, wsp, vec, vec, vec], scratch_shapes=[pltpu.VMEM((nb, tr, 128), F32)] * 3,
        compiler_params=_cp(1), name=name)(a, w3, b, lg, lb, dv)


def mod_fwd(rows, w, bsl, name):
    Ly, D, Nc = w.shape
    tn = _div_tile(Nc, 512)

    def body(r_ref, w_ref, b_ref, o_ref):
        s = _silu(r_ref[...]).astype(BF16)
        o_ref[...] = jnp.dot(s, w_ref[...].astype(BF16), preferred_element_type=F32) + b_ref[...]

    return pl.pallas_call(
        body, out_shape=_sds((Ly, 16, Nc), F32), grid=(Ly, Nc // tn),
        in_specs=[pl.BlockSpec((16, D), lambda l, j: (0, 0)), pl.BlockSpec((None, D, tn), lambda l, j: (l, 0, j)),
                  pl.BlockSpec((None, 1, tn), lambda l, j: (l, 0, j))],
        out_specs=pl.BlockSpec((None, 16, tn), lambda l, j: (l, 0, j)), compiler_params=_cp(2), name=name)(rows, w, bsl)


def mod_bwd(rows, w, dm, name):
    Ly, D, Nc = w.shape
    tn = _div_tile(Nc, 512)
    nj = Nc // tn

    def body(r_ref, w_ref, dm_ref, dw_ref, ds_ref):
        j = pl.program_id(1)
        s = _silu(r_ref[...]).astype(BF16)
        dmv = dm_ref[...].astype(BF16)
        dw_ref[...] = lax.dot_general(s, dmv, (((0,), (0,)), ((), ())), preferred_element_type=F32)
        _acc(ds_ref, lax.dot_general(dmv, w_ref[...].astype(BF16), (((1,), (1,)), ((), ())), preferred_element_type=F32), j == 0)

    return pl.pallas_call(
        body, out_shape=[_sds((Ly, D, Nc), F32), _sds((Ly, 16, D), F32)], grid=(Ly, nj),
        in_specs=[pl.BlockSpec((16, D), lambda l, j: (0, 0)), pl.BlockSpec((None, D, tn), lambda l, j: (l, 0, j)),
                  pl.BlockSpec((None, 16, tn), lambda l, j: (l, 0, j))],
        out_specs=[pl.BlockSpec((None, D, tn), lambda l, j: (l, 0, j)), pl.BlockSpec((None, 16, D), lambda l, j: (l, 0, 0))],
        compiler_params=_cp(2), name=name)(rows, w, dm)


def silu_grad(dsc, c, name):
    def body(d_ref, c_ref, o_ref):
        x = c_ref[...]
        s = jax.nn.sigmoid(x)
        o_ref[...] = d_ref[...] * (s * (1.0 + x * (1.0 - s)))

    return pl.pallas_call(body, out_shape=_sds(c.shape, F32), name=name)(dsc, c)


def _coords():
    return lax.axis_index("x"), lax.axis_index("y"), lax.axis_index("c")


def _flip(v, bit):
    return 1 - v if bit else v


def allgather8(x, name):
    R, C = x.shape

    def body(x_ref, o_ref, send_sems, recv_sems, local_sem):
        mx, my, mc = _coords()
        me = 4 * mx + 2 * my + mc
        mine = pltpu.make_async_copy(x_ref, o_ref.at[me], local_sem)
        mine.start()
        copies = []
        for k in range(1, 8):
            px, py, pc = _flip(mx, k & 4), _flip(my, k & 2), _flip(mc, k & 1)
            cp = pltpu.make_async_remote_copy(src_ref=x_ref, dst_ref=o_ref.at[me], send_sem=send_sems.at[k - 1],
                                              recv_sem=recv_sems.at[k - 1], device_id=(px, py, pc), device_id_type=MESH)
            cp.start()
            copies.append((cp, 4 * px + 2 * py + pc))
        for k, (cp, peer) in enumerate(copies):
            pltpu.make_async_remote_copy(src_ref=x_ref, dst_ref=o_ref.at[peer], send_sem=send_sems.at[k], recv_sem=recv_sems.at[k],
                                         device_id=(mx, my, mc), device_id_type=MESH).wait_recv()
        for cp, _ in copies:
            cp.wait_send()
        mine.wait()

    return pl.pallas_call(
        body, out_shape=_sds((8, R, C), F32), in_specs=[pl.BlockSpec(memory_space=pltpu.VMEM)],
        out_specs=pl.BlockSpec(memory_space=pltpu.VMEM),
        scratch_shapes=[pltpu.SemaphoreType.DMA((7,)), pltpu.SemaphoreType.DMA((7,)), pltpu.SemaphoreType.DMA],
        compiler_params=pltpu.CompilerParams(vmem_limit_bytes=VMEM_LIMIT), name=name)(x)


def chip_exchange(arrs, name, bcast=False):
    n = len(arrs)

    def src(ref, k):
        return ref if bcast else ref.at[k]

    def body(*refs):
        ins, outs = refs[:n], refs[n:2 * n]
        send_sems, recv_sems, local_sems = refs[2 * n:]
        mx, my, mc = _coords()
        me = 2 * mx + my
        started = []
        for a in range(n):
            mine = pltpu.make_async_copy(src(ins[a], me), outs[a].at[me], local_sems.at[a])
            mine.start()
            started.append(mine)
        sends = []
        for a in range(n):
            for k in range(1, 4):
                px, py = _flip(mx, k & 2), _flip(my, k & 1)
                cp = pltpu.make_async_remote_copy(src_ref=src(ins[a], 2 * px + py), dst_ref=outs[a].at[me], send_sem=send_sems.at[3 * a + k - 1],
                                                  recv_sem=recv_sems.at[3 * a + k - 1], device_id=(px, py, mc), device_id_type=MESH)
                cp.start()
                sends.append((cp, a, k, 2 * px + py))
        for cp, a, k, peer in sends:
            pltpu.make_async_remote_copy(src_ref=src(ins[a], me), dst_ref=outs[a].at[peer], send_sem=send_sems.at[3 * a + k - 1],
                                         recv_sem=recv_sems.at[3 * a + k - 1], device_id=(mx, my, mc), device_id_type=MESH).wait_recv()
        for cp, *_ in sends:
            cp.wait_send()
        for mine in started:
            mine.wait()

    hbm = pl.BlockSpec(memory_space=pl.ANY)
    return pl.pallas_call(
        body, out_shape=[_sds(((4,) + a.shape) if bcast else a.shape, a.dtype) for a in arrs], in_specs=[hbm] * n, out_specs=[hbm] * n,
        scratch_shapes=[pltpu.SemaphoreType.DMA((3 * n,)), pltpu.SemaphoreType.DMA((3 * n,)), pltpu.SemaphoreType.DMA((n,))],
        name=name)(*arrs)


def sibling_exchange(arrs, name):
    n = len(arrs)

    def body(*refs):
        ins, outs = refs[:n], refs[n:2 * n]
        send_sems, recv_sems = refs[2 * n:]
        mx, my, mc = _coords()
        cps = []
        for a in range(n):
            cp = pltpu.make_async_remote_copy(src_ref=ins[a], dst_ref=outs[a], send_sem=send_sems.at[a], recv_sem=recv_sems.at[a],
                                              device_id=(mx, my, 1 - mc), device_id_type=MESH)
            cp.start()
            cps.append(cp)
        for cp in cps:
            cp.wait()

    hbm = pl.BlockSpec(memory_space=pl.ANY)
    return pl.pallas_call(body, out_shape=[_sds(a.shape, a.dtype) for a in arrs], in_specs=[hbm] * n, out_specs=[hbm] * n,
                          scratch_shapes=[pltpu.SemaphoreType.DMA((n,)), pltpu.SemaphoreType.DMA((n,))], name=name)(*arrs)


def sum_slabs(x, name, tr=256):
    n, R, C = x.shape
    tr = _div_tile(R, tr, 8)

    def body(x_ref, o_ref):
        acc = x_ref[0].astype(F32)
        for k in range(1, n):
            acc = acc + x_ref[k].astype(F32)
        o_ref[...] = acc

    return pl.pallas_call(body, out_shape=_sds((R, C), F32), grid=(R // tr,), in_specs=[pl.BlockSpec((n, tr, C), lambda i: (0, i, 0))],
                          out_specs=pl.BlockSpec((tr, C), lambda i: (i, 0)), compiler_params=_cp(1), name=name)(x)


def adamw(w, g1, g2, m, v, name, tr=256):
    R, C = w.shape
    tr = _div_tile(R, tr, 8)
    row = pl.BlockSpec((tr, C), lambda i: (i, 0))
    ins = [w, g1] + ([g2] if g2 is not None else []) + [m, v]

    def body(*refs):
        w_ref, g_ref = refs[0], refs[1]
        g = g_ref[...]
        pos = 2
        if g2 is not None:
            g = g + refs[2][...]
            pos = 3
        m_ref, v_ref, go_ref, d_ref, mo_ref, vo_ref = refs[pos:]
        mn = ADAM_B1 * m_ref[...] + (1.0 - ADAM_B1) * g
        vn = ADAM_B2 * v_ref[...] + (1.0 - ADAM_B2) * (g * g)
        m_hat = mn / (1.0 - ADAM_B1 ** ADAM_STEP)
        v_hat = vn / (1.0 - ADAM_B2 ** ADAM_STEP)
        go_ref[...] = g
        d_ref[...] = -ADAM_LR * (m_hat / (jnp.sqrt(v_hat) + ADAM_EPS) + ADAM_WD * w_ref[...])
        mo_ref[...] = mn
        vo_ref[...] = vn

    return pl.pallas_call(body, out_shape=[_sds((R, C), F32)] * 4, grid=(R // tr,), in_specs=[row] * len(ins), out_specs=[row] * 4,
                          compiler_params=_cp(1), name=name)(*ins)


W_NAMES = ("c_ctx", "mod_w", "mod_b", "pre_mix_g", "post_mix_g", "pre_mlp_g", "post_mlp_g", "mlp_w1", "mlp_w2", "ssm_in_w",
           "ssm_conv_w", "ssm_conv_b", "ssm_a_log_f", "ssm_dt_bias_f", "ssm_d_f", "ssm_a_log_b", "ssm_dt_bias_b", "ssm_d_b",
           "ssm_norm_g", "ssm_out_w", "conf_pw1_w", "conf_pw1_b", "conf_dw_w", "conf_dw_b", "conf_ln_g", "conf_ln_b",
           "conf_pw2_w", "conf_pw2_b")
BIG = {"mlp_w1": "col", "mlp_w2": "row", "ssm_in_w": "col", "ssm_out_w": "row", "conf_pw1_w": "col", "conf_pw2_w": "row"}
SMALL_SHARDED = ("ssm_conv_w", "conf_pw1_b", "conf_dw_w", "conf_dw_b", "conf_ln_g", "conf_ln_b", "conf_pw2_b")
PACK_W = 1024


def _pack(arrs):
    flat = jnp.concatenate([a.reshape(-1).astype(F32) for a in arrs])
    n = flat.shape[0]
    tot = -(-n // (8 * PACK_W)) * (8 * PACK_W)
    return jnp.pad(flat, (0, tot - n)).reshape(tot // PACK_W, PACK_W)


def _unpack(buf, shapes):
    lead = buf.shape[:-2]
    flat = buf.reshape(lead + (-1,))
    out, off = [], 0
    for shp in shapes:
        n = 1
        for d in shp:
            n *= d
        out.append(flat[..., off:off + n].reshape(lead + tuple(shp)))
        off += n
    return out


def _full_from_chips(g, kind):
    if kind == "col":
        return jnp.moveaxis(g, 0, -2).reshape(g.shape[1:-1] + (4 * g.shape[-1],))
    return jnp.moveaxis(g, 0, 1).reshape((g.shape[1], 4 * g.shape[2]) + g.shape[3:])


def _chip_slabs(full, kind):
    if kind == "col":
        return jnp.moveaxis(full.reshape(full.shape[:-1] + (4, full.shape[-1] // 4)), -2, 0)
    return jnp.moveaxis(full.reshape((full.shape[0], 4, full.shape[1] // 4) + full.shape[2:]), 1, 0)


def _view2d(a):
    if a.ndim == 1:
        return a.reshape(1, -1)
    return a.reshape(-1, a.shape[-1])


def kernel(x, c, ctx, c_ctx, mod_w, mod_b, pre_mix_g, post_mix_g, pre_mlp_g, post_mlp_g, mlp_w1, mlp_w2, ssm_in_w, ssm_conv_w, ssm_conv_b, ssm_a_log_f, ssm_dt_bias_f, ssm_d_f, ssm_a_log_b, ssm_dt_bias_b, ssm_d_b, ssm_norm_g, ssm_out_w, conf_pw1_w, conf_pw1_b, conf_dw_w, conf_dw_b, conf_ln_g, conf_ln_b, conf_pw2_w, conf_pw2_b, loss_target, m_c_ctx, m_mod_w, m_mod_b, m_pre_mix_g, m_post_mix_g, m_pre_mlp_g, m_post_mlp_g, m_mlp_w1, m_mlp_w2, m_ssm_in_w, m_ssm_conv_w, m_ssm_conv_b, m_ssm_a_log_f, m_ssm_dt_bias_f, m_ssm_d_f, m_ssm_a_log_b, m_ssm_dt_bias_b, m_ssm_d_b, m_ssm_norm_g, m_ssm_out_w, m_conf_pw1_w, m_conf_pw1_b, m_conf_dw_w, m_conf_dw_b, m_conf_ln_g, m_conf_ln_b, m_conf_pw2_w, m_conf_pw2_b, v_c_ctx, v_mod_w, v_mod_b, v_pre_mix_g, v_post_mix_g, v_pre_mlp_g, v_post_mlp_g, v_mlp_w1, v_mlp_w2, v_ssm_in_w, v_ssm_conv_w, v_ssm_conv_b, v_ssm_a_log_f, v_ssm_dt_bias_f, v_ssm_d_f, v_ssm_a_log_b, v_ssm_dt_bias_b, v_ssm_d_b, v_ssm_norm_g, v_ssm_out_w, v_conf_pw1_w, v_conf_pw1_b, v_conf_dw_w, v_conf_dw_b, v_conf_ln_g, v_conf_ln_b, v_conf_pw2_w, v_conf_pw2_b):
    given = dict(locals())
    W = {n: given[n] for n in W_NAMES}
    L, D = x.shape[1], x.shape[2]
    Lc = ctx.shape[1]
    T = Lc + L
    depth = mod_w.shape[0]
    d_inner = ssm_norm_g.shape[1]
    H = ssm_a_log_f.shape[1]
    xbc = ssm_conv_b.shape[1]
    GN = (xbc - d_inner) // 2
    G = GN // N_STATE
    rows_grid = L // GRID_W
    tr = Lc
    ncc = Lc // CHUNK
    assert H == 8 * G and Lc % CHUNK == 0 and L % Lc == 0 and tr % GRID_W == 0 and tr % rows_grid == 0
    tc = _div_tile(GN, 512)
    assert d_inner % tc == 0
    mx, my, mc = _coords()
    chip = 2 * mx + my
    dev = 4 * mx + 2 * my + mc

    small_shapes = [(1, D)] + [W[n].shape for n in SMALL_SHARDED]
    got = allgather8(_pack([c] + [W[n] for n in SMALL_SHARDED]), "gather_small")
    parts = _unpack(got, small_shapes)
    c_all = parts[0].reshape(8, D)
    full_small = {n: jnp.concatenate([p[2 * k] for k in range(4)], axis=-1) for n, p in zip(SMALL_SHARDED, parts[1:])}

    cond = jnp.concatenate([c_all, c_ctx.reshape(1, D), jnp.zeros((7, D), F32)], axis=0)
    ncol = mod_w.shape[2]
    bsl = lax.dynamic_slice(mod_b, (0, chip * ncol), (depth, ncol)).reshape(depth, 1, ncol)
    m_loc = mod_fwd(cond, mod_w, bsl, "mod_fwd")
    m_all = allgather8(m_loc.reshape(depth * 16, ncol), "gather_mod").reshape(8, depth, 16, ncol)
    m_full = jnp.concatenate([m_all[2 * k] for k in range(4)], axis=-1)
    m_lat = lax.dynamic_slice(m_full, (0, dev, 0), (depth, 1, 6 * D))
    m2 = jnp.concatenate([m_full[:, 8:9], m_lat], axis=1)

    def six(i):
        return [m2[i, :, k * D:(k + 1) * D].reshape(2, 1, D) for k in range(6)]

    big_names = list(BIG)
    gathered = chip_exchange([W[n].astype(BF16) for n in big_names], "gather_weights", bcast=True)
    Wb = {n: _full_from_chips(g, BIG[n]) for n, g in zip(big_names, gathered)}

    def to_scan(u):
        lat = u[Lc:].reshape(rows_grid, GRID_W, u.shape[1]).swapaxes(0, 1).reshape(L, u.shape[1])
        return jnp.concatenate([u[:Lc], lat], axis=0)

    def from_scan(u):
        lat = u[Lc:].reshape(GRID_W, rows_grid, u.shape[1]).swapaxes(0, 1).reshape(L, u.shape[1])
        return jnp.concatenate([u[:Lc], lat], axis=0)

    def ssm_params(j):
        def two(f, b):
            return jnp.stack([f[j], b[j]]).reshape(2, G, 1, 8)
        return two(ssm_dt_bias_f, ssm_dt_bias_b), two(ssm_a_log_f, ssm_a_log_b), two(ssm_d_f, ssm_d_b)

    def dw3_of(j):
        w = full_small["conf_dw_w"][j]
        return w.reshape(w.shape[0], D // 128, 128).swapaxes(0, 1)

    h = jnp.concatenate([ctx[0], x[0]], axis=0)
    saved = []
    for i in range(depth):
        kind, j = i % 2, i // 2
        col_major = (j % 2) == 1
        sh1, sc1, g1, sh2, sc2, g2 = six(i)
        s = {"h": h}
        u = prenorm_fwd(h, pre_mix_g[i][None], sh1, sc1, tr, f"prenorm_mix{i}")
        if col_major:
            u = to_scan(u)
        s["u"] = u
        if kind == 0:
            proj = mm(u, Wb["ssm_in_w"][j], "nn", F32, name=f"ssm_in{i}")
            pre, act = conv5_fwd(proj, full_small["ssm_conv_w"][j], ssm_conv_b[j][None], d_inner // tc, tr, tc, f"ssm_conv{i}")
            dtr = proj[:, d_inner + xbc:].reshape(T, 2, G, 8).transpose(1, 2, 0, 3)
            bias, alog, dsk = ssm_params(j)
            y2, hsave = ssd_fwd(act, dtr, bias, alog, dsk, d_inner, ncc, f"ssd_fwd{i}")
            yn = gnorm_fwd(y2, proj, ssm_norm_g[j][None], CHUNK, f"ssm_gnorm{i}")
            out = mm(yn, Wb["ssm_out_w"][j], "nn", F32, name=f"ssm_out{i}")
            s.update(proj=proj, pre=pre, act=act, dtr=dtr, y2=y2, hsave=hsave, yn=yn)
        else:
            seg = rows_grid if col_major else GRID_W
            a = mm(u, Wb["conf_pw1_w"][j], "nn", F32, bias=full_small["conf_pw1_b"][j][None], name=f"conf_pw1_{i}")
            v = confmid_fwd(a, dw3_of(j), full_small["conf_dw_b"][j][None], full_small["conf_ln_g"][j][None],
                            full_small["conf_ln_b"][j][None], seg, tr, f"conf_mid{i}")
            out = mm(v, Wb["conf_pw2_w"][j], "nn", F32, bias=full_small["conf_pw2_b"][j][None], name=f"conf_pw2_{i}")
            s.update(a=a, v=v, seg=seg)
        if col_major:
            out = from_scan(out)
        h1 = post_fwd(h, out, post_mix_g[i][None], g1, tr, f"post_mix{i}")
        u2 = prenorm_fwd(h1, pre_mlp_g[i][None], sh2, sc2, tr, f"prenorm_mlp{i}")
        hid, actm = mm(u2, Wb["mlp_w1"][i], "nn", F32, relu2=True, name=f"mlp_up{i}")
        f = mm(actm, Wb["mlp_w2"][i], "nn", F32, name=f"mlp_down{i}")
        h = post_fwd(h1, f, post_mlp_g[i][None], g2, tr, f"post_mlp{i}")
        s.update(out=out, h1=h1, u2=u2, hid=hid, actm=actm, f=f)
        saved.append(s)

    loss_blk, Gr = loss_head(h, loss_target[0], tr, "loss_head")
    loss = lax.psum(loss_blk[0, 0], ("x", "y", "c"))

    gb = {n: [None] * W[n].shape[0] for n in BIG}
    gs = {n: [None] * W[n].shape[0] for n in W_NAMES if n not in BIG and n not in ("c_ctx", "mod_w", "mod_b")}
    dmod = [None] * depth
    for i in reversed(range(depth)):
        kind, j = i % 2, i // 2
        col_major = (j % 2) == 1
        sh1, sc1, g1, sh2, sc2, g2 = six(i)
        s = saved[i]
        df, gs["post_mlp_g"][i], dg2, _ = post_bwd(s["f"], post_mlp_g[i][None], g2, Gr, tr, f"post_mlp_bwd{i}")
        gb["mlp_w2"][i] = mm(s["actm"], df, "tn", BF16, name=f"mlp_down_wg{i}")
        dhid = mm(df, Wb["mlp_w2"][i], "nt", BF16, mul_relu=s["hid"], name=f"mlp_down_dg{i}")
        gb["mlp_w1"][i] = mm(s["u2"], dhid, "tn", BF16, name=f"mlp_up_wg{i}")
        du2 = mm(dhid, Wb["mlp_w1"][i], "nt", F32, name=f"mlp_up_dg{i}")
        Gr, gs["pre_mlp_g"][i], dsh2, dsc2 = prenorm_bwd(s["h1"], pre_mlp_g[i][None], sh2, sc2, du2, Gr, tr, f"prenorm_mlp_bwd{i}")
        dout, gs["post_mix_g"][i], dg1, dout_sum = post_bwd(s["out"], post_mix_g[i][None], g1, Gr, tr, f"post_mix_bwd{i}")
        if col_major:
            dout = to_scan(dout)
        if kind == 0:
            gb["ssm_out_w"][j] = mm(s["yn"], dout, "tn", BF16, name=f"ssm_out_wg{i}")
            dyn = mm(dout, Wb["ssm_out_w"][j], "nt", F32, name=f"ssm_out_dg{i}")
            dys, dz, gs["ssm_norm_g"][j] = gnorm_bwd(s["y2"], s["proj"], ssm_norm_g[j][None], dyn, CHUNK, f"ssm_gnorm_bwd{i}")
            bias, alog, dsk = ssm_params(j)
            dx2, db2, dc2, ddtr, dbias, dalog, ddsk = ssd_bwd(s["act"], s["dtr"], bias, alog, dsk, s["hsave"], dys, d_inner, ncc, f"ssd_bwd{i}")
            cw = full_small["ssm_conv_w"][j]
            nx, nb_ = d_inner // tc, GN // tc
            dxx, dwx, dbx = conv5_bwd(dx2, s["pre"], s["proj"], cw, 0, nx, tr, tc, f"ssm_conv_bwd_x{i}")
            dxb, dwb, dbb = conv5_bwd(db2, s["pre"], s["proj"], cw, nx, 2 * nx, tr, tc, f"ssm_conv_bwd_b{i}")
            dxc, dwc, dbc = conv5_bwd(dc2, s["pre"], s["proj"], cw, nx + nb_, 2 * nx + nb_, tr, tc, f"ssm_conv_bwd_c{i}")
            gs["ssm_conv_w"][j] = jnp.concatenate([dwx, dwb, dwc], axis=1)
            gs["ssm_conv_b"][j] = jnp.concatenate([dbx, dbb, dbc], axis=1)[0]
            ddt = ddtr.transpose(2, 0, 1, 3).reshape(T, 2 * H).astype(BF16)
            dproj = jnp.concatenate([dz, dxx, dxb, dxc, ddt], axis=1)
            gb["ssm_in_w"][j] = mm(s["u"], dproj, "tn", BF16, name=f"ssm_in_wg{i}")
            du = mm(dproj, Wb["ssm_in_w"][j], "nt", F32, name=f"ssm_in_dg{i}")
            for nm, val in (("ssm_dt_bias", dbias), ("ssm_a_log", dalog), ("ssm_d", ddsk)):
                gs[nm + "_f"][j] = val[0].reshape(H)
                gs[nm + "_b"][j] = val[1].reshape(H)
        else:
            gb["conf_pw2_w"][j] = mm(s["v"], dout, "tn", BF16, name=f"conf_pw2_wg{i}")
            gs["conf_pw2_b"][j] = dout_sum[0]
            dv = mm(dout, Wb["conf_pw2_w"][j], "nt", F32, name=f"conf_pw2_dg{i}")
            da, da_sum, dw3, ddb, dlg, dlb = confmid_bwd(s["a"], dw3_of(j), full_small["conf_dw_b"][j][None], full_small["conf_ln_g"][j][None],
                                                          full_small["conf_ln_b"][j][None], dv, s["seg"], tr, f"conf_mid_bwd{i}")
            gs["conf_pw1_b"][j] = da_sum[0]
            gs["conf_dw_w"][j] = dw3.swapaxes(0, 1).reshape(dw3.shape[1], D)
            gs["conf_dw_b"][j], gs["conf_ln_g"][j], gs["conf_ln_b"][j] = ddb[0], dlg[0], dlb[0]
            gb["conf_pw1_w"][j] = mm(s["u"], da, "tn", BF16, name=f"conf_pw1_wg{i}")
            du = mm(da, Wb["conf_pw1_w"][j], "nt", F32, name=f"conf_pw1_dg{i}")
        if col_major:
            du = from_scan(du)
        Gr, gs["pre_mix_g"][i], dsh1, dsc1 = prenorm_bwd(s["h"], pre_mix_g[i][None], sh1, sc1, du, Gr, tr, f"prenorm_mix_bwd{i}")
        dmod[i] = jnp.concatenate([t.reshape(2, D) for t in (dsh1, dsc1, dg1, dsh2, dsc2, dg2)], axis=1)
    grad_x = Gr[Lc:][None]

    small_names = list(gs)
    small_local = [jnp.stack([t.reshape(W[n].shape[1:] if n not in SMALL_SHARDED else t.shape) for t in gs[n]]) for n in small_names]
    small_shapes = [t.shape for t in small_local] + [(depth, 2, 6 * D)]
    got = allgather8(_pack(small_local + [jnp.stack(dmod)]), "gather_small_grads")
    summed = _unpack(sum_slabs(got, "sum_small_grads"), small_shapes)
    grads = {}
    for n, t in zip(small_names, summed[:-1]):
        if n in SMALL_SHARDED:
            w = W[n].shape[-1]
            t = lax.dynamic_slice_in_dim(t, chip * w, w, axis=t.ndim - 1)
        grads[n] = t
    grads["mod_b"] = summed[-1][:, 0] + summed[-1][:, 1]
    dm_all = _unpack(got, small_shapes)[-1]
    dm_ctx = sum_slabs(dm_all[:, :, 0], "sum_dmod_ctx")
    dm_rows = jnp.concatenate([dm_all[:, :, 1].swapaxes(0, 1), dm_ctx[:, None], jnp.zeros((depth, 7, 6 * D), F32)], axis=1)
    dm_mine = lax.dynamic_slice_in_dim(dm_rows, chip * ncol, ncol, axis=2)
    grads["mod_w"], dcond = mod_bwd(cond, mod_w, dm_mine, "mod_bwd")
    dcc = sum_slabs(dcond[:, 8:9], "sum_dcond_layers")
    dcc_all = allgather8(jnp.pad(dcc, ((0, 7), (0, 0))), "gather_dcond")
    dcc_sum = sum_slabs(dcc_all[0::2, 0:1], "sum_dcond_chips")
    grads["c_ctx"] = silu_grad(dcc_sum, c_ctx.reshape(1, D), "c_ctx_grad").reshape(D)

    slabs = [_chip_slabs(jnp.stack(gb[n]), BIG[n]) for n in big_names]
    recv = chip_exchange(slabs, "scatter_grads")
    part = [sum_slabs(r.reshape((4, -1, r.shape[-1])), f"sum_grads_{n}") for n, r in zip(big_names, recv)]
    other = sibling_exchange(part, "sibling_grads")

    res = {}
    for n in W_NAMES:
        w2 = _view2d(W[n])
        if n in BIG:
            k = big_names.index(n)
            g1_, g2_ = part[k], other[k]
        else:
            g1_, g2_ = _view2d(grads[n]), None
        cols = w2.shape[1]
        outs = adamw(w2, g1_, g2_, _view2d(given["m_" + n]), _view2d(given["v_" + n]), f"adamw_{n}", tr=max(8, (262144 // cols) // 8 * 8))
        res[n] = [o.reshape(W[n].shape) for o in outs]
    return (loss, grad_x, *[res[n][0] for n in W_NAMES], *[res[n][1] for n in W_NAMES], *[res[n][2] for n in W_NAMES],
            *[res[n][3] for n in W_NAMES])
```

```python
import functools

import jax
import jax.numpy as jnp
from jax import lax
from jax.experimental import pallas as pl
from jax.experimental.pallas import tpu as pltpu

GRID_W = 64
CHUNK = 128
N_STATE = 128
EPS = 1e-6
ADAM_LR, ADAM_B1, ADAM_B2, ADAM_EPS, ADAM_WD, ADAM_STEP = 0.001, 0.9, 0.999, 1e-08, 0.01, 10
VMEM_LIMIT = 56 * 1024 * 1024
F32, BF16 = jnp.float32, jnp.bfloat16
HI = lax.Precision.HIGHEST
MESH = pl.DeviceIdType.MESH


def _cp(n_grid):
    return pltpu.CompilerParams(dimension_semantics=("arbitrary",) * n_grid, vmem_limit_bytes=VMEM_LIMIT)


def _sds(shape, dtype):
    return jax.ShapeDtypeStruct(tuple(shape), dtype)


def _div_tile(n, target, unit=128):
    best = None
    t = unit
    while t <= min(n, target):
        if n % t == 0:
            best = t
        t += unit
    return best if best is not None else n


def _rms(x, g):
    return x * lax.rsqrt(jnp.mean(x * x, axis=-1, keepdims=True) + EPS) * g


def _silu(x):
    return x * jax.nn.sigmoid(x)


def mm(a, b, mode, out_dtype, *, bias=None, relu2=False, mul_relu=None, name, tm=768, tn=1152, tk=2048):
    if mode == "nn":
        (M, C), (_, N) = a.shape, b.shape
    elif mode == "nt":
        (M, C), (N, _) = a.shape, b.shape
    else:
        (C, M), (_, N) = a.shape, b.shape
    if mode == "tn":
        tm, tn, tk = _div_tile(M, 1024), _div_tile(N, tn), _div_tile(C, tk, 8)
    else:
        tm, tn, tk = _div_tile(M, tm, 8), _div_tile(N, tn), _div_tile(C, tk)
    nk = C // tk
    a_spec = {"nn": pl.BlockSpec((tm, tk), lambda i, j, k: (i, k)), "nt": pl.BlockSpec((tm, tk), lambda i, j, k: (i, k)),
              "tn": pl.BlockSpec((tk, tm), lambda i, j, k: (k, i))}[mode]
    b_spec = {"nn": pl.BlockSpec((tk, tn), lambda i, j, k: (k, j)), "nt": pl.BlockSpec((tn, tk), lambda i, j, k: (j, k)),
              "tn": pl.BlockSpec((tk, tn), lambda i, j, k: (k, j))}[mode]
    dims = {"nn": (((1,), (0,)), ((), ())), "nt": (((1,), (1,)), ((), ())), "tn": (((0,), (0,)), ((), ()))}[mode]
    ins, specs = [a, b], [a_spec, b_spec]
    if bias is not None:
        ins.append(bias)
        specs.append(pl.BlockSpec((1, tn), lambda i, j, k: (0, j)))
    if mul_relu is not None:
        ins.append(mul_relu)
        specs.append(pl.BlockSpec((tm, tn), lambda i, j, k: (i, j)))
    o_spec = pl.BlockSpec((tm, tn), lambda i, j, k: (i, j))
    outs, out_specs = [_sds((M, N), out_dtype)], [o_spec]
    if relu2:
        outs.append(_sds((M, N), BF16))
        out_specs.append(o_spec)

    def body(*refs):
        a_ref, b_ref = refs[0], refs[1]
        pos = 2
        bias_ref = mr_ref = None
        if bias is not None:
            bias_ref = refs[pos]
            pos += 1
        if mul_relu is not None:
            mr_ref = refs[pos]
            pos += 1
        o_ref = refs[pos]
        o2_ref = refs[pos + 1] if relu2 else None
        part = lax.dot_general(a_ref[...].astype(BF16), b_ref[...].astype(BF16), dims, preferred_element_type=F32)

        def finish(r):
            if bias_ref is not None:
                r = r + bias_ref[...]
            if mr_ref is not None:
                r = r * (2.0 * jnp.maximum(mr_ref[...], 0.0))
            o_ref[...] = r.astype(o_ref.dtype)
            if o2_ref is not None:
                q = jnp.maximum(r, 0.0)
                o2_ref[...] = (q * q).astype(BF16)

        if nk == 1:
            finish(part)
        else:
            acc_ref = refs[-1]
            k = pl.program_id(2)

            @pl.when(k == 0)
            def _():
                acc_ref[...] = part

            @pl.when(jnp.logical_and(k > 0, k < nk - 1))
            def _():
                acc_ref[...] += part

            @pl.when(k == nk - 1)
            def _():
                finish(acc_ref[...] + part)

    res = pl.pallas_call(body, out_shape=outs, grid=(M // tm, N // tn, nk), in_specs=specs, out_specs=out_specs,
                         scratch_shapes=[pltpu.VMEM((tm, tn), F32)] if nk > 1 else [], compiler_params=_cp(3), name=name)(*ins)
    return res if relu2 else res[0]


def _seg_spec(D):
    return pl.BlockSpec((None, 1, D), lambda i: (jnp.minimum(i, 1), 0, 0))


def _prenorm_fn(h, g, sh, sc):
    return _rms(h, g) * (1.0 + sc) + sh


def prenorm_fwd(h, g, sh, sc, tr, name):
    T, D = h.shape
    row = pl.BlockSpec((tr, D), lambda i: (i, 0))
    vec = pl.BlockSpec((1, D), lambda i: (0, 0))

    def body(h_ref, g_ref, sh_ref, sc_ref, u_ref):
        u_ref[...] = _prenorm_fn(h_ref[...], g_ref[...], sh_ref[...], sc_ref[...]).astype(BF16)

    return pl.pallas_call(body, out_shape=_sds((T, D), BF16), grid=(T // tr,), in_specs=[row, vec, _seg_spec(D), _seg_spec(D)],
                          out_specs=row, compiler_params=_cp(1), name=name)(h, g, sh, sc)


def _acc(ref, val, first):
    @pl.when(first)
    def _():
        ref[...] = val

    @pl.when(jnp.logical_not(first))
    def _():
        ref[...] += val


def prenorm_bwd(h, g, sh, sc, du, G, tr, name):
    T, D = h.shape
    row = pl.BlockSpec((tr, D), lambda i: (i, 0))
    vec = pl.BlockSpec((1, D), lambda i: (0, 0))

    def body(h_ref, g_ref, sh_ref, sc_ref, du_ref, G_ref, Go_ref, dg_ref, dsh_ref, dsc_ref):
        i = pl.program_id(0)
        _, vjp = jax.vjp(_prenorm_fn, h_ref[...], g_ref[...], sh_ref[...], sc_ref[...])
        dh, dg, dsh, dsc = vjp(du_ref[...].astype(F32))
        Go_ref[...] = G_ref[...] + dh
        _acc(dg_ref, dg, i == 0)
        _acc(dsh_ref, dsh, i <= 1)
        _acc(dsc_ref, dsc, i <= 1)

    return pl.pallas_call(
        body, out_shape=[_sds((T, D), F32), _sds((1, D), F32), _sds((2, 1, D), F32), _sds((2, 1, D), F32)], grid=(T // tr,),
        in_specs=[row, vec, _seg_spec(D), _seg_spec(D), row, row], out_specs=[row, vec, _seg_spec(D), _seg_spec(D)],
        compiler_params=_cp(1), name=name)(h, g, sh, sc, du, G)


def _post_fn(y, gp, gate):
    return gate * _rms(y, gp)


def post_fwd(h, y, gp, gate, tr, name):
    T, D = h.shape
    row = pl.BlockSpec((tr, D), lambda i: (i, 0))
    vec = pl.BlockSpec((1, D), lambda i: (0, 0))

    def body(h_ref, y_ref, gp_ref, gate_ref, o_ref):
        o_ref[...] = h_ref[...] + _post_fn(y_ref[...], gp_ref[...], gate_ref[...])

    return pl.pallas_call(body, out_shape=_sds((T, D), F32), grid=(T // tr,), in_specs=[row, row, vec, _seg_spec(D)],
                          out_specs=row, compiler_params=_cp(1), name=name)(h, y, gp, gate)


def post_bwd(y, gp, gate, G, tr, name):
    T, D = y.shape
    row = pl.BlockSpec((tr, D), lambda i: (i, 0))
    vec = pl.BlockSpec((1, D), lambda i: (0, 0))

    def body(y_ref, gp_ref, gate_ref, G_ref, dy_ref, dgp_ref, dgate_ref, dsum_ref):
        i = pl.program_id(0)
        _, vjp = jax.vjp(_post_fn, y_ref[...], gp_ref[...], gate_ref[...])
        dy, dgp, dgate = vjp(G_ref[...])
        dy_ref[...] = dy.astype(BF16)
        _acc(dgp_ref, dgp, i == 0)
        _acc(dgate_ref, dgate, i <= 1)
        _acc(dsum_ref, jnp.sum(dy, axis=0, keepdims=True), i == 0)

    return pl.pallas_call(
        body, out_shape=[_sds((T, D), BF16), _sds((1, D), F32), _sds((2, 1, D), F32), _sds((1, D), F32)], grid=(T // tr,),
        in_specs=[row, vec, _seg_spec(D), row], out_specs=[row, vec, _seg_spec(D), vec], compiler_params=_cp(1), name=name)(y, gp, gate, G)


def loss_head(h, target, tr, name):
    T, D = h.shape
    row = pl.BlockSpec((tr, D), lambda i: (i, 0))
    trow = pl.BlockSpec((tr, D), lambda i: (jnp.maximum(i - 1, 0), 0))

    def body(h_ref, t_ref, loss_ref, G_ref):
        i = pl.program_id(0)

        @pl.when(i == 0)
        def _():
            loss_ref[...] = jnp.zeros_like(loss_ref)
            G_ref[...] = jnp.zeros_like(G_ref)

        @pl.when(i > 0)
        def _():
            e = h_ref[...] - t_ref[...]
            G_ref[...] = e * (1.0 / D)
            loss_ref[...] += jnp.sum(e * e) * (0.5 / D)

    return pl.pallas_call(body, out_shape=[_sds((8, 128), F32), _sds((T, D), F32)], grid=(T // tr,), in_specs=[row, trow],
                          out_specs=[pl.BlockSpec((8, 128), lambda i: (0, 0)), row], compiler_params=_cp(1), name=name)(h, target)


def _halo_specs(tr, tc, T, col0, lead=()):
    n8 = tr // 8
    nl = len(lead)
    cur = pl.BlockSpec(lead + (tr, tc), lambda j, i: (0,) * nl + (i, col0 + j))
    prev = pl.BlockSpec(lead + (8, tc), lambda j, i: (0,) * nl + (jnp.maximum(i * n8 - 1, 0), col0 + j))
    nxt = pl.BlockSpec(lead + (8, tc), lambda j, i: (0,) * nl + (jnp.minimum((i + 1) * n8, T // 8 - 1), col0 + j))
    return [cur, prev, nxt]


def _with_halo(cur, prev, nxt, i, nt):
    keep_prev = (i >= 2).astype(cur.dtype)
    keep_next = jnp.logical_and(i >= 1, i < nt - 1).astype(cur.dtype)
    return jnp.concatenate([prev * keep_prev, cur, nxt * keep_next], axis=0)


def _shift_rows(ext, o, tr):
    n = ext.shape[0]
    return pltpu.roll(ext, (-o) % n, 0)[8:8 + tr]


def conv5_fwd(proj, w, b, col0, tr, tc, name):
    T = proj.shape[0]
    K, C = w.shape
    nt = T // tr

    def body(x_ref, xp_ref, xn_ref, w_ref, b_ref, pre_ref, act_ref):
        i = pl.program_id(1)
        ext = _with_halo(x_ref[...], xp_ref[...], xn_ref[...], i, nt)
        wv = w_ref[...]
        acc = jnp.zeros((tr, tc), F32) + b_ref[...]
        for k in range(K):
            acc = acc + wv[k:k + 1, :] * _shift_rows(ext, k - K // 2, tr)
        pre_ref[...] = acc
        act_ref[...] = _silu(acc)

    out = pl.BlockSpec((tr, tc), lambda j, i: (i, j))
    return pl.pallas_call(
        body, out_shape=[_sds((T, C), F32), _sds((T, C), F32)], grid=(C // tc, nt),
        in_specs=_halo_specs(tr, tc, T, col0) + [pl.BlockSpec((K, tc), lambda j, i: (0, j)), pl.BlockSpec((1, tc), lambda j, i: (0, j))],
        out_specs=[out, out], compiler_params=_cp(2), name=name)(proj, proj, proj, w, b)


def conv5_bwd(dact, pre, proj, w, colp, colx, tr, tc, name):
    _, T, Cp = dact.shape
    K = w.shape[0]
    nt = T // tr

    def body(d_ref, dp_ref, dn_ref, p_ref, pp_ref, pn_ref, x_ref, xp_ref, xn_ref, w_ref, dx_ref, dw_ref, db_ref):
        i = pl.program_id(1)

        def dpre_of(d, p):
            s = jax.nn.sigmoid(p)
            return (d[0] + d[1]) * (s * (1.0 + p * (1.0 - s)))

        dext = _with_halo(dpre_of(d_ref[...], p_ref[...]), dpre_of(dp_ref[...], pp_ref[...]), dpre_of(dn_ref[...], pn_ref[...]), i, nt)
        xext = _with_halo(x_ref[...], xp_ref[...], xn_ref[...], i, nt)
        dcur = dext[8:8 + tr]
        wv = w_ref[...]
        dx = jnp.zeros((tr, tc), F32)
        for k in range(K):
            o = k - K // 2
            dx = dx + wv[k:k + 1, :] * _shift_rows(dext, -o, tr)
            _acc(dw_ref.at[k:k + 1, :], jnp.sum(dcur * _shift_rows(xext, o, tr), axis=0, keepdims=True), i == 0)
        dx_ref[...] = dx.astype(BF16)
        _acc(db_ref, jnp.sum(dcur, axis=0, keepdims=True), i == 0)

    out = pl.BlockSpec((tr, tc), lambda j, i: (i, j))
    return pl.pallas_call(
        body, out_shape=[_sds((T, Cp), BF16), _sds((K, Cp), F32), _sds((1, Cp), F32)], grid=(Cp // tc, nt),
        in_specs=_halo_specs(tr, tc, T, 0, lead=(2,)) + _halo_specs(tr, tc, T, colp) + _halo_specs(tr, tc, T, colx)
        + [pl.BlockSpec((K, tc), lambda j, i: (0, colp + j))],
        out_specs=[out, pl.BlockSpec((K, tc), lambda j, i: (0, j)), pl.BlockSpec((1, tc), lambda j, i: (0, j))],
        compiler_params=_cp(2), name=name)(dact, dact, dact, pre, pre, pre, proj, proj, proj, w)


def _ssd_chunk(xg, bg, cg, dtr, hin, bias, alog, dsk, rev):
    Q, P8 = xg.shape
    nh = dtr.shape[1]
    P = P8 // nh
    N = bg.shape[1]
    dt = jax.nn.softplus(dtr + bias)
    da = dt * (-jnp.exp(alog))
    r_i = lax.broadcasted_iota(jnp.int32, (Q, Q), 0)
    c_i = lax.broadcasted_iota(jnp.int32, (Q, Q), 1)
    mask = jnp.where(rev, c_i - r_i, r_i - c_i) >= 0
    cs = jnp.dot(mask.astype(F32), da, precision=HI, preferred_element_type=F32)
    cs_t = cs.T
    expand = (lax.broadcasted_iota(jnp.int32, (nh, P8), 0) == lax.broadcasted_iota(jnp.int32, (nh, P8), 1) // P).astype(F32)
    blk = (lax.broadcasted_iota(jnp.int32, (nh, nh * Q), 0) == lax.broadcasted_iota(jnp.int32, (nh, nh * Q), 1) // Q).astype(F32)

    def over_lanes(v):
        return jnp.dot(v, expand, precision=HI, preferred_element_type=F32)

    tot = jnp.where(rev, cs[0:1, :], cs[Q - 1:Q, :])
    dt_x, cs_x, tot_x, dsk_x = over_lanes(dt), over_lanes(cs), over_lanes(tot), over_lanes(dsk)
    lhs = jnp.concatenate([cs, jnp.ones((Q, nh), F32)], axis=1)
    rhs = jnp.concatenate([blk, -blk * jnp.concatenate([cs_t] * nh, axis=1)], axis=0)
    seg = jnp.dot(lhs, rhs, precision=HI, preferred_element_type=F32)
    decay = jnp.exp(jnp.where(jnp.concatenate([mask] * nh, axis=1), seg, -jnp.inf))
    scores = lax.dot_general(cg.astype(BF16), bg.astype(BF16), (((1,), (1,)), ((), ())), preferred_element_type=F32)
    m_all = (jnp.concatenate([scores] * nh, axis=1) * decay).astype(BF16)
    xdt = xg * dt_x
    xdt_b = xdt.astype(BF16)
    xde = (xdt * jnp.exp(tot_x - cs_x)).astype(BF16)
    e_in = jnp.exp(cs_x)
    low = lax.broadcasted_iota(jnp.int32, (1, 2 * P), 1) < P
    cb, bb = cg.astype(BF16), bg.astype(BF16)
    zero = jnp.zeros((), BF16)
    ys, sts = [], []
    for p in range(nh // 2):
        sl = slice(2 * p * P, 2 * (p + 1) * P)
        xp = xdt_b[:, sl]
        y = jnp.dot(m_all[:, 2 * p * Q:(2 * p + 1) * Q], jnp.where(low, xp, zero), preferred_element_type=F32)
        y = y + jnp.dot(m_all[:, (2 * p + 1) * Q:(2 * p + 2) * Q], jnp.where(low, zero, xp), preferred_element_type=F32)
        y = y + lax.dot_general(cb, hin[sl, :].astype(BF16), (((1,), (1,)), ((), ())), preferred_element_type=F32) * e_in[:, sl]
        ys.append(y)
        sts.append(lax.dot_general(xde[:, sl], bb, (((0,), (0,)), ((), ())), preferred_element_type=F32))
    y = jnp.concatenate(ys, axis=1) + dsk_x * xg
    tot_c = jnp.where(rev, cs_t[:, 0:1], cs_t[:, Q - 1:Q])
    etot = lax.dot_general(expand, jnp.broadcast_to(jnp.exp(tot_c), (nh, N)), (((0,), (0,)), ((), ())), precision=HI,
                           preferred_element_type=F32)
    return y, etot * hin + jnp.concatenate(sts, axis=0)


def _ssd_specs(Q, P8, N, ncc, NC, xcol_b, xcol_c, back):
    def chunk(d, s):
        s = (NC - 1 - s) if back else s
        return jnp.where(d == 0, s, jnp.where(s < ncc, ncc - 1 - s, ncc + NC - 1 - s))

    def step(s):
        return (NC - 1 - s) if back else s

    x = pl.BlockSpec((Q, P8), lambda d, g, s: (chunk(d, s), g))
    bsp = pl.BlockSpec((Q, N), lambda d, g, s: (chunk(d, s), xcol_b + g))
    csp = pl.BlockSpec((Q, N), lambda d, g, s: (chunk(d, s), xcol_c + g))
    dt = pl.BlockSpec((None, None, Q, 8), lambda d, g, s: (d, g, chunk(d, s), 0))
    par = pl.BlockSpec((None, None, 1, 8), lambda d, g, s: (d, g, 0, 0))
    hst = pl.BlockSpec((None, None, None, P8, N), lambda d, g, s: (d, g, step(s), 0, 0))
    yd = pl.BlockSpec((None, Q, P8), lambda d, g, s: (d, chunk(d, s), g))
    bd = pl.BlockSpec((None, Q, N), lambda d, g, s: (d, chunk(d, s), g))
    return x, bsp, csp, dt, par, hst, yd, bd


def ssd_fwd(act, dtr, bias, alog, dsk, d_inner, ncc, name):
    T = act.shape[0]
    G = dtr.shape[1]
    Q, N = CHUNK, N_STATE
    NC = T // Q
    P8 = d_inner // G
    x, bsp, csp, dt, par, hst, yd, _ = _ssd_specs(Q, P8, N, ncc, NC, d_inner // N, d_inner // N + G, False)

    def body(x_ref, b_ref, c_ref, dt_ref, bias_ref, alog_ref, dsk_ref, y_ref, h_ref, st_ref):
        d, s = pl.program_id(0), pl.program_id(2)

        @pl.when(s == 0)
        def _():
            st_ref[...] = jnp.zeros_like(st_ref)

        hin = st_ref[...]
        h_ref[...] = hin
        y, ho = _ssd_chunk(x_ref[...], b_ref[...], c_ref[...], dt_ref[...], hin, bias_ref[...], alog_ref[...], dsk_ref[...], d == 1)
        y_ref[...] = y
        st_ref[...] = ho

    return pl.pallas_call(
        body, out_shape=[_sds((2, T, d_inner), F32), _sds((2, G, NC, P8, N), F32)], grid=(2, G, NC),
        in_specs=[x, bsp, csp, dt, par, par, par], out_specs=[yd, hst], scratch_shapes=[pltpu.VMEM((P8, N), F32)],
        compiler_params=_cp(3), name=name)(act, act, act, dtr, bias, alog, dsk)


def ssd_bwd(act, dtr, bias, alog, dsk, hsave, dy, d_inner, ncc, name):
    T = act.shape[0]
    G = dtr.shape[1]
    Q, N = CHUNK, N_STATE
    NC = T // Q
    P8 = d_inner // G
    x, bsp, csp, dt, par, hst, yd, bd = _ssd_specs(Q, P8, N, ncc, NC, d_inner // N, d_inner // N + G, True)
    dysp = pl.BlockSpec((Q, P8), x.index_map)

    def body(x_ref, b_ref, c_ref, dt_ref, bias_ref, alog_ref, dsk_ref, h_ref, dy_ref,
             dx_ref, db_ref, dc_ref, ddt_ref, dbias_ref, dalog_ref, ddsk_ref, dh_ref):
        d, s = pl.program_id(0), pl.program_id(2)

        @pl.when(s == 0)
        def _():
            dh_ref[...] = jnp.zeros_like(dh_ref)

        args = (x_ref[...], b_ref[...], c_ref[...], dt_ref[...], h_ref[...], bias_ref[...], alog_ref[...], dsk_ref[...])

        _, vjp = jax.vjp(functools.partial(_ssd_chunk, rev=d == 1), *args)
        dx, db, dc, ddt, dhin, dbias, dalog, ddsk = vjp((dy_ref[...], dh_ref[...]))
        dx_ref[...] = dx
        db_ref[...] = db
        dc_ref[...] = dc
        ddt_ref[...] = ddt
        dh_ref[...] = dhin
        _acc(dbias_ref, dbias, s == 0)
        _acc(dalog_ref, dalog, s == 0)
        _acc(ddsk_ref, ddsk, s == 0)

    GN = G * N
    return pl.pallas_call(
        body,
        out_shape=[_sds((2, T, d_inner), F32), _sds((2, T, GN), F32), _sds((2, T, GN), F32), _sds(dtr.shape, F32),
                   _sds(bias.shape, F32), _sds(bias.shape, F32), _sds(bias.shape, F32)],
        grid=(2, G, NC), in_specs=[x, bsp, csp, dt, par, par, par, hst, dysp], out_specs=[yd, bd, bd, dt, par, par, par],
        scratch_shapes=[pltpu.VMEM((P8, N), F32)], compiler_params=_cp(3), name=name)(act, act, act, dtr, bias, alog, dsk, hsave, dy)


def _gnorm_fn(yf, yb, z, g):
    return _rms((yf + yb) * _silu(z), g)


def gnorm_fwd(y2, proj, g, tr, name):
    _, T, C = y2.shape
    yf = pl.BlockSpec((None, tr, C), lambda i: (0, i, 0))
    yb = pl.BlockSpec((None, tr, C), lambda i: (1, i, 0))
    row = pl.BlockSpec((tr, C), lambda i: (i, 0))
    vec = pl.BlockSpec((1, C), lambda i: (0, 0))

    def body(yf_ref, yb_ref, z_ref, g_ref, o_ref):
        o_ref[...] = _gnorm_fn(yf_ref[...], yb_ref[...], z_ref[...], g_ref[...]).astype(BF16)

    return pl.pallas_call(body, out_shape=_sds((T, C), BF16), grid=(T // tr,), in_specs=[yf, yb, row, vec], out_specs=row,
                          compiler_params=_cp(1), name=name)(y2, y2, proj, g)


def gnorm_bwd(y2, proj, g, dyn, tr, name):
    _, T, C = y2.shape
    yf = pl.BlockSpec((None, tr, C), lambda i: (0, i, 0))
    yb = pl.BlockSpec((None, tr, C), lambda i: (1, i, 0))
    row = pl.BlockSpec((tr, C), lambda i: (i, 0))
    vec = pl.BlockSpec((1, C), lambda i: (0, 0))

    def body(yf_ref, yb_ref, z_ref, g_ref, d_ref, dy_ref, dz_ref, dg_ref):
        i = pl.program_id(0)
        _, vjp = jax.vjp(_gnorm_fn, yf_ref[...], yb_ref[...], z_ref[...], g_ref[...])
        dyf, _, dz, dg = vjp(d_ref[...].astype(F32))
        dy_ref[...] = dyf
        dz_ref[...] = dz.astype(BF16)
        _acc(dg_ref, dg, i == 0)

    return pl.pallas_call(body, out_shape=[_sds((T, C), F32), _sds((T, C), BF16), _sds((1, C), F32)], grid=(T // tr,),
                          in_specs=[yf, yb, row, vec, row], out_specs=[row, row, vec], compiler_params=_cp(1), name=name)(y2, y2, proj, g, dyn)


def _glu_fn(a):
    D = a.shape[1] // 2
    return a[:, :D] * jax.nn.sigmoid(a[:, D:])


def _ln_swish_fn(v, g, b):
    mu = jnp.mean(v, axis=-1, keepdims=True)
    xc = v - mu
    var = jnp.mean(xc * xc, axis=-1, keepdims=True)
    y = xc * lax.rsqrt(var + EPS) * g + b
    return y * jax.nn.sigmoid(y)


def _seg_pos(tr, seg, i):
    p = lax.broadcasted_iota(jnp.int32, (tr, 1), 0)
    s = jnp.where(i == 0, tr, seg)
    return p & (s - 1), s


def _dw_taps(v, w, pos, s, sign):
    tr = v.shape[0]
    K = w.shape[0]
    acc = jnp.zeros_like(v)
    for k in range(K):
        o = sign * (k - K // 2)
        q = pos + o
        ok = jnp.logical_and(q >= 0, q < s).astype(v.dtype)
        acc = acc + w[k:k + 1, :] * (pltpu.roll(v, (-o) % tr, 0) * ok)
    return acc


def _lane_blocks(v, ref):
    for c in range(v.shape[1] // 128):
        ref[c] = v[:, c * 128:(c + 1) * 128]


def _from_lane_blocks(ref):
    return jnp.concatenate([ref[c] for c in range(ref.shape[0])], axis=1)


def confmid_fwd(a, w3, b, lg, lb, seg, tr, name):
    T, D2 = a.shape
    D = D2 // 2
    nb, K, _ = w3.shape
    vec = pl.BlockSpec((1, D), lambda i: (0, 0))

    def body(a_ref, w_ref, b_ref, lg_ref, lb_ref, o_ref, s0_ref, s1_ref):
        i = pl.program_id(0)
        pos, s = _seg_pos(tr, seg, i)
        _lane_blocks(_glu_fn(a_ref[...]), s0_ref)

        def blk(c, carry):
            s1_ref[c] = _dw_taps(s0_ref[c], w_ref[c], pos, s, 1)
            return carry

        lax.fori_loop(0, nb, blk, 0)
        o_ref[...] = _ln_swish_fn(_from_lane_blocks(s1_ref) + b_ref[...], lg_ref[...], lb_ref[...]).astype(BF16)

    return pl.pallas_call(
        body, out_shape=_sds((T, D), BF16), grid=(T // tr,),
        in_specs=[pl.BlockSpec((tr, D2), lambda i: (i, 0)), pl.BlockSpec((nb, K, 128), lambda i: (0, 0, 0)), vec, vec, vec],
        out_specs=pl.BlockSpec((tr, D), lambda i: (i, 0)), scratch_shapes=[pltpu.VMEM((nb, tr, 128), F32)] * 2,
        compiler_params=_cp(1), name=name)(a, w3, b, lg, lb)


def confmid_bwd(a, w3, b, lg, lb, dv, seg, tr, name):
    T, D2 = a.shape
    D = D2 // 2
    nb, K, _ = w3.shape
    vec = pl.BlockSpec((1, D), lambda i: (0, 0))
    vec2 = pl.BlockSpec((1, D2), lambda i: (0, 0))
    wsp = pl.BlockSpec((nb, K, 128), lambda i: (0, 0, 0))

    def body(a_ref, w_ref, b_ref, lg_ref, lb_ref, dv_ref, da_ref, dsum_ref, dw_ref, db_ref, dlg_ref, dlb_ref, s0_ref, s1_ref, s2_ref):
        i = pl.program_id(0)
        first = i == 0
        pos, s = _seg_pos(tr, seg, i)
        v0, glu_vjp = jax.vjp(_glu_fn, a_ref[...])
        _lane_blocks(v0, s0_ref)

        def conv(c, carry):
            s1_ref[c] = _dw_taps(s0_ref[c], w_ref[c], pos, s, 1)
            return carry

        lax.fori_loop(0, nb, conv, 0)
        _, ln_vjp = jax.vjp(_ln_swish_fn, _from_lane_blocks(s1_ref) + b_ref[...], lg_ref[...], lb_ref[...])
        dv1, dlg, dlb = ln_vjp(dv_ref[...].astype(F32))
        _acc(db_ref, jnp.sum(dv1, axis=0, keepdims=True), first)
        _acc(dlg_ref, dlg, first)
        _acc(dlb_ref, dlb, first)
        _lane_blocks(dv1, s2_ref)

        @pl.when(first)
        def _():
            dw_ref[...] = jnp.zeros_like(dw_ref)

        def conv_t(c, carry):
            d1, v0c = s2_ref[c], s0_ref[c]
            s1_ref[c] = _dw_taps(d1, w_ref[c], pos, s, -1)
            for k in range(K):
                o = k - K // 2
                q = pos + o
                ok = jnp.logical_and(q >= 0, q < s).astype(F32)
                dw_ref[c, k:k + 1, :] += jnp.sum(d1 * (pltpu.roll(v0c, (-o) % tr, 0) * ok), axis=0, keepdims=True)
            return carry

        lax.fori_loop(0, nb, conv_t, 0)
        (da,) = glu_vjp(_from_lane_blocks(s1_ref))
        da_ref[...] = da.astype(BF16)
        _acc(dsum_ref, jnp.sum(da, axis=0, keepdims=True), first)

    return pl.pallas_call(
        body, out_shape=[_sds((T, D2), BF16), _sds((1, D2), F32), _sds((nb, K, 128), F32), _sds((1, D), F32), _sds((1, D), F32), _sds((1, D), F32)],
        grid=(T // tr,), in_specs=[pl.BlockSpec((tr, D2), lambda i: (i, 0)), wsp, vec, vec, vec, pl.BlockSpec((tr, D), lambda i: (i, 0))],
        out_specs=[pl.BlockSpec((tr, D2), lambda i: (i, 0)), vec2, wsp, vec, vec, vec], scratch_shapes=[pltpu.VMEM((nb, tr, 128), F32)] * 3,
        compiler_params=_cp(1), name=name)(a, w3, b, lg, lb, dv)


def mod_fwd(rows, w, bsl, name):
    Ly, D, Nc = w.shape
    tn = _div_tile(Nc, 512)

    def body(r_ref, w_ref, b_ref, o_ref):
        s = _silu(r_ref[...]).astype(BF16)
        o_ref[...] = jnp.dot(s, w_ref[...].astype(BF16), preferred_element_type=F32) + b_ref[...]

    return pl.pallas_call(
        body, out_shape=_sds((Ly, 16, Nc), F32), grid=(Ly, Nc // tn),
        in_specs=[pl.BlockSpec((16, D), lambda l, j: (0, 0)), pl.BlockSpec((None, D, tn), lambda l, j: (l, 0, j)),
                  pl.BlockSpec((None, 1, tn), lambda l, j: (l, 0, j))],
        out_specs=pl.BlockSpec((None, 16, tn), lambda l, j: (l, 0, j)), compiler_params=_cp(2), name=name)(rows, w, bsl)


def mod_bwd(rows, w, dm, name):
    Ly, D, Nc = w.shape
    tn = _div_tile(Nc, 512)
    nj = Nc // tn

    def body(r_ref, w_ref, dm_ref, dw_ref, ds_ref):
        j = pl.program_id(1)
        s = _silu(r_ref[...]).astype(BF16)
        dmv = dm_ref[...].astype(BF16)
        dw_ref[...] = lax.dot_general(s, dmv, (((0,), (0,)), ((), ())), preferred_element_type=F32)
        _acc(ds_ref, lax.dot_general(dmv, w_ref[...].astype(BF16), (((1,), (1,)), ((), ())), preferred_element_type=F32), j == 0)

    return pl.pallas_call(
        body, out_shape=[_sds((Ly, D, Nc), F32), _sds((Ly, 16, D), F32)], grid=(Ly, nj),
        in_specs=[pl.BlockSpec((16, D), lambda l, j: (0, 0)), pl.BlockSpec((None, D, tn), lambda l, j: (l, 0, j)),
                  pl.BlockSpec((None, 16, tn), lambda l, j: (l, 0, j))],
        out_specs=[pl.BlockSpec((None, D, tn), lambda l, j: (l, 0, j)), pl.BlockSpec((None, 16, D), lambda l, j: (l, 0, 0))],
        compiler_params=_cp(2), name=name)(rows, w, dm)


def silu_grad(dsc, c, name):
    def body(d_ref, c_ref, o_ref):
        x = c_ref[...]
        s = jax.nn.sigmoid(x)
        o_ref[...] = d_ref[...] * (s * (1.0 + x * (1.0 - s)))

    return pl.pallas_call(body, out_shape=_sds(c.shape, F32), name=name)(dsc, c)


def _coords():
    return lax.axis_index("x"), lax.axis_index("y"), lax.axis_index("c")


def _flip(v, bit):
    return 1 - v if bit else v


def allgather8(x, name):
    R, C = x.shape

    def body(x_ref, o_ref, send_sems, recv_sems, local_sem):
        mx, my, mc = _coords()
        me = 4 * mx + 2 * my + mc
        mine = pltpu.make_async_copy(x_ref, o_ref.at[me], local_sem)
        mine.start()
        copies = []
        for k in range(1, 8):
            px, py, pc = _flip(mx, k & 4), _flip(my, k & 2), _flip(mc, k & 1)
            cp = pltpu.make_async_remote_copy(src_ref=x_ref, dst_ref=o_ref.at[me], send_sem=send_sems.at[k - 1],
                                              recv_sem=recv_sems.at[k - 1], device_id=(px, py, pc), device_id_type=MESH)
            cp.start()
            copies.append((cp, 4 * px + 2 * py + pc))
        for k, (cp, peer) in enumerate(copies):
            pltpu.make_async_remote_copy(src_ref=x_ref, dst_ref=o_ref.at[peer], send_sem=send_sems.at[k], recv_sem=recv_sems.at[k],
                                         device_id=(mx, my, mc), device_id_type=MESH).wait_recv()
        for cp, _ in copies:
            cp.wait_send()
        mine.wait()

    return pl.pallas_call(
        body, out_shape=_sds((8, R, C), F32), in_specs=[pl.BlockSpec(memory_space=pltpu.VMEM)],
        out_specs=pl.BlockSpec(memory_space=pltpu.VMEM),
        scratch_shapes=[pltpu.SemaphoreType.DMA((7,)), pltpu.SemaphoreType.DMA((7,)), pltpu.SemaphoreType.DMA],
        compiler_params=pltpu.CompilerParams(vmem_limit_bytes=VMEM_LIMIT), name=name)(x)


def chip_exchange(arrs, name, bcast=False):
    n = len(arrs)

    def src(ref, k):
        return ref if bcast else ref.at[k]

    def body(*refs):
        ins, outs = refs[:n], refs[n:2 * n]
        send_sems, recv_sems, local_sems = refs[2 * n:]
        mx, my, mc = _coords()
        me = 2 * mx + my
        started = []
        for a in range(n):
            mine = pltpu.make_async_copy(src(ins[a], me), outs[a].at[me], local_sems.at[a])
            mine.start()
            started.append(mine)
        sends = []
        for a in range(n):
            for k in range(1, 4):
                px, py = _flip(mx, k & 2), _flip(my, k & 1)
                cp = pltpu.make_async_remote_copy(src_ref=src(ins[a], 2 * px + py), dst_ref=outs[a].at[me], send_sem=send_sems.at[3 * a + k - 1],
                                                  recv_sem=recv_sems.at[3 * a + k - 1], device_id=(px, py, mc), device_id_type=MESH)
                cp.start()
                sends.append((cp, a, k, 2 * px + py))
        for cp, a, k, peer in sends:
            pltpu.make_async_remote_copy(src_ref=src(ins[a], me), dst_ref=outs[a].at[peer], send_sem=send_sems.at[3 * a + k - 1],
                                         recv_sem=recv_sems.at[3 * a + k - 1], device_id=(mx, my, mc), device_id_type=MESH).wait_recv()
        for cp, *_ in sends:
            cp.wait_send()
        for mine in started:
            mine.wait()

    hbm = pl.BlockSpec(memory_space=pl.ANY)
    return pl.pallas_call(
        body, out_shape=[_sds(((4,) + a.shape) if bcast else a.shape, a.dtype) for a in arrs], in_specs=[hbm] * n, out_specs=[hbm] * n,
        scratch_shapes=[pltpu.SemaphoreType.DMA((3 * n,)), pltpu.SemaphoreType.DMA((3 * n,)), pltpu.SemaphoreType.DMA((n,))],
        name=name)(*arrs)


def chip_allgather(arrs, name):
    n = len(arrs)

    def body(*refs):
        ins, outs = refs[:n], refs[n:2 * n]
        send_sems, recv_sems, pass_send, pass_recv, local_sems = refs[2 * n:]
        mx, my, mc = _coords()
        me = 2 * mx + my
        started = []
        for a in range(n):
            mine = pltpu.make_async_copy(ins[a], outs[a].at[me], local_sems.at[a])
            mine.start()
            started.append(mine)
        halves, sends = [], []
        for a in range(n):
            hl = arrs[a].shape[0] // 2
            halves.append((pl.ds(mc * hl, hl), pl.ds((1 - mc) * hl, hl)))
            for k in range(1, 4):
                px, py = _flip(mx, k & 2), _flip(my, k & 1)
                cp = pltpu.make_async_remote_copy(src_ref=ins[a].at[halves[a][0]], dst_ref=outs[a].at[me, halves[a][0]],
                                                  send_sem=send_sems.at[3 * a + k - 1], recv_sem=recv_sems.at[3 * a + k - 1],
                                                  device_id=(px, py, mc), device_id_type=MESH)
                cp.start()
                sends.append((cp, a, 3 * a + k - 1, 2 * px + py))
        passed = []
        for cp, a, s, peer in sends:
            got = outs[a].at[peer, halves[a][0]]
            pltpu.make_async_remote_copy(src_ref=got, dst_ref=got, send_sem=send_sems.at[s], recv_sem=recv_sems.at[s],
                                         device_id=(mx, my, mc), device_id_type=MESH).wait_recv()
            fw = pltpu.make_async_remote_copy(src_ref=got, dst_ref=got, send_sem=pass_send.at[s], recv_sem=pass_recv.at[s],
                                              device_id=(mx, my, 1 - mc), device_id_type=MESH)
            fw.start()
            passed.append(fw)
        for cp, a, s, peer in sends:
            theirs = outs[a].at[peer, halves[a][1]]
            pltpu.make_async_remote_copy(src_ref=theirs, dst_ref=theirs, send_sem=pass_send.at[s], recv_sem=pass_recv.at[s],
                                         device_id=(mx, my, mc), device_id_type=MESH).wait_recv()
        for cp, *_ in sends:
            cp.wait_send()
        for fw in passed:
            fw.wait_send()
        for mine in started:
            mine.wait()

    hbm = pl.BlockSpec(memory_space=pl.ANY)
    sems = pltpu.SemaphoreType.DMA((3 * n,))
    return pl.pallas_call(body, out_shape=[_sds((4,) + a.shape, a.dtype) for a in arrs], in_specs=[hbm] * n, out_specs=[hbm] * n,
                          scratch_shapes=[sems, sems, sems, sems, pltpu.SemaphoreType.DMA((n,))], name=name)(*arrs)


def sibling_swap_halves(arrs, name):
    n = len(arrs)

    def body(*refs):
        ins, outs = refs[:n], refs[n:2 * n]
        send_sems, recv_sems = refs[2 * n:]
        mx, my, mc = _coords()
        cps = []
        for a in range(n):
            hl = arrs[a].shape[1] // 2
            cp = pltpu.make_async_remote_copy(src_ref=ins[a].at[pl.ds(0, 4), pl.ds((1 - mc) * hl, hl)], dst_ref=outs[a],
                                              send_sem=send_sems.at[a], recv_sem=recv_sems.at[a],
                                              device_id=(mx, my, 1 - mc), device_id_type=MESH)
            cp.start()
            cps.append(cp)
        for cp in cps:
            cp.wait()

    hbm = pl.BlockSpec(memory_space=pl.ANY)
    return pl.pallas_call(body, out_shape=[_sds((4, a.shape[1] // 2) + a.shape[2:], a.dtype) for a in arrs], in_specs=[hbm] * n,
                          out_specs=[hbm] * n, scratch_shapes=[pltpu.SemaphoreType.DMA((n,)), pltpu.SemaphoreType.DMA((n,))], name=name)(*arrs)


def sibling_merge(arrs, name):
    n = len(arrs)

    def body(*refs):
        ins, outs = refs[:n], refs[n:2 * n]
        send_sems, recv_sems, local_sems = refs[2 * n:]
        mx, my, mc = _coords()
        cps = []
        for a in range(n):
            hl = arrs[a].shape[0]
            mine, theirs = outs[a].at[pl.ds(mc * hl, hl)], outs[a].at[pl.ds((1 - mc) * hl, hl)]
            loc = pltpu.make_async_copy(ins[a], mine, local_sems.at[a])
            loc.start()
            cp = pltpu.make_async_remote_copy(src_ref=ins[a], dst_ref=mine, send_sem=send_sems.at[a], recv_sem=recv_sems.at[a],
                                              device_id=(mx, my, 1 - mc), device_id_type=MESH)
            cp.start()
            cps.append((loc, cp, theirs))
        for a, (loc, cp, theirs) in enumerate(cps):
            cp.wait_send()
            pltpu.make_async_remote_copy(src_ref=theirs, dst_ref=theirs, send_sem=send_sems.at[a], recv_sem=recv_sems.at[a],
                                         device_id=(mx, my, mc), device_id_type=MESH).wait_recv()
            loc.wait()

    hbm = pl.BlockSpec(memory_space=pl.ANY)
    return pl.pallas_call(body, out_shape=[_sds((2 * a.shape[0],) + a.shape[1:], a.dtype) for a in arrs], in_specs=[hbm] * n, out_specs=[hbm] * n,
                          scratch_shapes=[pltpu.SemaphoreType.DMA((n,)), pltpu.SemaphoreType.DMA((n,)), pltpu.SemaphoreType.DMA((n,))],
                          name=name)(*arrs)


def add_pair(a, b, name, tr=512):
    R, C = a.shape
    tr = _div_tile(R, tr, 8)
    row = pl.BlockSpec((tr, C), lambda i: (i, 0))

    def body(a_ref, b_ref, o_ref):
        o_ref[...] = (a_ref[...].astype(F32) + b_ref[...].astype(F32)).astype(BF16)

    return pl.pallas_call(body, out_shape=_sds((R, C), BF16), grid=(R // tr,), in_specs=[row, row], out_specs=row,
                          compiler_params=_cp(1), name=name)(a, b)


def sum_slabs(x, name, tr=256):
    n, R, C = x.shape
    tr = _div_tile(R, tr, 8)

    def body(x_ref, o_ref):
        acc = x_ref[0].astype(F32)
        for k in range(1, n):
            acc = acc + x_ref[k].astype(F32)
        o_ref[...] = acc

    return pl.pallas_call(body, out_shape=_sds((R, C), F32), grid=(R // tr,), in_specs=[pl.BlockSpec((n, tr, C), lambda i: (0, i, 0))],
                          out_specs=pl.BlockSpec((tr, C), lambda i: (i, 0)), compiler_params=_cp(1), name=name)(x)


def adamw(w, g1, g2, m, v, name, tr=256):
    R, C = w.shape
    tr = _div_tile(R, tr, 8)
    row = pl.BlockSpec((tr, C), lambda i: (i, 0))
    ins = [w, g1] + ([g2] if g2 is not None else []) + [m, v]

    def body(*refs):
        w_ref, g_ref = refs[0], refs[1]
        g = g_ref[...]
        pos = 2
        if g2 is not None:
            g = g + refs[2][...]
            pos = 3
        m_ref, v_ref, go_ref, d_ref, mo_ref, vo_ref = refs[pos:]
        mn = ADAM_B1 * m_ref[...] + (1.0 - ADAM_B1) * g
        vn = ADAM_B2 * v_ref[...] + (1.0 - ADAM_B2) * (g * g)
        m_hat = mn / (1.0 - ADAM_B1 ** ADAM_STEP)
        v_hat = vn / (1.0 - ADAM_B2 ** ADAM_STEP)
        go_ref[...] = g
        d_ref[...] = -ADAM_LR * (m_hat / (jnp.sqrt(v_hat) + ADAM_EPS) + ADAM_WD * w_ref[...])
        mo_ref[...] = mn
        vo_ref[...] = vn

    return pl.pallas_call(body, out_shape=[_sds((R, C), F32)] * 4, grid=(R // tr,), in_specs=[row] * len(ins), out_specs=[row] * 4,
                          compiler_params=_cp(1), name=name)(*ins)


W_NAMES = ("c_ctx", "mod_w", "mod_b", "pre_mix_g", "post_mix_g", "pre_mlp_g", "post_mlp_g", "mlp_w1", "mlp_w2", "ssm_in_w",
           "ssm_conv_w", "ssm_conv_b", "ssm_a_log_f", "ssm_dt_bias_f", "ssm_d_f", "ssm_a_log_b", "ssm_dt_bias_b", "ssm_d_b",
           "ssm_norm_g", "ssm_out_w", "conf_pw1_w", "conf_pw1_b", "conf_dw_w", "conf_dw_b", "conf_ln_g", "conf_ln_b",
           "conf_pw2_w", "conf_pw2_b")
BIG = {"mlp_w1": "col", "mlp_w2": "row", "ssm_in_w": "col", "ssm_out_w": "row", "conf_pw1_w": "col", "conf_pw2_w": "row"}
SMALL_SHARDED = ("ssm_conv_w", "conf_pw1_b", "conf_dw_w", "conf_dw_b", "conf_ln_g", "conf_ln_b", "conf_pw2_b")
PACK_W = 1024


def _pack(arrs):
    flat = jnp.concatenate([a.reshape(-1).astype(F32) for a in arrs])
    n = flat.shape[0]
    tot = -(-n // (8 * PACK_W)) * (8 * PACK_W)
    return jnp.pad(flat, (0, tot - n)).reshape(tot // PACK_W, PACK_W)


def _unpack(buf, shapes):
    lead = buf.shape[:-2]
    flat = buf.reshape(lead + (-1,))
    out, off = [], 0
    for shp in shapes:
        n = 1
        for d in shp:
            n *= d
        out.append(flat[..., off:off + n].reshape(lead + tuple(shp)))
        off += n
    return out


def _full_from_chips(g, kind):
    if kind == "col":
        return jnp.moveaxis(g, 0, -2).reshape(g.shape[1:-1] + (4 * g.shape[-1],))
    return jnp.moveaxis(g, 0, 1).reshape((g.shape[1], 4 * g.shape[2]) + g.shape[3:])


def _chip_slabs(full, kind):
    if kind == "col":
        return jnp.moveaxis(full.reshape(full.shape[:-1] + (4, full.shape[-1] // 4)), -2, 0)
    return jnp.moveaxis(full.reshape((full.shape[0], 4, full.shape[1] // 4) + full.shape[2:]), 1, 0)


def _view2d(a):
    if a.ndim == 1:
        return a.reshape(1, -1)
    return a.reshape(-1, a.shape[-1])


def kernel(x, c, ctx, c_ctx, mod_w, mod_b, pre_mix_g, post_mix_g, pre_mlp_g, post_mlp_g, mlp_w1, mlp_w2, ssm_in_w, ssm_conv_w, ssm_conv_b, ssm_a_log_f, ssm_dt_bias_f, ssm_d_f, ssm_a_log_b, ssm_dt_bias_b, ssm_d_b, ssm_norm_g, ssm_out_w, conf_pw1_w, conf_pw1_b, conf_dw_w, conf_dw_b, conf_ln_g, conf_ln_b, conf_pw2_w, conf_pw2_b, loss_target, m_c_ctx, m_mod_w, m_mod_b, m_pre_mix_g, m_post_mix_g, m_pre_mlp_g, m_post_mlp_g, m_mlp_w1, m_mlp_w2, m_ssm_in_w, m_ssm_conv_w, m_ssm_conv_b, m_ssm_a_log_f, m_ssm_dt_bias_f, m_ssm_d_f, m_ssm_a_log_b, m_ssm_dt_bias_b, m_ssm_d_b, m_ssm_norm_g, m_ssm_out_w, m_conf_pw1_w, m_conf_pw1_b, m_conf_dw_w, m_conf_dw_b, m_conf_ln_g, m_conf_ln_b, m_conf_pw2_w, m_conf_pw2_b, v_c_ctx, v_mod_w, v_mod_b, v_pre_mix_g, v_post_mix_g, v_pre_mlp_g, v_post_mlp_g, v_mlp_w1, v_mlp_w2, v_ssm_in_w, v_ssm_conv_w, v_ssm_conv_b, v_ssm_a_log_f, v_ssm_dt_bias_f, v_ssm_d_f, v_ssm_a_log_b, v_ssm_dt_bias_b, v_ssm_d_b, v_ssm_norm_g, v_ssm_out_w, v_conf_pw1_w, v_conf_pw1_b, v_conf_dw_w, v_conf_dw_b, v_conf_ln_g, v_conf_ln_b, v_conf_pw2_w, v_conf_pw2_b):
    given = dict(locals())
    W = {n: given[n] for n in W_NAMES}
    L, D = x.shape[1], x.shape[2]
    Lc = ctx.shape[1]
    T = Lc + L
    depth = mod_w.shape[0]
    d_inner = ssm_norm_g.shape[1]
    H = ssm_a_log_f.shape[1]
    xbc = ssm_conv_b.shape[1]
    GN = (xbc - d_inner) // 2
    G = GN // N_STATE
    rows_grid = L // GRID_W
    tr = Lc
    ncc = Lc // CHUNK
    assert H == 8 * G and Lc % CHUNK == 0 and L % Lc == 0 and tr % GRID_W == 0 and tr % rows_grid == 0
    tc = _div_tile(GN, 512)
    assert d_inner % tc == 0
    mx, my, mc = _coords()
    chip = 2 * mx + my
    dev = 4 * mx + 2 * my + mc

    small_shapes = [(1, D)] + [W[n].shape for n in SMALL_SHARDED]
    got = allgather8(_pack([c] + [W[n] for n in SMALL_SHARDED]), "gather_small")
    parts = _unpack(got, small_shapes)
    c_all = parts[0].reshape(8, D)
    full_small = {n: jnp.concatenate([p[2 * k] for k in range(4)], axis=-1) for n, p in zip(SMALL_SHARDED, parts[1:])}

    cond = jnp.concatenate([c_all, c_ctx.reshape(1, D), jnp.zeros((7, D), F32)], axis=0)
    ncol = mod_w.shape[2]
    bsl = lax.dynamic_slice(mod_b, (0, chip * ncol), (depth, ncol)).reshape(depth, 1, ncol)
    m_loc = mod_fwd(cond, mod_w, bsl, "mod_fwd")
    m_all = allgather8(m_loc.reshape(depth * 16, ncol), "gather_mod").reshape(8, depth, 16, ncol)
    m_full = jnp.concatenate([m_all[2 * k] for k in range(4)], axis=-1)
    m_lat = lax.dynamic_slice(m_full, (0, dev, 0), (depth, 1, 6 * D))
    m2 = jnp.concatenate([m_full[:, 8:9], m_lat], axis=1)

    def six(i):
        return [m2[i, :, k * D:(k + 1) * D].reshape(2, 1, D) for k in range(6)]

    big_names = list(BIG)
    gathered = chip_allgather([W[n].astype(BF16) for n in big_names], "gather_weights")
    Wb = {n: _full_from_chips(g, BIG[n]) for n, g in zip(big_names, gathered)}

    def to_scan(u):
        lat = u[Lc:].reshape(rows_grid, GRID_W, u.shape[1]).swapaxes(0, 1).reshape(L, u.shape[1])
        return jnp.concatenate([u[:Lc], lat], axis=0)

    def from_scan(u):
        lat = u[Lc:].reshape(GRID_W, rows_grid, u.shape[1]).swapaxes(0, 1).reshape(L, u.shape[1])
        return jnp.concatenate([u[:Lc], lat], axis=0)

    def ssm_params(j):
        def two(f, b):
            return jnp.stack([f[j], b[j]]).reshape(2, G, 1, 8)
        return two(ssm_dt_bias_f, ssm_dt_bias_b), two(ssm_a_log_f, ssm_a_log_b), two(ssm_d_f, ssm_d_b)

    def dw3_of(j):
        w = full_small["conf_dw_w"][j]
        return w.reshape(w.shape[0], D // 128, 128).swapaxes(0, 1)

    h = jnp.concatenate([ctx[0], x[0]], axis=0)
    saved = []
    for i in range(depth):
        kind, j = i % 2, i // 2
        col_major = (j % 2) == 1
        sh1, sc1, g1, sh2, sc2, g2 = six(i)
        s = {"h": h}
        u = prenorm_fwd(h, pre_mix_g[i][None], sh1, sc1, tr, f"prenorm_mix{i}")
        if col_major:
            u = to_scan(u)
        s["u"] = u
        if kind == 0:
            proj = mm(u, Wb["ssm_in_w"][j], "nn", F32, name=f"ssm_in{i}")
            pre, act = conv5_fwd(proj, full_small["ssm_conv_w"][j], ssm_conv_b[j][None], d_inner // tc, tr, tc, f"ssm_conv{i}")
            dtr = proj[:, d_inner + xbc:].reshape(T, 2, G, 8).transpose(1, 2, 0, 3)
            bias, alog, dsk = ssm_params(j)
            y2, hsave = ssd_fwd(act, dtr, bias, alog, dsk, d_inner, ncc, f"ssd_fwd{i}")
            yn = gnorm_fwd(y2, proj, ssm_norm_g[j][None], CHUNK, f"ssm_gnorm{i}")
            out = mm(yn, Wb["ssm_out_w"][j], "nn", F32, name=f"ssm_out{i}")
            s.update(proj=proj, pre=pre, act=act, dtr=dtr, y2=y2, hsave=hsave, yn=yn)
        else:
            seg = rows_grid if col_major else GRID_W
            a = mm(u, Wb["conf_pw1_w"][j], "nn", F32, bias=full_small["conf_pw1_b"][j][None], name=f"conf_pw1_{i}")
            v = confmid_fwd(a, dw3_of(j), full_small["conf_dw_b"][j][None], full_small["conf_ln_g"][j][None],
                            full_small["conf_ln_b"][j][None], seg, tr, f"conf_mid{i}")
            out = mm(v, Wb["conf_pw2_w"][j], "nn", F32, bias=full_small["conf_pw2_b"][j][None], name=f"conf_pw2_{i}")
            s.update(a=a, v=v, seg=seg)
        if col_major:
            out = from_scan(out)
        h1 = post_fwd(h, out, post_mix_g[i][None], g1, tr, f"post_mix{i}")
        u2 = prenorm_fwd(h1, pre_mlp_g[i][None], sh2, sc2, tr, f"prenorm_mlp{i}")
        hid, actm = mm(u2, Wb["mlp_w1"][i], "nn", F32, relu2=True, name=f"mlp_up{i}")
        f = mm(actm, Wb["mlp_w2"][i], "nn", F32, name=f"mlp_down{i}")
        h = post_fwd(h1, f, post_mlp_g[i][None], g2, tr, f"post_mlp{i}")
        s.update(out=out, h1=h1, u2=u2, hid=hid, actm=actm, f=f)
        saved.append(s)

    loss_blk, Gr = loss_head(h, loss_target[0], tr, "loss_head")
    loss = lax.psum(loss_blk[0, 0], ("x", "y", "c"))

    gb = {n: [None] * W[n].shape[0] for n in BIG}
    gs = {n: [None] * W[n].shape[0] for n in W_NAMES if n not in BIG and n not in ("c_ctx", "mod_w", "mod_b")}
    dmod = [None] * depth
    for i in reversed(range(depth)):
        kind, j = i % 2, i // 2
        col_major = (j % 2) == 1
        sh1, sc1, g1, sh2, sc2, g2 = six(i)
        s = saved[i]
        df, gs["post_mlp_g"][i], dg2, _ = post_bwd(s["f"], post_mlp_g[i][None], g2, Gr, tr, f"post_mlp_bwd{i}")
        gb["mlp_w2"][i] = mm(s["actm"], df, "tn", BF16, name=f"mlp_down_wg{i}")
        dhid = mm(df, Wb["mlp_w2"][i], "nt", BF16, mul_relu=s["hid"], name=f"mlp_down_dg{i}")
        gb["mlp_w1"][i] = mm(s["u2"], dhid, "tn", BF16, name=f"mlp_up_wg{i}")
        du2 = mm(dhid, Wb["mlp_w1"][i], "nt", F32, name=f"mlp_up_dg{i}")
        Gr, gs["pre_mlp_g"][i], dsh2, dsc2 = prenorm_bwd(s["h1"], pre_mlp_g[i][None], sh2, sc2, du2, Gr, tr, f"prenorm_mlp_bwd{i}")
        dout, gs["post_mix_g"][i], dg1, dout_sum = post_bwd(s["out"], post_mix_g[i][None], g1, Gr, tr, f"post_mix_bwd{i}")
        if col_major:
            dout = to_scan(dout)
        if kind == 0:
            gb["ssm_out_w"][j] = mm(s["yn"], dout, "tn", BF16, name=f"ssm_out_wg{i}")
            dyn = mm(dout, Wb["ssm_out_w"][j], "nt", F32, name=f"ssm_out_dg{i}")
            dys, dz, gs["ssm_norm_g"][j] = gnorm_bwd(s["y2"], s["proj"], ssm_norm_g[j][None], dyn, CHUNK, f"ssm_gnorm_bwd{i}")
            bias, alog, dsk = ssm_params(j)
            dx2, db2, dc2, ddtr, dbias, dalog, ddsk = ssd_bwd(s["act"], s["dtr"], bias, alog, dsk, s["hsave"], dys, d_inner, ncc, f"ssd_bwd{i}")
            cw = full_small["ssm_conv_w"][j]
            nx, nb_ = d_inner // tc, GN // tc
            dxx, dwx, dbx = conv5_bwd(dx2, s["pre"], s["proj"], cw, 0, nx, tr, tc, f"ssm_conv_bwd_x{i}")
            dxb, dwb, dbb = conv5_bwd(db2, s["pre"], s["proj"], cw, nx, 2 * nx, tr, tc, f"ssm_conv_bwd_b{i}")
            dxc, dwc, dbc = conv5_bwd(dc2, s["pre"], s["proj"], cw, nx + nb_, 2 * nx + nb_, tr, tc, f"ssm_conv_bwd_c{i}")
            gs["ssm_conv_w"][j] = jnp.concatenate([dwx, dwb, dwc], axis=1)
            gs["ssm_conv_b"][j] = jnp.concatenate([dbx, dbb, dbc], axis=1)[0]
            ddt = ddtr.transpose(2, 0, 1, 3).reshape(T, 2 * H).astype(BF16)
            dproj = jnp.concatenate([dz, dxx, dxb, dxc, ddt], axis=1)
            gb["ssm_in_w"][j] = mm(s["u"], dproj, "tn", BF16, name=f"ssm_in_wg{i}")
            du = mm(dproj, Wb["ssm_in_w"][j], "nt", F32, name=f"ssm_in_dg{i}")
            for nm, val in (("ssm_dt_bias", dbias), ("ssm_a_log", dalog), ("ssm_d", ddsk)):
                gs[nm + "_f"][j] = val[0].reshape(H)
                gs[nm + "_b"][j] = val[1].reshape(H)
        else:
            gb["conf_pw2_w"][j] = mm(s["v"], dout, "tn", BF16, name=f"conf_pw2_wg{i}")
            gs["conf_pw2_b"][j] = dout_sum[0]
            dv = mm(dout, Wb["conf_pw2_w"][j], "nt", F32, name=f"conf_pw2_dg{i}")
            da, da_sum, dw3, ddb, dlg, dlb = confmid_bwd(s["a"], dw3_of(j), full_small["conf_dw_b"][j][None], full_small["conf_ln_g"][j][None],
                                                          full_small["conf_ln_b"][j][None], dv, s["seg"], tr, f"conf_mid_bwd{i}")
            gs["conf_pw1_b"][j] = da_sum[0]
            gs["conf_dw_w"][j] = dw3.swapaxes(0, 1).reshape(dw3.shape[1], D)
            gs["conf_dw_b"][j], gs["conf_ln_g"][j], gs["conf_ln_b"][j] = ddb[0], dlg[0], dlb[0]
            gb["conf_pw1_w"][j] = mm(s["u"], da, "tn", BF16, name=f"conf_pw1_wg{i}")
            du = mm(da, Wb["conf_pw1_w"][j], "nt", F32, name=f"conf_pw1_dg{i}")
        if col_major:
            du = from_scan(du)
        Gr, gs["pre_mix_g"][i], dsh1, dsc1 = prenorm_bwd(s["h"], pre_mix_g[i][None], sh1, sc1, du, Gr, tr, f"prenorm_mix_bwd{i}")
        dmod[i] = jnp.concatenate([t.reshape(2, D) for t in (dsh1, dsc1, dg1, dsh2, dsc2, dg2)], axis=1)
    grad_x = Gr[Lc:][None]

    small_names = list(gs)
    small_local = [jnp.stack([t.reshape(W[n].shape[1:] if n not in SMALL_SHARDED else t.shape) for t in gs[n]]) for n in small_names]
    small_shapes = [t.shape for t in small_local] + [(depth, 2, 6 * D)]
    got = allgather8(_pack(small_local + [jnp.stack(dmod)]), "gather_small_grads")
    summed = _unpack(sum_slabs(got, "sum_small_grads"), small_shapes)
    grads = {}
    for n, t in zip(small_names, summed[:-1]):
        if n in SMALL_SHARDED:
            w = W[n].shape[-1]
            t = lax.dynamic_slice_in_dim(t, chip * w, w, axis=t.ndim - 1)
        grads[n] = t
    grads["mod_b"] = summed[-1][:, 0] + summed[-1][:, 1]
    dm_all = _unpack(got, small_shapes)[-1]
    dm_ctx = sum_slabs(dm_all[:, :, 0], "sum_dmod_ctx")
    dm_rows = jnp.concatenate([dm_all[:, :, 1].swapaxes(0, 1), dm_ctx[:, None], jnp.zeros((depth, 7, 6 * D), F32)], axis=1)
    dm_mine = lax.dynamic_slice_in_dim(dm_rows, chip * ncol, ncol, axis=2)
    grads["mod_w"], dcond = mod_bwd(cond, mod_w, dm_mine, "mod_bwd")
    dcc = sum_slabs(dcond[:, 8:9], "sum_dcond_layers")
    dcc_all = allgather8(jnp.pad(dcc, ((0, 7), (0, 0))), "gather_dcond")
    dcc_sum = sum_slabs(dcc_all[0::2, 0:1], "sum_dcond_chips")
    grads["c_ctx"] = silu_grad(dcc_sum, c_ctx.reshape(1, D), "c_ctx_grad").reshape(D)

    slabs = [_chip_slabs(jnp.stack(gb[n]), BIG[n]) for n in big_names]
    theirs = sibling_swap_halves(slabs, "sibling_swap_grads")
    chip_part = []
    for n, s, t in zip(big_names, slabs, theirs):
        hl = t.shape[1]
        own = lax.dynamic_slice_in_dim(s, mc * hl, hl, axis=1)
        chip_part.append(add_pair(_view2d(own), _view2d(t), f"add_cores_{n}").reshape(t.shape))
    recv = chip_exchange(chip_part, "scatter_grads")
    half = [sum_slabs(r.reshape((4, -1, r.shape[-1])), f"sum_grads_{n}").reshape(r.shape[1:]) for n, r in zip(big_names, recv)]
    for n, g in zip(big_names, sibling_merge(half, "sibling_merge_grads")):
        grads[n] = g

    res = {}
    for n in W_NAMES:
        w2 = _view2d(W[n])
        cols = w2.shape[1]
        outs = adamw(w2, _view2d(grads[n]), None, _view2d(given["m_" + n]), _view2d(given["v_" + n]), f"adamw_{n}",
                     tr=max(8, (262144 // cols) // 8 * 8))
        res[n] = [o.reshape(W[n].shape) for o in outs]
    return (loss, grad_x, *[res[n][0] for n in W_NAMES], *[res[n][1] for n in W_NAMES], *[res[n][2] for n in W_NAMES],
            *[res[n][3] for n in W_NAMES])
```

```python
import functools

import jax
import jax.numpy as jnp
from jax import lax
from jax.experimental import pallas as pl
from jax.experimental.pallas import tpu as pltpu

GRID_W = 64
CHUNK = 128
N_STATE = 128
EPS = 1e-6
ADAM_LR, ADAM_B1, ADAM_B2, ADAM_EPS, ADAM_WD, ADAM_STEP = 0.001, 0.9, 0.999, 1e-08, 0.01, 10
VMEM_LIMIT = 56 * 1024 * 1024
F32, BF16 = jnp.float32, jnp.bfloat16
HI = lax.Precision.HIGHEST
SPREAD = lax.Precision.HIGH
MESH = pl.DeviceIdType.MESH


def _cp(n_grid):
    return pltpu.CompilerParams(dimension_semantics=("arbitrary",) * n_grid, vmem_limit_bytes=VMEM_LIMIT)


def _sds(shape, dtype):
    return jax.ShapeDtypeStruct(tuple(shape), dtype)


def _div_tile(n, target, unit=128):
    best = None
    t = unit
    while t <= min(n, target):
        if n % t == 0:
            best = t
        t += unit
    return best if best is not None else n


def _rms(x, g):
    return x * lax.rsqrt(jnp.mean(x * x, axis=-1, keepdims=True) + EPS) * g


def _silu(x):
    return x * jax.nn.sigmoid(x)


def mm(a, b, mode, out_dtype, *, bias=None, relu2=False, mul_relu=None, name, tm=768, tn=1152, tk=2048):
    if mode == "nn":
        (M, C), (_, N) = a.shape, b.shape
    elif mode == "nt":
        (M, C), (N, _) = a.shape, b.shape
    else:
        (C, M), (_, N) = a.shape, b.shape
    if mode == "tn":
        tm, tn, tk = _div_tile(M, 1024), _div_tile(N, tn), _div_tile(C, tk, 8)
    else:
        tm, tn, tk = _div_tile(M, tm, 8), _div_tile(N, tn), _div_tile(C, tk)
    nk = C // tk
    a_spec = {"nn": pl.BlockSpec((tm, tk), lambda i, j, k: (i, k)), "nt": pl.BlockSpec((tm, tk), lambda i, j, k: (i, k)),
              "tn": pl.BlockSpec((tk, tm), lambda i, j, k: (k, i))}[mode]
    b_spec = {"nn": pl.BlockSpec((tk, tn), lambda i, j, k: (k, j)), "nt": pl.BlockSpec((tn, tk), lambda i, j, k: (j, k)),
              "tn": pl.BlockSpec((tk, tn), lambda i, j, k: (k, j))}[mode]
    dims = {"nn": (((1,), (0,)), ((), ())), "nt": (((1,), (1,)), ((), ())), "tn": (((0,), (0,)), ((), ()))}[mode]
    ins, specs = [a, b], [a_spec, b_spec]
    if bias is not None:
        ins.append(bias)
        specs.append(pl.BlockSpec((1, tn), lambda i, j, k: (0, j)))
    if mul_relu is not None:
        ins.append(mul_relu)
        specs.append(pl.BlockSpec((tm, tn), lambda i, j, k: (i, j)))
    o_spec = pl.BlockSpec((tm, tn), lambda i, j, k: (i, j))
    outs, out_specs = [_sds((M, N), out_dtype)], [o_spec]
    if relu2:
        outs.append(_sds((M, N), BF16))
        out_specs.append(o_spec)

    def body(*refs):
        a_ref, b_ref = refs[0], refs[1]
        pos = 2
        bias_ref = mr_ref = None
        if bias is not None:
            bias_ref = refs[pos]
            pos += 1
        if mul_relu is not None:
            mr_ref = refs[pos]
            pos += 1
        o_ref = refs[pos]
        o2_ref = refs[pos + 1] if relu2 else None
        part = lax.dot_general(a_ref[...].astype(BF16), b_ref[...].astype(BF16), dims, preferred_element_type=F32)

        def finish(r):
            if bias_ref is not None:
                r = r + bias_ref[...]
            if mr_ref is not None:
                r = r * (2.0 * jnp.maximum(mr_ref[...], 0.0))
            o_ref[...] = r.astype(o_ref.dtype)
            if o2_ref is not None:
                q = jnp.maximum(r, 0.0)
                o2_ref[...] = (q * q).astype(BF16)

        if nk == 1:
            finish(part)
        else:
            acc_ref = refs[-1]
            k = pl.program_id(2)

            @pl.when(k == 0)
            def _():
                acc_ref[...] = part

            @pl.when(jnp.logical_and(k > 0, k < nk - 1))
            def _():
                acc_ref[...] += part

            @pl.when(k == nk - 1)
            def _():
                finish(acc_ref[...] + part)

    res = pl.pallas_call(body, out_shape=outs, grid=(M // tm, N // tn, nk), in_specs=specs, out_specs=out_specs,
                         scratch_shapes=[pltpu.VMEM((tm, tn), F32)] if nk > 1 else [], compiler_params=_cp(3), name=name)(*ins)
    return res if relu2 else res[0]


def _seg_spec(D):
    return pl.BlockSpec((None, 1, D), lambda i: (jnp.minimum(i, 1), 0, 0))


def _prenorm_fn(h, g, sh, sc):
    return _rms(h, g) * (1.0 + sc) + sh


def prenorm_fwd(h, g, sh, sc, tr, name):
    T, D = h.shape
    row = pl.BlockSpec((tr, D), lambda i: (i, 0))
    vec = pl.BlockSpec((1, D), lambda i: (0, 0))

    def body(h_ref, g_ref, sh_ref, sc_ref, u_ref):
        u_ref[...] = _prenorm_fn(h_ref[...], g_ref[...], sh_ref[...], sc_ref[...]).astype(BF16)

    return pl.pallas_call(body, out_shape=_sds((T, D), BF16), grid=(T // tr,), in_specs=[row, vec, _seg_spec(D), _seg_spec(D)],
                          out_specs=row, compiler_params=_cp(1), name=name)(h, g, sh, sc)


def _acc(ref, val, first):
    @pl.when(first)
    def _():
        ref[...] = val

    @pl.when(jnp.logical_not(first))
    def _():
        ref[...] += val


def prenorm_bwd(h, g, sh, sc, du, G, tr, name):
    T, D = h.shape
    row = pl.BlockSpec((tr, D), lambda i: (i, 0))
    vec = pl.BlockSpec((1, D), lambda i: (0, 0))

    def body(h_ref, g_ref, sh_ref, sc_ref, du_ref, G_ref, Go_ref, dg_ref, dsh_ref, dsc_ref):
        i = pl.program_id(0)
        _, vjp = jax.vjp(_prenorm_fn, h_ref[...], g_ref[...], sh_ref[...], sc_ref[...])
        dh, dg, dsh, dsc = vjp(du_ref[...].astype(F32))
        Go_ref[...] = G_ref[...] + dh
        _acc(dg_ref, dg, i == 0)
        _acc(dsh_ref, dsh, i <= 1)
        _acc(dsc_ref, dsc, i <= 1)

    return pl.pallas_call(
        body, out_shape=[_sds((T, D), F32), _sds((1, D), F32), _sds((2, 1, D), F32), _sds((2, 1, D), F32)], grid=(T // tr,),
        in_specs=[row, vec, _seg_spec(D), _seg_spec(D), row, row], out_specs=[row, vec, _seg_spec(D), _seg_spec(D)],
        compiler_params=_cp(1), name=name)(h, g, sh, sc, du, G)


def _post_fn(y, gp, gate):
    return gate * _rms(y, gp)


def post_fwd(h, y, gp, gate, tr, name):
    T, D = h.shape
    row = pl.BlockSpec((tr, D), lambda i: (i, 0))
    vec = pl.BlockSpec((1, D), lambda i: (0, 0))

    def body(h_ref, y_ref, gp_ref, gate_ref, o_ref):
        o_ref[...] = h_ref[...] + _post_fn(y_ref[...], gp_ref[...], gate_ref[...])

    return pl.pallas_call(body, out_shape=_sds((T, D), F32), grid=(T // tr,), in_specs=[row, row, vec, _seg_spec(D)],
                          out_specs=row, compiler_params=_cp(1), name=name)(h, y, gp, gate)


def post_bwd(y, gp, gate, G, tr, name):
    T, D = y.shape
    row = pl.BlockSpec((tr, D), lambda i: (i, 0))
    vec = pl.BlockSpec((1, D), lambda i: (0, 0))

    def body(y_ref, gp_ref, gate_ref, G_ref, dy_ref, dgp_ref, dgate_ref, dsum_ref):
        i = pl.program_id(0)
        _, vjp = jax.vjp(_post_fn, y_ref[...], gp_ref[...], gate_ref[...])
        dy, dgp, dgate = vjp(G_ref[...])
        dy_ref[...] = dy.astype(BF16)
        _acc(dgp_ref, dgp, i == 0)
        _acc(dgate_ref, dgate, i <= 1)
        _acc(dsum_ref, jnp.sum(dy, axis=0, keepdims=True), i == 0)

    return pl.pallas_call(
        body, out_shape=[_sds((T, D), BF16), _sds((1, D), F32), _sds((2, 1, D), F32), _sds((1, D), F32)], grid=(T // tr,),
        in_specs=[row, vec, _seg_spec(D), row], out_specs=[row, vec, _seg_spec(D), vec], compiler_params=_cp(1), name=name)(y, gp, gate, G)


def loss_head(h, target, tr, name):
    T, D = h.shape
    row = pl.BlockSpec((tr, D), lambda i: (i, 0))
    trow = pl.BlockSpec((tr, D), lambda i: (jnp.maximum(i - 1, 0), 0))

    def body(h_ref, t_ref, loss_ref, G_ref):
        i = pl.program_id(0)

        @pl.when(i == 0)
        def _():
            loss_ref[...] = jnp.zeros_like(loss_ref)
            G_ref[...] = jnp.zeros_like(G_ref)

        @pl.when(i > 0)
        def _():
            e = h_ref[...] - t_ref[...]
            G_ref[...] = e * (1.0 / D)
            loss_ref[...] += jnp.sum(e * e) * (0.5 / D)

    return pl.pallas_call(body, out_shape=[_sds((8, 128), F32), _sds((T, D), F32)], grid=(T // tr,), in_specs=[row, trow],
                          out_specs=[pl.BlockSpec((8, 128), lambda i: (0, 0)), row], compiler_params=_cp(1), name=name)(h, target)


def _halo_specs(tr, tc, T, col0, lead=()):
    n8 = tr // 8
    nl = len(lead)
    cur = pl.BlockSpec(lead + (tr, tc), lambda j, i: (0,) * nl + (i, col0 + j))
    prev = pl.BlockSpec(lead + (8, tc), lambda j, i: (0,) * nl + (jnp.maximum(i * n8 - 1, 0), col0 + j))
    nxt = pl.BlockSpec(lead + (8, tc), lambda j, i: (0,) * nl + (jnp.minimum((i + 1) * n8, T // 8 - 1), col0 + j))
    return [cur, prev, nxt]


def _with_halo(cur, prev, nxt, i, nt):
    keep_prev = (i >= 2).astype(cur.dtype)
    keep_next = jnp.logical_and(i >= 1, i < nt - 1).astype(cur.dtype)
    return jnp.concatenate([prev * keep_prev, cur, nxt * keep_next], axis=0)


def _shift_rows(ext, o, tr):
    n = ext.shape[0]
    return pltpu.roll(ext, (-o) % n, 0)[8:8 + tr]


def conv5_fwd(proj, w, b, col0, tr, tc, name):
    T = proj.shape[0]
    K, C = w.shape
    nt = T // tr

    def body(x_ref, xp_ref, xn_ref, w_ref, b_ref, pre_ref, act_ref):
        i = pl.program_id(1)
        ext = _with_halo(x_ref[...], xp_ref[...], xn_ref[...], i, nt)
        wv = w_ref[...]
        acc = jnp.zeros((tr, tc), F32) + b_ref[...]
        for k in range(K):
            acc = acc + wv[k:k + 1, :] * _shift_rows(ext, k - K // 2, tr)
        pre_ref[...] = acc
        act_ref[...] = _silu(acc)

    out = pl.BlockSpec((tr, tc), lambda j, i: (i, j))
    return pl.pallas_call(
        body, out_shape=[_sds((T, C), F32), _sds((T, C), F32)], grid=(C // tc, nt),
        in_specs=_halo_specs(tr, tc, T, col0) + [pl.BlockSpec((K, tc), lambda j, i: (0, j)), pl.BlockSpec((1, tc), lambda j, i: (0, j))],
        out_specs=[out, out], compiler_params=_cp(2), name=name)(proj, proj, proj, w, b)


def conv5_bwd(dact, pre, proj, w, colp, colx, tr, tc, name):
    _, T, Cp = dact.shape
    K = w.shape[0]
    nt = T // tr

    def body(d_ref, dp_ref, dn_ref, p_ref, pp_ref, pn_ref, x_ref, xp_ref, xn_ref, w_ref, dx_ref, dw_ref, db_ref):
        i = pl.program_id(1)

        def dpre_of(d, p):
            s = jax.nn.sigmoid(p)
            return (d[0] + d[1]) * (s * (1.0 + p * (1.0 - s)))

        dext = _with_halo(dpre_of(d_ref[...], p_ref[...]), dpre_of(dp_ref[...], pp_ref[...]), dpre_of(dn_ref[...], pn_ref[...]), i, nt)
        xext = _with_halo(x_ref[...], xp_ref[...], xn_ref[...], i, nt)
        dcur = dext[8:8 + tr]
        wv = w_ref[...]
        dx = jnp.zeros((tr, tc), F32)
        for k in range(K):
            o = k - K // 2
            dx = dx + wv[k:k + 1, :] * _shift_rows(dext, -o, tr)
            _acc(dw_ref.at[k:k + 1, :], jnp.sum(dcur * _shift_rows(xext, o, tr), axis=0, keepdims=True), i == 0)
        dx_ref[...] = dx.astype(BF16)
        _acc(db_ref, jnp.sum(dcur, axis=0, keepdims=True), i == 0)

    out = pl.BlockSpec((tr, tc), lambda j, i: (i, j))
    return pl.pallas_call(
        body, out_shape=[_sds((T, Cp), BF16), _sds((K, Cp), F32), _sds((1, Cp), F32)], grid=(Cp // tc, nt),
        in_specs=_halo_specs(tr, tc, T, 0, lead=(2,)) + _halo_specs(tr, tc, T, colp) + _halo_specs(tr, tc, T, colx)
        + [pl.BlockSpec((K, tc), lambda j, i: (0, colp + j))],
        out_specs=[out, pl.BlockSpec((K, tc), lambda j, i: (0, j)), pl.BlockSpec((1, tc), lambda j, i: (0, j))],
        compiler_params=_cp(2), name=name)(dact, dact, dact, pre, pre, pre, proj, proj, proj, w)


def _ssd_chunk(xg, bg, cg, dtr, hin, bias, alog, dsk, rev):
    Q, P8 = xg.shape
    nh = dtr.shape[1]
    P = P8 // nh
    N = bg.shape[1]
    dt = jax.nn.softplus(dtr + bias)
    da = dt * (-jnp.exp(alog))
    r_i = lax.broadcasted_iota(jnp.int32, (Q, Q), 0)
    c_i = lax.broadcasted_iota(jnp.int32, (Q, Q), 1)
    mask = jnp.where(rev, c_i - r_i, r_i - c_i) >= 0
    cs = jnp.dot(mask.astype(F32), da, precision=HI, preferred_element_type=F32)
    cs_t = cs.T
    expand = (lax.broadcasted_iota(jnp.int32, (nh, P8), 0) == lax.broadcasted_iota(jnp.int32, (nh, P8), 1) // P).astype(F32)
    blk = (lax.broadcasted_iota(jnp.int32, (nh, nh * Q), 0) == lax.broadcasted_iota(jnp.int32, (nh, nh * Q), 1) // Q).astype(F32)

    def over_lanes(v):
        return jnp.dot(v, expand, precision=SPREAD, preferred_element_type=F32)

    tot = jnp.where(rev, cs[0:1, :], cs[Q - 1:Q, :])
    dt_x, cs_x, tot_x, dsk_x = over_lanes(dt), over_lanes(cs), over_lanes(tot), over_lanes(dsk)
    lhs = jnp.concatenate([cs, jnp.ones((Q, nh), F32)], axis=1)
    rhs = jnp.concatenate([blk, -blk * jnp.concatenate([cs_t] * nh, axis=1)], axis=0)
    seg = jnp.dot(lhs, rhs, precision=SPREAD, preferred_element_type=F32)
    decay = jnp.exp(jnp.where(jnp.concatenate([mask] * nh, axis=1), seg, -jnp.inf))
    scores = lax.dot_general(cg.astype(BF16), bg.astype(BF16), (((1,), (1,)), ((), ())), preferred_element_type=F32)
    m_all = (jnp.concatenate([scores] * nh, axis=1) * decay).astype(BF16)
    xdt = xg * dt_x
    xdt_b = xdt.astype(BF16)
    xde = (xdt * jnp.exp(tot_x - cs_x)).astype(BF16)
    e_in = jnp.exp(cs_x)
    low = lax.broadcasted_iota(jnp.int32, (1, 2 * P), 1) < P
    cb, bb = cg.astype(BF16), bg.astype(BF16)
    zero = jnp.zeros((), BF16)
    ys, sts = [], []
    for p in range(nh // 2):
        sl = slice(2 * p * P, 2 * (p + 1) * P)
        xp = xdt_b[:, sl]
        y = jnp.dot(m_all[:, 2 * p * Q:(2 * p + 1) * Q], jnp.where(low, xp, zero), preferred_element_type=F32)
        y = y + jnp.dot(m_all[:, (2 * p + 1) * Q:(2 * p + 2) * Q], jnp.where(low, zero, xp), preferred_element_type=F32)
        y = y + lax.dot_general(cb, hin[sl, :].astype(BF16), (((1,), (1,)), ((), ())), preferred_element_type=F32) * e_in[:, sl]
        ys.append(y)
        sts.append(lax.dot_general(xde[:, sl], bb, (((0,), (0,)), ((), ())), preferred_element_type=F32))
    y = jnp.concatenate(ys, axis=1) + dsk_x * xg
    tot_c = jnp.where(rev, cs_t[:, 0:1], cs_t[:, Q - 1:Q])
    etot = lax.dot_general(expand, jnp.broadcast_to(jnp.exp(tot_c), (nh, N)), (((0,), (0,)), ((), ())), precision=SPREAD,
                           preferred_element_type=F32)
    return y, etot * hin + jnp.concatenate(sts, axis=0)


def _ssd_specs(Q, P8, N, ncc, NC, xcol_b, xcol_c, back):
    def chunk(d, s):
        s = (NC - 1 - s) if back else s
        return jnp.where(d == 0, s, jnp.where(s < ncc, ncc - 1 - s, ncc + NC - 1 - s))

    def step(s):
        return (NC - 1 - s) if back else s

    x = pl.BlockSpec((Q, P8), lambda d, g, s: (chunk(d, s), g))
    bsp = pl.BlockSpec((Q, N), lambda d, g, s: (chunk(d, s), xcol_b + g))
    csp = pl.BlockSpec((Q, N), lambda d, g, s: (chunk(d, s), xcol_c + g))
    dt = pl.BlockSpec((None, None, Q, 8), lambda d, g, s: (d, g, chunk(d, s), 0))
    par = pl.BlockSpec((None, None, 1, 8), lambda d, g, s: (d, g, 0, 0))
    hst = pl.BlockSpec((None, None, None, P8, N), lambda d, g, s: (d, g, step(s), 0, 0))
    yd = pl.BlockSpec((None, Q, P8), lambda d, g, s: (d, chunk(d, s), g))
    bd = pl.BlockSpec((None, Q, N), lambda d, g, s: (d, chunk(d, s), g))
    return x, bsp, csp, dt, par, hst, yd, bd


def ssd_fwd(act, dtr, bias, alog, dsk, d_inner, ncc, name):
    T = act.shape[0]
    G = dtr.shape[1]
    Q, N = CHUNK, N_STATE
    NC = T // Q
    P8 = d_inner // G
    x, bsp, csp, dt, par, hst, yd, _ = _ssd_specs(Q, P8, N, ncc, NC, d_inner // N, d_inner // N + G, False)

    def body(x_ref, b_ref, c_ref, dt_ref, bias_ref, alog_ref, dsk_ref, y_ref, h_ref, st_ref):
        d, s = pl.program_id(0), pl.program_id(2)

        @pl.when(s == 0)
        def _():
            st_ref[...] = jnp.zeros_like(st_ref)

        hin = st_ref[...]
        h_ref[...] = hin
        y, ho = _ssd_chunk(x_ref[...], b_ref[...], c_ref[...], dt_ref[...], hin, bias_ref[...], alog_ref[...], dsk_ref[...], d == 1)
        y_ref[...] = y
        st_ref[...] = ho

    return pl.pallas_call(
        body, out_shape=[_sds((2, T, d_inner), F32), _sds((2, G, NC, P8, N), F32)], grid=(2, G, NC),
        in_specs=[x, bsp, csp, dt, par, par, par], out_specs=[yd, hst], scratch_shapes=[pltpu.VMEM((P8, N), F32)],
        compiler_params=_cp(3), name=name)(act, act, act, dtr, bias, alog, dsk)


def ssd_bwd(act, dtr, bias, alog, dsk, hsave, dy, d_inner, ncc, name):
    T = act.shape[0]
    G = dtr.shape[1]
    Q, N = CHUNK, N_STATE
    NC = T // Q
    P8 = d_inner // G
    x, bsp, csp, dt, par, hst, yd, bd = _ssd_specs(Q, P8, N, ncc, NC, d_inner // N, d_inner // N + G, True)
    dysp = pl.BlockSpec((Q, P8), x.index_map)

    def body(x_ref, b_ref, c_ref, dt_ref, bias_ref, alog_ref, dsk_ref, h_ref, dy_ref,
             dx_ref, db_ref, dc_ref, ddt_ref, dbias_ref, dalog_ref, ddsk_ref, dh_ref):
        d, s = pl.program_id(0), pl.program_id(2)

        @pl.when(s == 0)
        def _():
            dh_ref[...] = jnp.zeros_like(dh_ref)

        args = (x_ref[...], b_ref[...], c_ref[...], dt_ref[...], h_ref[...], bias_ref[...], alog_ref[...], dsk_ref[...])

        _, vjp = jax.vjp(functools.partial(_ssd_chunk, rev=d == 1), *args)
        dx, db, dc, ddt, dhin, dbias, dalog, ddsk = vjp((dy_ref[...], dh_ref[...]))
        dx_ref[...] = dx
        db_ref[...] = db
        dc_ref[...] = dc
        ddt_ref[...] = ddt
        dh_ref[...] = dhin
        _acc(dbias_ref, dbias, s == 0)
        _acc(dalog_ref, dalog, s == 0)
        _acc(ddsk_ref, ddsk, s == 0)

    GN = G * N
    return pl.pallas_call(
        body,
        out_shape=[_sds((2, T, d_inner), F32), _sds((2, T, GN), F32), _sds((2, T, GN), F32), _sds(dtr.shape, F32),
                   _sds(bias.shape, F32), _sds(bias.shape, F32), _sds(bias.shape, F32)],
        grid=(2, G, NC), in_specs=[x, bsp, csp, dt, par, par, par, hst, dysp], out_specs=[yd, bd, bd, dt, par, par, par],
        scratch_shapes=[pltpu.VMEM((P8, N), F32)], compiler_params=_cp(3), name=name)(act, act, act, dtr, bias, alog, dsk, hsave, dy)


def _gnorm_fn(yf, yb, z, g):
    return _rms((yf + yb) * _silu(z), g)


def gnorm_fwd(y2, proj, g, tr, name):
    _, T, C = y2.shape
    yf = pl.BlockSpec((None, tr, C), lambda i: (0, i, 0))
    yb = pl.BlockSpec((None, tr, C), lambda i: (1, i, 0))
    row = pl.BlockSpec((tr, C), lambda i: (i, 0))
    vec = pl.BlockSpec((1, C), lambda i: (0, 0))

    def body(yf_ref, yb_ref, z_ref, g_ref, o_ref):
        o_ref[...] = _gnorm_fn(yf_ref[...], yb_ref[...], z_ref[...], g_ref[...]).astype(BF16)

    return pl.pallas_call(body, out_shape=_sds((T, C), BF16), grid=(T // tr,), in_specs=[yf, yb, row, vec], out_specs=row,
                          compiler_params=_cp(1), name=name)(y2, y2, proj, g)


def gnorm_bwd(y2, proj, g, dyn, tr, name):
    _, T, C = y2.shape
    yf = pl.BlockSpec((None, tr, C), lambda i: (0, i, 0))
    yb = pl.BlockSpec((None, tr, C), lambda i: (1, i, 0))
    row = pl.BlockSpec((tr, C), lambda i: (i, 0))
    vec = pl.BlockSpec((1, C), lambda i: (0, 0))

    def body(yf_ref, yb_ref, z_ref, g_ref, d_ref, dy_ref, dz_ref, dg_ref):
        i = pl.program_id(0)
        _, vjp = jax.vjp(_gnorm_fn, yf_ref[...], yb_ref[...], z_ref[...], g_ref[...])
        dyf, _, dz, dg = vjp(d_ref[...].astype(F32))
        dy_ref[...] = dyf
        dz_ref[...] = dz.astype(BF16)
        _acc(dg_ref, dg, i == 0)

    return pl.pallas_call(body, out_shape=[_sds((T, C), F32), _sds((T, C), BF16), _sds((1, C), F32)], grid=(T // tr,),
                          in_specs=[yf, yb, row, vec, row], out_specs=[row, row, vec], compiler_params=_cp(1), name=name)(y2, y2, proj, g, dyn)


def _glu_fn(a):
    D = a.shape[1] // 2
    return a[:, :D] * jax.nn.sigmoid(a[:, D:])


def _ln_swish_fn(v, g, b):
    mu = jnp.mean(v, axis=-1, keepdims=True)
    xc = v - mu
    var = jnp.mean(xc * xc, axis=-1, keepdims=True)
    y = xc * lax.rsqrt(var + EPS) * g + b
    return y * jax.nn.sigmoid(y)


def _seg_pos(tr, seg, i):
    p = lax.broadcasted_iota(jnp.int32, (tr, 1), 0)
    s = jnp.where(i == 0, tr, seg)
    return p & (s - 1), s


def _dw_taps(v, w, pos, s, sign):
    tr = v.shape[0]
    K = w.shape[0]
    acc = jnp.zeros_like(v)
    for k in range(K):
        o = sign * (k - K // 2)
        q = pos + o
        ok = jnp.logical_and(q >= 0, q < s).astype(v.dtype)
        acc = acc + w[k:k + 1, :] * (pltpu.roll(v, (-o) % tr, 0) * ok)
    return acc


def _lane_blocks(v, ref):
    for c in range(v.shape[1] // 128):
        ref[c] = v[:, c * 128:(c + 1) * 128]


def _from_lane_blocks(ref):
    return jnp.concatenate([ref[c] for c in range(ref.shape[0])], axis=1)


def confmid_fwd(a, w3, b, lg, lb, seg, tr, name):
    T, D2 = a.shape
    D = D2 // 2
    nb, K, _ = w3.shape
    vec = pl.BlockSpec((1, D), lambda i: (0, 0))
    row = pl.BlockSpec((tr, D), lambda i: (i, 0))

    def body(a_ref, w_ref, b_ref, lg_ref, lb_ref, o_ref, v1_ref, s0_ref, s1_ref):
        i = pl.program_id(0)
        pos, s = _seg_pos(tr, seg, i)
        _lane_blocks(_glu_fn(a_ref[...]), s0_ref)

        def blk(c, carry):
            s1_ref[c] = _dw_taps(s0_ref[c], w_ref[c], pos, s, 1)
            return carry

        lax.fori_loop(0, nb, blk, 0)
        v1 = _from_lane_blocks(s1_ref) + b_ref[...]
        v1_ref[...] = v1
        o_ref[...] = _ln_swish_fn(v1, lg_ref[...], lb_ref[...]).astype(BF16)

    return pl.pallas_call(
        body, out_shape=[_sds((T, D), BF16), _sds((T, D), F32)], grid=(T // tr,),
        in_specs=[pl.BlockSpec((tr, D2), lambda i: (i, 0)), pl.BlockSpec((nb, K, 128), lambda i: (0, 0, 0)), vec, vec, vec],
        out_specs=[row, row], scratch_shapes=[pltpu.VMEM((nb, tr, 128), F32)] * 2, compiler_params=_cp(1), name=name)(a, w3, b, lg, lb)


def confmid_bwd(a, v1, w3, lg, lb, dv, seg, tr, name):
    T, D2 = a.shape
    D = D2 // 2
    nb, K, _ = w3.shape
    vec = pl.BlockSpec((1, D), lambda i: (0, 0))
    vec2 = pl.BlockSpec((1, D2), lambda i: (0, 0))
    wsp = pl.BlockSpec((nb, K, 128), lambda i: (0, 0, 0))
    row = pl.BlockSpec((tr, D), lambda i: (i, 0))

    def body(a_ref, v1_ref, w_ref, lg_ref, lb_ref, dv_ref, da_ref, dsum_ref, dw_ref, db_ref, dlg_ref, dlb_ref, s0_ref, s1_ref, s2_ref):
        i = pl.program_id(0)
        first = i == 0
        pos, s = _seg_pos(tr, seg, i)
        v0, glu_vjp = jax.vjp(_glu_fn, a_ref[...])
        _lane_blocks(v0, s0_ref)
        _, ln_vjp = jax.vjp(_ln_swish_fn, v1_ref[...], lg_ref[...], lb_ref[...])
        dv1, dlg, dlb = ln_vjp(dv_ref[...].astype(F32))
        _acc(db_ref, jnp.sum(dv1, axis=0, keepdims=True), first)
        _acc(dlg_ref, dlg, first)
        _acc(dlb_ref, dlb, first)
        _lane_blocks(dv1, s2_ref)

        @pl.when(first)
        def _():
            dw_ref[...] = jnp.zeros_like(dw_ref)

        def conv_t(c, carry):
            d1, v0c = s2_ref[c], s0_ref[c]
            s1_ref[c] = _dw_taps(d1, w_ref[c], pos, s, -1)
            for k in range(K):
                o = k - K // 2
                q = pos + o
                ok = jnp.logical_and(q >= 0, q < s).astype(F32)
                dw_ref[c, k:k + 1, :] += jnp.sum(d1 * (pltpu.roll(v0c, (-o) % tr, 0) * ok), axis=0, keepdims=True)
            return carry

        lax.fori_loop(0, nb, conv_t, 0)
        (da,) = glu_vjp(_from_lane_blocks(s1_ref))
        da_ref[...] = da.astype(BF16)
        _acc(dsum_ref, jnp.sum(da, axis=0, keepdims=True), first)

    return pl.pallas_call(
        body, out_shape=[_sds((T, D2), BF16), _sds((1, D2), F32), _sds((nb, K, 128), F32), _sds((1, D), F32), _sds((1, D), F32), _sds((1, D), F32)],
        grid=(T // tr,), in_specs=[pl.BlockSpec((tr, D2), lambda i: (i, 0)), row, wsp, vec, vec, row],
        out_specs=[pl.BlockSpec((tr, D2), lambda i: (i, 0)), vec2, wsp, vec, vec, vec], scratch_shapes=[pltpu.VMEM((nb, tr, 128), F32)] * 3,
        compiler_params=_cp(1), name=name)(a, v1, w3, lg, lb, dv)


def mod_fwd(rows, w, bsl, name):
    Ly, D, Nc = w.shape
    tn = _div_tile(Nc, 512)

    def body(r_ref, w_ref, b_ref, o_ref):
        s = _silu(r_ref[...]).astype(BF16)
        o_ref[...] = jnp.dot(s, w_ref[...].astype(BF16), preferred_element_type=F32) + b_ref[...]

    return pl.pallas_call(
        body, out_shape=_sds((Ly, 16, Nc), F32), grid=(Ly, Nc // tn),
        in_specs=[pl.BlockSpec((16, D), lambda l, j: (0, 0)), pl.BlockSpec((None, D, tn), lambda l, j: (l, 0, j)),
                  pl.BlockSpec((None, 1, tn), lambda l, j: (l, 0, j))],
        out_specs=pl.BlockSpec((None, 16, tn), lambda l, j: (l, 0, j)), compiler_params=_cp(2), name=name)(rows, w, bsl)


def mod_bwd(rows, w, dm, name):
    Ly, D, Nc = w.shape
    tn = _div_tile(Nc, 512)
    nj = Nc // tn

    def body(r_ref, w_ref, dm_ref, dw_ref, ds_ref):
        j = pl.program_id(1)
        s = _silu(r_ref[...]).astype(BF16)
        dmv = dm_ref[...].astype(BF16)
        dw_ref[...] = lax.dot_general(s, dmv, (((0,), (0,)), ((), ())), preferred_element_type=F32)
        _acc(ds_ref, lax.dot_general(dmv, w_ref[...].astype(BF16), (((1,), (1,)), ((), ())), preferred_element_type=F32), j == 0)

    return pl.pallas_call(
        body, out_shape=[_sds((Ly, D, Nc), F32), _sds((Ly, 16, D), F32)], grid=(Ly, nj),
        in_specs=[pl.BlockSpec((16, D), lambda l, j: (0, 0)), pl.BlockSpec((None, D, tn), lambda l, j: (l, 0, j)),
                  pl.BlockSpec((None, 16, tn), lambda l, j: (l, 0, j))],
        out_specs=[pl.BlockSpec((None, D, tn), lambda l, j: (l, 0, j)), pl.BlockSpec((None, 16, D), lambda l, j: (l, 0, 0))],
        compiler_params=_cp(2), name=name)(rows, w, dm)


def silu_grad(dsc, c, name):
    def body(d_ref, c_ref, o_ref):
        x = c_ref[...]
        s = jax.nn.sigmoid(x)
        o_ref[...] = d_ref[...] * (s * (1.0 + x * (1.0 - s)))

    return pl.pallas_call(body, out_shape=_sds(c.shape, F32), name=name)(dsc, c)


def _coords():
    return lax.axis_index("x"), lax.axis_index("y"), lax.axis_index("c")


def _flip(v, bit):
    return 1 - v if bit else v


def allgather8(x, name):
    R, C = x.shape

    def body(x_ref, o_ref, send_sems, recv_sems, local_sem):
        mx, my, mc = _coords()
        me = 4 * mx + 2 * my + mc
        mine = pltpu.make_async_copy(x_ref, o_ref.at[me], local_sem)
        mine.start()
        copies = []
        for k in range(1, 8):
            px, py, pc = _flip(mx, k & 4), _flip(my, k & 2), _flip(mc, k & 1)
            cp = pltpu.make_async_remote_copy(src_ref=x_ref, dst_ref=o_ref.at[me], send_sem=send_sems.at[k - 1],
                                              recv_sem=recv_sems.at[k - 1], device_id=(px, py, pc), device_id_type=MESH)
            cp.start()
            copies.append((cp, 4 * px + 2 * py + pc))
        for k, (cp, peer) in enumerate(copies):
            pltpu.make_async_remote_copy(src_ref=x_ref, dst_ref=o_ref.at[peer], send_sem=send_sems.at[k], recv_sem=recv_sems.at[k],
                                         device_id=(mx, my, mc), device_id_type=MESH).wait_recv()
        for cp, _ in copies:
            cp.wait_send()
        mine.wait()

    return pl.pallas_call(
        body, out_shape=_sds((8, R, C), F32), in_specs=[pl.BlockSpec(memory_space=pltpu.VMEM)],
        out_specs=pl.BlockSpec(memory_space=pltpu.VMEM),
        scratch_shapes=[pltpu.SemaphoreType.DMA((7,)), pltpu.SemaphoreType.DMA((7,)), pltpu.SemaphoreType.DMA],
        compiler_params=pltpu.CompilerParams(vmem_limit_bytes=VMEM_LIMIT), name=name)(x)


def chip_exchange(arrs, name):
    n = len(arrs)

    def src(ref, k):
        return ref.at[k]

    def body(*refs):
        ins, outs = refs[:n], refs[n:2 * n]
        send_sems, recv_sems, local_sems = refs[2 * n:]
        mx, my, mc = _coords()
        me = 2 * mx + my
        started = []
        for a in range(n):
            mine = pltpu.make_async_copy(src(ins[a], me), outs[a].at[me], local_sems.at[a])
            mine.start()
            started.append(mine)
        sends = []
        for a in range(n):
            for k in range(1, 4):
                px, py = _flip(mx, k & 2), _flip(my, k & 1)
                cp = pltpu.make_async_remote_copy(src_ref=src(ins[a], 2 * px + py), dst_ref=outs[a].at[me], send_sem=send_sems.at[3 * a + k - 1],
                                                  recv_sem=recv_sems.at[3 * a + k - 1], device_id=(px, py, mc), device_id_type=MESH)
                cp.start()
                sends.append((cp, a, k, 2 * px + py))
        for cp, a, k, peer in sends:
            pltpu.make_async_remote_copy(src_ref=src(ins[a], me), dst_ref=outs[a].at[peer], send_sem=send_sems.at[3 * a + k - 1],
                                         recv_sem=recv_sems.at[3 * a + k - 1], device_id=(mx, my, mc), device_id_type=MESH).wait_recv()
        for cp, *_ in sends:
            cp.wait_send()
        for mine in started:
            mine.wait()

    hbm = pl.BlockSpec(memory_space=pl.ANY)
    return pl.pallas_call(
        body, out_shape=[_sds(a.shape, a.dtype) for a in arrs], in_specs=[hbm] * n, out_specs=[hbm] * n,
        scratch_shapes=[pltpu.SemaphoreType.DMA((3 * n,)), pltpu.SemaphoreType.DMA((3 * n,)), pltpu.SemaphoreType.DMA((n,))],
        name=name)(*arrs)


def chip_allgather(arrs, name):
    n = len(arrs)

    def body(*refs):
        ins, outs = refs[:n], refs[n:2 * n]
        send_sems, recv_sems, pass_send, pass_recv, local_sems = refs[2 * n:]
        mx, my, mc = _coords()
        me = 2 * mx + my
        started = []
        for a in range(n):
            mine = pltpu.make_async_copy(ins[a], outs[a].at[me], local_sems.at[a])
            mine.start()
            started.append(mine)
        sends = []
        for a in range(n):
            for k in range(1, 4):
                px, py = _flip(mx, k & 2), _flip(my, k & 1)
                cp = pltpu.make_async_remote_copy(src_ref=ins[a].at[mc], dst_ref=outs[a].at[me, mc],
                                                  send_sem=send_sems.at[3 * a + k - 1], recv_sem=recv_sems.at[3 * a + k - 1],
                                                  device_id=(px, py, mc), device_id_type=MESH)
                cp.start()
                sends.append((cp, a, 3 * a + k - 1, 2 * px + py))
        passed = []
        for cp, a, s, peer in sends:
            got = outs[a].at[peer, mc]
            pltpu.make_async_remote_copy(src_ref=got, dst_ref=got, send_sem=send_sems.at[s], recv_sem=recv_sems.at[s],
                                         device_id=(mx, my, mc), device_id_type=MESH).wait_recv()
            fw = pltpu.make_async_remote_copy(src_ref=got, dst_ref=got, send_sem=pass_send.at[s], recv_sem=pass_recv.at[s],
                                              device_id=(mx, my, 1 - mc), device_id_type=MESH)
            fw.start()
            passed.append(fw)
        for cp, a, s, peer in sends:
            theirs = outs[a].at[peer, 1 - mc]
            pltpu.make_async_remote_copy(src_ref=theirs, dst_ref=theirs, send_sem=pass_send.at[s], recv_sem=pass_recv.at[s],
                                         device_id=(mx, my, mc), device_id_type=MESH).wait_recv()
        for cp, *_ in sends:
            cp.wait_send()
        for fw in passed:
            fw.wait_send()
        for mine in started:
            mine.wait()

    hbm = pl.BlockSpec(memory_space=pl.ANY)
    sems = pltpu.SemaphoreType.DMA((3 * n,))
    return pl.pallas_call(body, out_shape=[_sds((4,) + a.shape, a.dtype) for a in arrs], in_specs=[hbm] * n, out_specs=[hbm] * n,
                          scratch_shapes=[sems, sems, sems, sems, pltpu.SemaphoreType.DMA((n,))], name=name)(*arrs)


def sibling_swap_halves(arrs, name):
    n = len(arrs)

    def body(*refs):
        ins, outs = refs[:n], refs[n:2 * n]
        send_sems, recv_sems = refs[2 * n:]
        mx, my, mc = _coords()
        cps = []
        for a in range(n):
            hl = arrs[a].shape[1] // 2
            cp = pltpu.make_async_remote_copy(src_ref=ins[a].at[pl.ds(0, 4), pl.ds((1 - mc) * hl, hl)], dst_ref=outs[a],
                                              send_sem=send_sems.at[a], recv_sem=recv_sems.at[a],
                                              device_id=(mx, my, 1 - mc), device_id_type=MESH)
            cp.start()
            cps.append(cp)
        for cp in cps:
            cp.wait()

    hbm = pl.BlockSpec(memory_space=pl.ANY)
    return pl.pallas_call(body, out_shape=[_sds((4, a.shape[1] // 2) + a.shape[2:], a.dtype) for a in arrs], in_specs=[hbm] * n,
                          out_specs=[hbm] * n, scratch_shapes=[pltpu.SemaphoreType.DMA((n,)), pltpu.SemaphoreType.DMA((n,))], name=name)(*arrs)


def sibling_exchange(arrs, name):
    n = len(arrs)

    def body(*refs):
        ins, outs = refs[:n], refs[n:2 * n]
        send_sems, recv_sems = refs[2 * n:]
        mx, my, mc = _coords()
        cps = []
        for a in range(n):
            cp = pltpu.make_async_remote_copy(src_ref=ins[a], dst_ref=outs[a], send_sem=send_sems.at[a], recv_sem=recv_sems.at[a],
                                              device_id=(mx, my, 1 - mc), device_id_type=MESH)
            cp.start()
            cps.append(cp)
        for cp in cps:
            cp.wait()

    hbm = pl.BlockSpec(memory_space=pl.ANY)
    return pl.pallas_call(body, out_shape=[_sds(a.shape, a.dtype) for a in arrs], in_specs=[hbm] * n, out_specs=[hbm] * n,
                          scratch_shapes=[pltpu.SemaphoreType.DMA((n,)), pltpu.SemaphoreType.DMA((n,))], name=name)(*arrs)


def add_pair(a, b, name, tr=512):
    R, C = a.shape
    tr = _div_tile(R, tr, 8)
    row = pl.BlockSpec((tr, C), lambda i: (i, 0))

    def body(a_ref, b_ref, o_ref):
        o_ref[...] = (a_ref[...].astype(F32) + b_ref[...].astype(F32)).astype(BF16)

    return pl.pallas_call(body, out_shape=_sds((R, C), BF16), grid=(R // tr,), in_specs=[row, row], out_specs=row,
                          compiler_params=_cp(1), name=name)(a, b)


def sum_slabs(x, name, tr=256):
    n, R, C = x.shape
    tr = _div_tile(R, tr, 8)

    def body(x_ref, o_ref):
        acc = x_ref[0].astype(F32)
        for k in range(1, n):
            acc = acc + x_ref[k].astype(F32)
        o_ref[...] = acc

    return pl.pallas_call(body, out_shape=_sds((R, C), F32), grid=(R // tr,), in_specs=[pl.BlockSpec((n, tr, C), lambda i: (0, i, 0))],
                          out_specs=pl.BlockSpec((tr, C), lambda i: (i, 0)), compiler_params=_cp(1), name=name)(x)


def adamw(w, g, m, v, name, tr=256):
    R, C = w.shape
    split = isinstance(g, tuple)
    nh = 2 if split else 1
    tr = _div_tile(R // nh, tr, 8)
    nt = R // nh // tr
    row = pl.BlockSpec((tr, C), lambda h, i: (h * nt + i, 0))
    part = pl.BlockSpec((tr, C), lambda h, i: (i, 0))
    ins = [w] + (list(g) if split else [g]) + [m, v]

    def body(*refs):
        w_ref = refs[0]
        if split:
            mine = pl.program_id(0) == lax.axis_index("c")
            g = jnp.where(mine, refs[1][...], refs[2][...])
        else:
            g = refs[1][...]
        m_ref, v_ref, go_ref, d_ref, mo_ref, vo_ref = refs[nh + 1:]
        mn = ADAM_B1 * m_ref[...] + (1.0 - ADAM_B1) * g
        vn = ADAM_B2 * v_ref[...] + (1.0 - ADAM_B2) * (g * g)
        m_hat = mn / (1.0 - ADAM_B1 ** ADAM_STEP)
        v_hat = vn / (1.0 - ADAM_B2 ** ADAM_STEP)
        go_ref[...] = g
        d_ref[...] = -ADAM_LR * (m_hat / (jnp.sqrt(v_hat) + ADAM_EPS) + ADAM_WD * w_ref[...])
        mo_ref[...] = mn
        vo_ref[...] = vn

    return pl.pallas_call(body, out_shape=[_sds((R, C), F32)] * 4, grid=(nh, nt), in_specs=[row] + [part] * nh + [row, row],
                          out_specs=[row] * 4, compiler_params=_cp(2), name=name)(*ins)


W_NAMES = ("c_ctx", "mod_w", "mod_b", "pre_mix_g", "post_mix_g", "pre_mlp_g", "post_mlp_g", "mlp_w1", "mlp_w2", "ssm_in_w",
           "ssm_conv_w", "ssm_conv_b", "ssm_a_log_f", "ssm_dt_bias_f", "ssm_d_f", "ssm_a_log_b", "ssm_dt_bias_b", "ssm_d_b",
           "ssm_norm_g", "ssm_out_w", "conf_pw1_w", "conf_pw1_b", "conf_dw_w", "conf_dw_b", "conf_ln_g", "conf_ln_b",
           "conf_pw2_w", "conf_pw2_b")
BIG = {"mlp_w1": "col", "mlp_w2": "row", "ssm_in_w": "col", "ssm_out_w": "row", "conf_pw1_w": "col", "conf_pw2_w": "row"}
SMALL_SHARDED = ("ssm_conv_w", "conf_pw1_b", "conf_dw_w", "conf_dw_b", "conf_ln_g", "conf_ln_b", "conf_pw2_b")
PACK_W = 1024


def _pack(arrs):
    flat = jnp.concatenate([a.reshape(-1).astype(F32) for a in arrs])
    n = flat.shape[0]
    tot = -(-n // (8 * PACK_W)) * (8 * PACK_W)
    return jnp.pad(flat, (0, tot - n)).reshape(tot // PACK_W, PACK_W)


def _unpack(buf, shapes):
    lead = buf.shape[:-2]
    flat = buf.reshape(lead + (-1,))
    out, off = [], 0
    for shp in shapes:
        n = 1
        for d in shp:
            n *= d
        out.append(flat[..., off:off + n].reshape(lead + tuple(shp)))
        off += n
    return out


def _full_from_chips(g, kind):
    if kind == "col":
        return jnp.moveaxis(g, 0, -2).reshape(g.shape[1:-1] + (4 * g.shape[-1],))
    return jnp.moveaxis(g, 0, 1).reshape((g.shape[1], 4 * g.shape[2]) + g.shape[3:])


def _chip_slabs(full, kind):
    if kind == "col":
        return jnp.moveaxis(full.reshape(full.shape[:-1] + (4, full.shape[-1] // 4)), -2, 0)
    return jnp.moveaxis(full.reshape((full.shape[0], 4, full.shape[1] // 4) + full.shape[2:]), 1, 0)


def _view2d(a):
    if a.ndim == 1:
        return a.reshape(1, -1)
    return a.reshape(-1, a.shape[-1])


def kernel(x, c, ctx, c_ctx, mod_w, mod_b, pre_mix_g, post_mix_g, pre_mlp_g, post_mlp_g, mlp_w1, mlp_w2, ssm_in_w, ssm_conv_w, ssm_conv_b, ssm_a_log_f, ssm_dt_bias_f, ssm_d_f, ssm_a_log_b, ssm_dt_bias_b, ssm_d_b, ssm_norm_g, ssm_out_w, conf_pw1_w, conf_pw1_b, conf_dw_w, conf_dw_b, conf_ln_g, conf_ln_b, conf_pw2_w, conf_pw2_b, loss_target, m_c_ctx, m_mod_w, m_mod_b, m_pre_mix_g, m_post_mix_g, m_pre_mlp_g, m_post_mlp_g, m_mlp_w1, m_mlp_w2, m_ssm_in_w, m_ssm_conv_w, m_ssm_conv_b, m_ssm_a_log_f, m_ssm_dt_bias_f, m_ssm_d_f, m_ssm_a_log_b, m_ssm_dt_bias_b, m_ssm_d_b, m_ssm_norm_g, m_ssm_out_w, m_conf_pw1_w, m_conf_pw1_b, m_conf_dw_w, m_conf_dw_b, m_conf_ln_g, m_conf_ln_b, m_conf_pw2_w, m_conf_pw2_b, v_c_ctx, v_mod_w, v_mod_b, v_pre_mix_g, v_post_mix_g, v_pre_mlp_g, v_post_mlp_g, v_mlp_w1, v_mlp_w2, v_ssm_in_w, v_ssm_conv_w, v_ssm_conv_b, v_ssm_a_log_f, v_ssm_dt_bias_f, v_ssm_d_f, v_ssm_a_log_b, v_ssm_dt_bias_b, v_ssm_d_b, v_ssm_norm_g, v_ssm_out_w, v_conf_pw1_w, v_conf_pw1_b, v_conf_dw_w, v_conf_dw_b, v_conf_ln_g, v_conf_ln_b, v_conf_pw2_w, v_conf_pw2_b):
    given = dict(locals())
    W = {n: given[n] for n in W_NAMES}
    L, D = x.shape[1], x.shape[2]
    Lc = ctx.shape[1]
    T = Lc + L
    depth = mod_w.shape[0]
    d_inner = ssm_norm_g.shape[1]
    H = ssm_a_log_f.shape[1]
    xbc = ssm_conv_b.shape[1]
    GN = (xbc - d_inner) // 2
    G = GN // N_STATE
    rows_grid = L // GRID_W
    tr = Lc
    ncc = Lc // CHUNK
    assert H == 8 * G and Lc % CHUNK == 0 and L % Lc == 0 and tr % GRID_W == 0 and tr % rows_grid == 0
    tc = _div_tile(GN, 512)
    assert d_inner % tc == 0
    mx, my, mc = _coords()
    chip = 2 * mx + my
    dev = 4 * mx + 2 * my + mc

    small_shapes = [(1, D)] + [W[n].shape for n in SMALL_SHARDED]
    got = allgather8(_pack([c] + [W[n] for n in SMALL_SHARDED]), "gather_small")
    parts = _unpack(got, small_shapes)
    c_all = parts[0].reshape(8, D)
    full_small = {n: jnp.concatenate([p[2 * k] for k in range(4)], axis=-1) for n, p in zip(SMALL_SHARDED, parts[1:])}

    cond = jnp.concatenate([c_all, c_ctx.reshape(1, D), jnp.zeros((7, D), F32)], axis=0)
    ncol = mod_w.shape[2]
    bsl = lax.dynamic_slice(mod_b, (0, chip * ncol), (depth, ncol)).reshape(depth, 1, ncol)
    m_loc = mod_fwd(cond, mod_w, bsl, "mod_fwd")
    m_all = allgather8(m_loc.reshape(depth * 16, ncol), "gather_mod").reshape(8, depth, 16, ncol)
    m_full = jnp.concatenate([m_all[2 * k] for k in range(4)], axis=-1)
    m_lat = lax.dynamic_slice(m_full, (0, dev, 0), (depth, 1, 6 * D))
    m2 = jnp.concatenate([m_full[:, 8:9], m_lat], axis=1)

    def six(i):
        return [m2[i, :, k * D:(k + 1) * D].reshape(2, 1, D) for k in range(6)]

    big_names = list(BIG)
    gathered = chip_allgather([W[n].astype(BF16).reshape((2, W[n].shape[0] // 2) + W[n].shape[1:]) for n in big_names], "gather_weights")
    Wb = {}
    for n, g in zip(big_names, gathered):
        g = g.reshape((4,) + W[n].shape)
        Wb[n] = [_full_from_chips(g[:, j:j + 1], BIG[n])[0] for j in range(W[n].shape[0])]

    def to_scan(u):
        lat = u[Lc:].reshape(rows_grid, GRID_W, u.shape[1]).swapaxes(0, 1).reshape(L, u.shape[1])
        return jnp.concatenate([u[:Lc], lat], axis=0)

    def from_scan(u):
        lat = u[Lc:].reshape(GRID_W, rows_grid, u.shape[1]).swapaxes(0, 1).reshape(L, u.shape[1])
        return jnp.concatenate([u[:Lc], lat], axis=0)

    def ssm_params(j):
        def two(f, b):
            return jnp.stack([f[j], b[j]]).reshape(2, G, 1, 8)
        return two(ssm_dt_bias_f, ssm_dt_bias_b), two(ssm_a_log_f, ssm_a_log_b), two(ssm_d_f, ssm_d_b)

    def dw3_of(j):
        w = full_small["conf_dw_w"][j]
        return w.reshape(w.shape[0], D // 128, 128).swapaxes(0, 1)

    h = jnp.concatenate([ctx[0], x[0]], axis=0)
    saved = []
    for i in range(depth):
        kind, j = i % 2, i // 2
        col_major = (j % 2) == 1
        sh1, sc1, g1, sh2, sc2, g2 = six(i)
        s = {"h": h}
        u = prenorm_fwd(h, pre_mix_g[i][None], sh1, sc1, tr, f"prenorm_mix{i}")
        if col_major:
            u = to_scan(u)
        s["u"] = u
        if kind == 0:
            proj = mm(u, Wb["ssm_in_w"][j], "nn", F32, name=f"ssm_in{i}")
            pre, act = conv5_fwd(proj, full_small["ssm_conv_w"][j], ssm_conv_b[j][None], d_inner // tc, tr, tc, f"ssm_conv{i}")
            dtr = proj[:, d_inner + xbc:].reshape(T, 2, G, 8).transpose(1, 2, 0, 3)
            bias, alog, dsk = ssm_params(j)
            y2, hsave = ssd_fwd(act, dtr, bias, alog, dsk, d_inner, ncc, f"ssd_fwd{i}")
            yn = gnorm_fwd(y2, proj, ssm_norm_g[j][None], CHUNK, f"ssm_gnorm{i}")
            out = mm(yn, Wb["ssm_out_w"][j], "nn", F32, name=f"ssm_out{i}")
            s.update(proj=proj, pre=pre, act=act, dtr=dtr, y2=y2, hsave=hsave, yn=yn)
        else:
            seg = rows_grid if col_major else GRID_W
            a = mm(u, Wb["conf_pw1_w"][j], "nn", F32, bias=full_small["conf_pw1_b"][j][None], name=f"conf_pw1_{i}")
            v, v1 = confmid_fwd(a, dw3_of(j), full_small["conf_dw_b"][j][None], full_small["conf_ln_g"][j][None],
                                full_small["conf_ln_b"][j][None], seg, tr, f"conf_mid{i}")
            out = mm(v, Wb["conf_pw2_w"][j], "nn", F32, bias=full_small["conf_pw2_b"][j][None], name=f"conf_pw2_{i}")
            s.update(a=a, v=v, v1=v1, seg=seg)
        if col_major:
            out = from_scan(out)
        h1 = post_fwd(h, out, post_mix_g[i][None], g1, tr, f"post_mix{i}")
        u2 = prenorm_fwd(h1, pre_mlp_g[i][None], sh2, sc2, tr, f"prenorm_mlp{i}")
        hid, actm = mm(u2, Wb["mlp_w1"][i], "nn", F32, relu2=True, name=f"mlp_up{i}")
        f = mm(actm, Wb["mlp_w2"][i], "nn", F32, name=f"mlp_down{i}")
        h = post_fwd(h1, f, post_mlp_g[i][None], g2, tr, f"post_mlp{i}")
        s.update(out=out, h1=h1, u2=u2, hid=hid, actm=actm, f=f)
        saved.append(s)

    loss_blk, Gr = loss_head(h, loss_target[0], tr, "loss_head")
    loss = lax.psum(loss_blk[0, 0], ("x", "y", "c"))

    gb = {n: [None] * W[n].shape[0] for n in BIG}
    gs = {n: [None] * W[n].shape[0] for n in W_NAMES if n not in BIG and n not in ("c_ctx", "mod_w", "mod_b")}
    dmod = [None] * depth
    for i in reversed(range(depth)):
        kind, j = i % 2, i // 2
        col_major = (j % 2) == 1
        sh1, sc1, g1, sh2, sc2, g2 = six(i)
        s = saved[i]
        df, gs["post_mlp_g"][i], dg2, _ = post_bwd(s["f"], post_mlp_g[i][None], g2, Gr, tr, f"post_mlp_bwd{i}")
        gb["mlp_w2"][i] = mm(s["actm"], df, "tn", BF16, name=f"mlp_down_wg{i}")
        dhid = mm(df, Wb["mlp_w2"][i], "nt", BF16, mul_relu=s["hid"], name=f"mlp_down_dg{i}")
        gb["mlp_w1"][i] = mm(s["u2"], dhid, "tn", BF16, name=f"mlp_up_wg{i}")
        du2 = mm(dhid, Wb["mlp_w1"][i], "nt", F32, name=f"mlp_up_dg{i}")
        Gr, gs["pre_mlp_g"][i], dsh2, dsc2 = prenorm_bwd(s["h1"], pre_mlp_g[i][None], sh2, sc2, du2, Gr, tr, f"prenorm_mlp_bwd{i}")
        dout, gs["post_mix_g"][i], dg1, dout_sum = post_bwd(s["out"], post_mix_g[i][None], g1, Gr, tr, f"post_mix_bwd{i}")
        if col_major:
            dout = to_scan(dout)
        if kind == 0:
            gb["ssm_out_w"][j] = mm(s["yn"], dout, "tn", BF16, name=f"ssm_out_wg{i}")
            dyn = mm(dout, Wb["ssm_out_w"][j], "nt", F32, name=f"ssm_out_dg{i}")
            dys, dz, gs["ssm_norm_g"][j] = gnorm_bwd(s["y2"], s["proj"], ssm_norm_g[j][None], dyn, CHUNK, f"ssm_gnorm_bwd{i}")
            bias, alog, dsk = ssm_params(j)
            dx2, db2, dc2, ddtr, dbias, dalog, ddsk = ssd_bwd(s["act"], s["dtr"], bias, alog, dsk, s["hsave"], dys, d_inner, ncc, f"ssd_bwd{i}")
            cw = full_small["ssm_conv_w"][j]
            nx, nb_ = d_inner // tc, GN // tc
            dxx, dwx, dbx = conv5_bwd(dx2, s["pre"], s["proj"], cw, 0, nx, tr, tc, f"ssm_conv_bwd_x{i}")
            dxb, dwb, dbb = conv5_bwd(db2, s["pre"], s["proj"], cw, nx, 2 * nx, tr, tc, f"ssm_conv_bwd_b{i}")
            dxc, dwc, dbc = conv5_bwd(dc2, s["pre"], s["proj"], cw, nx + nb_, 2 * nx + nb_, tr, tc, f"ssm_conv_bwd_c{i}")
            gs["ssm_conv_w"][j] = jnp.concatenate([dwx, dwb, dwc], axis=1)
            gs["ssm_conv_b"][j] = jnp.concatenate([dbx, dbb, dbc], axis=1)[0]
            ddt = ddtr.transpose(2, 0, 1, 3).reshape(T, 2 * H).astype(BF16)
            dproj = jnp.concatenate([dz, dxx, dxb, dxc, ddt], axis=1)
            gb["ssm_in_w"][j] = mm(s["u"], dproj, "tn", BF16, name=f"ssm_in_wg{i}")
            du = mm(dproj, Wb["ssm_in_w"][j], "nt", F32, name=f"ssm_in_dg{i}")
            for nm, val in (("ssm_dt_bias", dbias), ("ssm_a_log", dalog), ("ssm_d", ddsk)):
                gs[nm + "_f"][j] = val[0].reshape(H)
                gs[nm + "_b"][j] = val[1].reshape(H)
        else:
            gb["conf_pw2_w"][j] = mm(s["v"], dout, "tn", BF16, name=f"conf_pw2_wg{i}")
            gs["conf_pw2_b"][j] = dout_sum[0]
            dv = mm(dout, Wb["conf_pw2_w"][j], "nt", F32, name=f"conf_pw2_dg{i}")
            da, da_sum, dw3, ddb, dlg, dlb = confmid_bwd(s["a"], s["v1"], dw3_of(j), full_small["conf_ln_g"][j][None],
                                                          full_small["conf_ln_b"][j][None], dv, s["seg"], tr, f"conf_mid_bwd{i}")
            gs["conf_pw1_b"][j] = da_sum[0]
            gs["conf_dw_w"][j] = dw3.swapaxes(0, 1).reshape(dw3.shape[1], D)
            gs["conf_dw_b"][j], gs["conf_ln_g"][j], gs["conf_ln_b"][j] = ddb[0], dlg[0], dlb[0]
            gb["conf_pw1_w"][j] = mm(s["u"], da, "tn", BF16, name=f"conf_pw1_wg{i}")
            du = mm(da, Wb["conf_pw1_w"][j], "nt", F32, name=f"conf_pw1_dg{i}")
        if col_major:
            du = from_scan(du)
        Gr, gs["pre_mix_g"][i], dsh1, dsc1 = prenorm_bwd(s["h"], pre_mix_g[i][None], sh1, sc1, du, Gr, tr, f"prenorm_mix_bwd{i}")
        dmod[i] = jnp.concatenate([t.reshape(2, D) for t in (dsh1, dsc1, dg1, dsh2, dsc2, dg2)], axis=1)
    grad_x = Gr[Lc:][None]

    small_names = list(gs)
    small_local = [jnp.stack([t.reshape(W[n].shape[1:] if n not in SMALL_SHARDED else t.shape) for t in gs[n]]) for n in small_names]
    small_shapes = [t.shape for t in small_local] + [(depth, 2, 6 * D)]
    got = allgather8(_pack(small_local + [jnp.stack(dmod)]), "gather_small_grads")
    summed = _unpack(sum_slabs(got, "sum_small_grads"), small_shapes)
    grads = {}
    for n, t in zip(small_names, summed[:-1]):
        if n in SMALL_SHARDED:
            w = W[n].shape[-1]
            t = lax.dynamic_slice_in_dim(t, chip * w, w, axis=t.ndim - 1)
        grads[n] = t
    grads["mod_b"] = summed[-1][:, 0] + summed[-1][:, 1]
    dm_all = _unpack(got, small_shapes)[-1]
    dm_ctx = sum_slabs(dm_all[:, :, 0], "sum_dmod_ctx")
    dm_rows = jnp.concatenate([dm_all[:, :, 1].swapaxes(0, 1), dm_ctx[:, None], jnp.zeros((depth, 7, 6 * D), F32)], axis=1)
    dm_mine = lax.dynamic_slice_in_dim(dm_rows, chip * ncol, ncol, axis=2)
    grads["mod_w"], dcond = mod_bwd(cond, mod_w, dm_mine, "mod_bwd")
    dcc = sum_slabs(dcond[:, 8:9], "sum_dcond_layers")
    dcc_all = allgather8(jnp.pad(dcc, ((0, 7), (0, 0))), "gather_dcond")
    dcc_sum = sum_slabs(dcc_all[0::2, 0:1], "sum_dcond_chips")
    grads["c_ctx"] = silu_grad(dcc_sum, c_ctx.reshape(1, D), "c_ctx_grad").reshape(D)

    slabs = [_chip_slabs(jnp.stack(gb[n]), BIG[n]) for n in big_names]
    theirs = sibling_swap_halves(slabs, "sibling_swap_grads")
    chip_part = []
    for n, s, t in zip(big_names, slabs, theirs):
        hl = t.shape[1]
        own = lax.dynamic_slice_in_dim(s, mc * hl, hl, axis=1)
        chip_part.append(add_pair(_view2d(own), _view2d(t), f"add_cores_{n}").reshape(t.shape))
    recv = chip_exchange(chip_part, "scatter_grads")
    half = [sum_slabs(r.reshape((4, -1, r.shape[-1])), f"sum_grads_{n}") for n, r in zip(big_names, recv)]
    for n, mine, theirs in zip(big_names, half, sibling_exchange(half, "sibling_grads")):
        grads[n] = (mine, theirs)

    res = {}
    for n in W_NAMES:
        w2 = _view2d(W[n])
        cols = w2.shape[1]
        g = grads[n] if isinstance(grads[n], tuple) else _view2d(grads[n])
        outs = adamw(w2, g, _view2d(given["m_" + n]), _view2d(given["v_" + n]), f"adamw_{n}", tr=max(8, (262144 // cols) // 8 * 8))
        res[n] = [o.reshape(W[n].shape) for o in outs]
    return (loss, grad_x, *[res[n][0] for n in W_NAMES], *[res[n][1] for n in W_NAMES], *[res[n][2] for n in W_NAMES],
            *[res[n][3] for n in W_NAMES])
```

```python
import functools

import jax
import jax.numpy as jnp
from jax import lax
from jax.experimental import pallas as pl
from jax.experimental.pallas import tpu as pltpu

GRID_W = 64
CHUNK = 128
N_STATE = 128
EPS = 1e-6
ADAM_LR, ADAM_B1, ADAM_B2, ADAM_EPS, ADAM_WD, ADAM_STEP = 0.001, 0.9, 0.999, 1e-08, 0.01, 10
VMEM_LIMIT = 56 * 1024 * 1024
MM_OPERAND_BYTES = 44 * 1024 * 1024
F32, BF16 = jnp.float32, jnp.bfloat16
HI = lax.Precision.HIGHEST
SPREAD = lax.Precision.HIGH
MESH = pl.DeviceIdType.MESH


def _cp(n_grid):
    return pltpu.CompilerParams(dimension_semantics=("arbitrary",) * n_grid, vmem_limit_bytes=VMEM_LIMIT)


def _sds(shape, dtype):
    return jax.ShapeDtypeStruct(tuple(shape), dtype)


def _div_tile(n, target, unit=128):
    best = None
    t = unit
    while t <= min(n, target):
        if n % t == 0:
            best = t
        t += unit
    return best if best is not None else n


def _rms(x, g):
    return x * lax.rsqrt(jnp.mean(x * x, axis=-1, keepdims=True) + EPS) * g


def _silu(x):
    return x * jax.nn.sigmoid(x)


def mm(a, b, mode, out_dtype, *, bias=None, relu2=False, mul_relu=None, name, tm=768, tn=1152, tk=2048):
    if mode == "nn":
        (M, C), (_, N) = a.shape, b.shape
    elif mode == "nt":
        (M, C), (N, _) = a.shape, b.shape
    else:
        (C, M), (_, N) = a.shape, b.shape
    if mode == "tn":
        tm, tn, tk = _div_tile(M, 1024), _div_tile(N, tn), _div_tile(C, 4224, 8)
    else:
        tm = _div_tile(M, tm, 8)
        if C > tk and C % 128 == 0 and 4 * C * (tm + 512) <= MM_OPERAND_BYTES:
            tn, tk = _div_tile(N, 512), C
        else:
            tn, tk = _div_tile(N, tn), _div_tile(C, 2 * tk if C > tk else tk)
    nk = C // tk
    a_spec = {"nn": pl.BlockSpec((tm, tk), lambda i, j, k: (i, k)), "nt": pl.BlockSpec((tm, tk), lambda i, j, k: (i, k)),
              "tn": pl.BlockSpec((tk, tm), lambda i, j, k: (k, i))}[mode]
    b_spec = {"nn": pl.BlockSpec((tk, tn), lambda i, j, k: (k, j)), "nt": pl.BlockSpec((tn, tk), lambda i, j, k: (j, k)),
              "tn": pl.BlockSpec((tk, tn), lambda i, j, k: (k, j))}[mode]
    dims = {"nn": (((1,), (0,)), ((), ())), "nt": (((1,), (1,)), ((), ())), "tn": (((0,), (0,)), ((), ()))}[mode]
    ins, specs = [a, b], [a_spec, b_spec]
    if bias is not None:
        ins.append(bias)
        specs.append(pl.BlockSpec((1, tn), lambda i, j, k: (0, j)))
    if mul_relu is not None:
        ins.append(mul_relu)
        specs.append(pl.BlockSpec((tm, tn), lambda i, j, k: (i, j)))
    o_spec = pl.BlockSpec((tm, tn), lambda i, j, k: (i, j))
    outs, out_specs = [_sds((M, N), out_dtype)], [o_spec]
    if relu2:
        outs.append(_sds((M, N), BF16))
        out_specs.append(o_spec)

    def body(*refs):
        a_ref, b_ref = refs[0], refs[1]
        pos = 2
        bias_ref = mr_ref = None
        if bias is not None:
            bias_ref = refs[pos]
            pos += 1
        if mul_relu is not None:
            mr_ref = refs[pos]
            pos += 1
        o_ref = refs[pos]
        o2_ref = refs[pos + 1] if relu2 else None
        part = lax.dot_general(a_ref[...].astype(BF16), b_ref[...].astype(BF16), dims, preferred_element_type=F32)

        def finish(r):
            if bias_ref is not None:
                r = r + bias_ref[...]
            if mr_ref is not None:
                r = r * (2.0 * jnp.maximum(mr_ref[...], 0.0))
            o_ref[...] = r.astype(o_ref.dtype)
            if o2_ref is not None:
                q = jnp.maximum(r, 0.0)
                o2_ref[...] = (q * q).astype(BF16)

        if nk == 1:
            finish(part)
        else:
            acc_ref = refs[-1]
            k = pl.program_id(2)

            @pl.when(k == 0)
            def _():
                acc_ref[...] = part

            @pl.when(jnp.logical_and(k > 0, k < nk - 1))
            def _():
                acc_ref[...] += part

            @pl.when(k == nk - 1)
            def _():
                finish(acc_ref[...] + part)

    res = pl.pallas_call(body, out_shape=outs, grid=(M // tm, N // tn, nk), in_specs=specs, out_specs=out_specs,
                         scratch_shapes=[pltpu.VMEM((tm, tn), F32)] if nk > 1 else [], compiler_params=_cp(3), name=name)(*ins)
    return res if relu2 else res[0]


def _seg_spec(D):
    return pl.BlockSpec((None, 1, D), lambda i: (jnp.minimum(i, 1), 0, 0))


def _prenorm_fn(h, g, sh, sc):
    return _rms(h, g) * (1.0 + sc) + sh


def prenorm_fwd(h, g, sh, sc, tr, name):
    T, D = h.shape
    row = pl.BlockSpec((tr, D), lambda i: (i, 0))
    vec = pl.BlockSpec((1, D), lambda i: (0, 0))

    def body(h_ref, g_ref, sh_ref, sc_ref, u_ref):
        u_ref[...] = _prenorm_fn(h_ref[...], g_ref[...], sh_ref[...], sc_ref[...]).astype(BF16)

    return pl.pallas_call(body, out_shape=_sds((T, D), BF16), grid=(T // tr,), in_specs=[row, vec, _seg_spec(D), _seg_spec(D)],
                          out_specs=row, compiler_params=_cp(1), name=name)(h, g, sh, sc)


def _acc(ref, val, first):
    @pl.when(first)
    def _():
        ref[...] = val

    @pl.when(jnp.logical_not(first))
    def _():
        ref[...] += val


def prenorm_bwd(h, g, sh, sc, du, G, tr, name):
    T, D = h.shape
    row = pl.BlockSpec((tr, D), lambda i: (i, 0))
    vec = pl.BlockSpec((1, D), lambda i: (0, 0))

    def body(h_ref, g_ref, sh_ref, sc_ref, du_ref, G_ref, Go_ref, dg_ref, dsh_ref, dsc_ref):
        i = pl.program_id(0)
        _, vjp = jax.vjp(_prenorm_fn, h_ref[...], g_ref[...], sh_ref[...], sc_ref[...])
        dh, dg, dsh, dsc = vjp(du_ref[...].astype(F32))
        Go_ref[...] = G_ref[...] + dh
        _acc(dg_ref, dg, i == 0)
        _acc(dsh_ref, dsh, i <= 1)
        _acc(dsc_ref, dsc, i <= 1)

    return pl.pallas_call(
        body, out_shape=[_sds((T, D), F32), _sds((1, D), F32), _sds((2, 1, D), F32), _sds((2, 1, D), F32)], grid=(T // tr,),
        in_specs=[row, vec, _seg_spec(D), _seg_spec(D), row, row], out_specs=[row, vec, _seg_spec(D), _seg_spec(D)],
        compiler_params=_cp(1), name=name)(h, g, sh, sc, du, G)


def _post_fn(y, gp, gate):
    return gate * _rms(y, gp)


def post_fwd(h, y, gp, gate, tr, name):
    T, D = h.shape
    row = pl.BlockSpec((tr, D), lambda i: (i, 0))
    vec = pl.BlockSpec((1, D), lambda i: (0, 0))

    def body(h_ref, y_ref, gp_ref, gate_ref, o_ref):
        o_ref[...] = h_ref[...] + _post_fn(y_ref[...], gp_ref[...], gate_ref[...])

    return pl.pallas_call(body, out_shape=_sds((T, D), F32), grid=(T // tr,), in_specs=[row, row, vec, _seg_spec(D)],
                          out_specs=row, compiler_params=_cp(1), name=name)(h, y, gp, gate)


def post_bwd(y, gp, gate, G, tr, name):
    T, D = y.shape
    row = pl.BlockSpec((tr, D), lambda i: (i, 0))
    vec = pl.BlockSpec((1, D), lambda i: (0, 0))

    def body(y_ref, gp_ref, gate_ref, G_ref, dy_ref, dgp_ref, dgate_ref, dsum_ref):
        i = pl.program_id(0)
        _, vjp = jax.vjp(_post_fn, y_ref[...], gp_ref[...], gate_ref[...])
        dy, dgp, dgate = vjp(G_ref[...])
        dy_ref[...] = dy.astype(BF16)
        _acc(dgp_ref, dgp, i == 0)
        _acc(dgate_ref, dgate, i <= 1)
        _acc(dsum_ref, jnp.sum(dy, axis=0, keepdims=True), i == 0)

    return pl.pallas_call(
        body, out_shape=[_sds((T, D), BF16), _sds((1, D), F32), _sds((2, 1, D), F32), _sds((1, D), F32)], grid=(T // tr,),
        in_specs=[row, vec, _seg_spec(D), row], out_specs=[row, vec, _seg_spec(D), vec], compiler_params=_cp(1), name=name)(y, gp, gate, G)


def loss_head(h, target, tr, name):
    T, D = h.shape
    row = pl.BlockSpec((tr, D), lambda i: (i, 0))
    trow = pl.BlockSpec((tr, D), lambda i: (jnp.maximum(i - 1, 0), 0))

    def body(h_ref, t_ref, loss_ref, G_ref):
        i = pl.program_id(0)

        @pl.when(i == 0)
        def _():
            loss_ref[...] = jnp.zeros_like(loss_ref)
            G_ref[...] = jnp.zeros_like(G_ref)

        @pl.when(i > 0)
        def _():
            e = h_ref[...] - t_ref[...]
            G_ref[...] = e * (1.0 / D)
            loss_ref[...] += jnp.sum(e * e) * (0.5 / D)

    return pl.pallas_call(body, out_shape=[_sds((8, 128), F32), _sds((T, D), F32)], grid=(T // tr,), in_specs=[row, trow],
                          out_specs=[pl.BlockSpec((8, 128), lambda i: (0, 0)), row], compiler_params=_cp(1), name=name)(h, target)


def _halo_specs(tr, tc, T, col0, lead=()):
    n8 = tr // 8
    nl = len(lead)
    cur = pl.BlockSpec(lead + (tr, tc), lambda j, i: (0,) * nl + (i, col0 + j))
    prev = pl.BlockSpec(lead + (8, tc), lambda j, i: (0,) * nl + (jnp.maximum(i * n8 - 1, 0), col0 + j))
    nxt = pl.BlockSpec(lead + (8, tc), lambda j, i: (0,) * nl + (jnp.minimum((i + 1) * n8, T // 8 - 1), col0 + j))
    return [cur, prev, nxt]


def _with_halo(cur, prev, nxt, i, nt):
    keep_prev = (i >= 2).astype(cur.dtype)
    keep_next = jnp.logical_and(i >= 1, i < nt - 1).astype(cur.dtype)
    return jnp.concatenate([prev * keep_prev, cur, nxt * keep_next], axis=0)


def _shift_rows(ext, o, tr):
    n = ext.shape[0]
    return pltpu.roll(ext, (-o) % n, 0)[8:8 + tr]


def conv5_fwd(proj, w, b, col0, tr, tc, name):
    T = proj.shape[0]
    K, C = w.shape
    nt = T // tr

    def body(x_ref, xp_ref, xn_ref, w_ref, b_ref, pre_ref, act_ref):
        i = pl.program_id(1)
        ext = _with_halo(x_ref[...], xp_ref[...], xn_ref[...], i, nt)
        wv = w_ref[...]
        acc = jnp.zeros((tr, tc), F32) + b_ref[...]
        for k in range(K):
            acc = acc + wv[k:k + 1, :] * _shift_rows(ext, k - K // 2, tr)
        pre_ref[...] = acc
        act_ref[...] = _silu(acc)

    out = pl.BlockSpec((tr, tc), lambda j, i: (i, j))
    return pl.pallas_call(
        body, out_shape=[_sds((T, C), F32), _sds((T, C), F32)], grid=(C // tc, nt),
        in_specs=_halo_specs(tr, tc, T, col0) + [pl.BlockSpec((K, tc), lambda j, i: (0, j)), pl.BlockSpec((1, tc), lambda j, i: (0, j))],
        out_specs=[out, out], compiler_params=_cp(2), name=name)(proj, proj, proj, w, b)


def conv5_bwd(dact, pre, proj, w, colp, colx, tr, tc, name):
    _, T, Cp = dact.shape
    K = w.shape[0]
    nt = T // tr

    def body(d_ref, dp_ref, dn_ref, p_ref, pp_ref, pn_ref, x_ref, xp_ref, xn_ref, w_ref, dx_ref, dw_ref, db_ref):
        i = pl.program_id(1)

        def dpre_of(d, p):
            s = jax.nn.sigmoid(p)
            return (d[0] + d[1]) * (s * (1.0 + p * (1.0 - s)))

        dext = _with_halo(dpre_of(d_ref[...], p_ref[...]), dpre_of(dp_ref[...], pp_ref[...]), dpre_of(dn_ref[...], pn_ref[...]), i, nt)
        xext = _with_halo(x_ref[...], xp_ref[...], xn_ref[...], i, nt)
        dcur = dext[8:8 + tr]
        wv = w_ref[...]
        dx = jnp.zeros((tr, tc), F32)
        for k in range(K):
            o = k - K // 2
            dx = dx + wv[k:k + 1, :] * _shift_rows(dext, -o, tr)
            _acc(dw_ref.at[k:k + 1, :], jnp.sum(dcur * _shift_rows(xext, o, tr), axis=0, keepdims=True), i == 0)
        dx_ref[...] = dx.astype(BF16)
        _acc(db_ref, jnp.sum(dcur, axis=0, keepdims=True), i == 0)

    out = pl.BlockSpec((tr, tc), lambda j, i: (i, j))
    return pl.pallas_call(
        body, out_shape=[_sds((T, Cp), BF16), _sds((K, Cp), F32), _sds((1, Cp), F32)], grid=(Cp // tc, nt),
        in_specs=_halo_specs(tr, tc, T, 0, lead=(2,)) + _halo_specs(tr, tc, T, colp) + _halo_specs(tr, tc, T, colx)
        + [pl.BlockSpec((K, tc), lambda j, i: (0, colp + j))],
        out_specs=[out, pl.BlockSpec((K, tc), lambda j, i: (0, j)), pl.BlockSpec((1, tc), lambda j, i: (0, j))],
        compiler_params=_cp(2), name=name)(dact, dact, dact, pre, pre, pre, proj, proj, proj, w)


def _ssd_chunk(xg, bg, cg, dtr, hin, bias, alog, dsk, rev):
    Q, P8 = xg.shape
    nh = dtr.shape[1]
    P = P8 // nh
    N = bg.shape[1]
    dt = jax.nn.softplus(dtr + bias)
    da = dt * (-jnp.exp(alog))
    r_i = lax.broadcasted_iota(jnp.int32, (Q, Q), 0)
    c_i = lax.broadcasted_iota(jnp.int32, (Q, Q), 1)
    mask = jnp.where(rev, c_i - r_i, r_i - c_i) >= 0
    cs = jnp.dot(mask.astype(F32), da, precision=HI, preferred_element_type=F32)
    cs_t = cs.T
    expand = (lax.broadcasted_iota(jnp.int32, (nh, P8), 0) == lax.broadcasted_iota(jnp.int32, (nh, P8), 1) // P).astype(F32)
    blk = (lax.broadcasted_iota(jnp.int32, (nh, nh * Q), 0) == lax.broadcasted_iota(jnp.int32, (nh, nh * Q), 1) // Q).astype(F32)

    def over_lanes(v):
        return jnp.dot(v, expand, precision=SPREAD, preferred_element_type=F32)

    tot = jnp.where(rev, cs[0:1, :], cs[Q - 1:Q, :])
    dt_x, cs_x, tot_x, dsk_x = over_lanes(dt), over_lanes(cs), over_lanes(tot), over_lanes(dsk)
    lhs = jnp.concatenate([cs, jnp.ones((Q, nh), F32)], axis=1)
    rhs = jnp.concatenate([blk, -blk * jnp.concatenate([cs_t] * nh, axis=1)], axis=0)
    seg = jnp.dot(lhs, rhs, precision=SPREAD, preferred_element_type=F32)
    decay = jnp.exp(jnp.where(jnp.concatenate([mask] * nh, axis=1), seg, -jnp.inf))
    scores = lax.dot_general(cg.astype(BF16), bg.astype(BF16), (((1,), (1,)), ((), ())), preferred_element_type=F32)
    m_all = (jnp.concatenate([scores] * nh, axis=1) * decay).astype(BF16)
    xdt = xg * dt_x
    xdt_b = xdt.astype(BF16)
    xde = (xdt * jnp.exp(tot_x - cs_x)).astype(BF16)
    e_in = jnp.exp(cs_x)
    low = lax.broadcasted_iota(jnp.int32, (1, 2 * P), 1) < P
    cb, bb = cg.astype(BF16), bg.astype(BF16)
    zero = jnp.zeros((), BF16)
    ys, sts = [], []
    for p in range(nh // 2):
        sl = slice(2 * p * P, 2 * (p + 1) * P)
        xp = xdt_b[:, sl]
        y = jnp.dot(m_all[:, 2 * p * Q:(2 * p + 1) * Q], jnp.where(low, xp, zero), preferred_element_type=F32)
        y = y + jnp.dot(m_all[:, (2 * p + 1) * Q:(2 * p + 2) * Q], jnp.where(low, zero, xp), preferred_element_type=F32)
        y = y + lax.dot_general(cb, hin[sl, :].astype(BF16), (((1,), (1,)), ((), ())), preferred_element_type=F32) * e_in[:, sl]
        ys.append(y)
        sts.append(lax.dot_general(xde[:, sl], bb, (((0,), (0,)), ((), ())), preferred_element_type=F32))
    y = jnp.concatenate(ys, axis=1) + dsk_x * xg
    tot_c = jnp.where(rev, cs_t[:, 0:1], cs_t[:, Q - 1:Q])
    etot = lax.dot_general(expand, jnp.broadcast_to(jnp.exp(tot_c), (nh, N)), (((0,), (0,)), ((), ())), precision=SPREAD,
                           preferred_element_type=F32)
    return y, etot * hin + jnp.concatenate(sts, axis=0)


def _ssd_specs(Q, P8, N, ncc, NC, xcol_b, xcol_c, back):
    def chunk(d, s):
        s = (NC - 1 - s) if back else s
        return jnp.where(d == 0, s, jnp.where(s < ncc, ncc - 1 - s, ncc + NC - 1 - s))

    def step(s):
        return (NC - 1 - s) if back else s

    x = pl.BlockSpec((Q, P8), lambda d, g, s: (chunk(d, s), g))
    bsp = pl.BlockSpec((Q, N), lambda d, g, s: (chunk(d, s), xcol_b + g))
    csp = pl.BlockSpec((Q, N), lambda d, g, s: (chunk(d, s), xcol_c + g))
    dt = pl.BlockSpec((None, None, Q, 8), lambda d, g, s: (d, g, chunk(d, s), 0))
    par = pl.BlockSpec((None, None, 1, 8), lambda d, g, s: (d, g, 0, 0))
    hst = pl.BlockSpec((None, None, None, P8, N), lambda d, g, s: (d, g, step(s), 0, 0))
    yd = pl.BlockSpec((None, Q, P8), lambda d, g, s: (d, chunk(d, s), g))
    bd = pl.BlockSpec((None, Q, N), lambda d, g, s: (d, chunk(d, s), g))
    return x, bsp, csp, dt, par, hst, yd, bd


def ssd_fwd(act, dtr, bias, alog, dsk, d_inner, ncc, name):
    T = act.shape[0]
    G = dtr.shape[1]
    Q, N = CHUNK, N_STATE
    NC = T // Q
    P8 = d_inner // G
    x, bsp, csp, dt, par, hst, yd, _ = _ssd_specs(Q, P8, N, ncc, NC, d_inner // N, d_inner // N + G, False)

    def body(x_ref, b_ref, c_ref, dt_ref, bias_ref, alog_ref, dsk_ref, y_ref, h_ref, st_ref):
        d, s = pl.program_id(0), pl.program_id(2)

        @pl.when(s == 0)
        def _():
            st_ref[...] = jnp.zeros_like(st_ref)

        hin = st_ref[...]
        h_ref[...] = hin
        y, ho = _ssd_chunk(x_ref[...], b_ref[...], c_ref[...], dt_ref[...], hin, bias_ref[...], alog_ref[...], dsk_ref[...], d == 1)
        y_ref[...] = y
        st_ref[...] = ho

    return pl.pallas_call(
        body, out_shape=[_sds((2, T, d_inner), F32), _sds((2, G, NC, P8, N), F32)], grid=(2, G, NC),
        in_specs=[x, bsp, csp, dt, par, par, par], out_specs=[yd, hst], scratch_shapes=[pltpu.VMEM((P8, N), F32)],
        compiler_params=_cp(3), name=name)(act, act, act, dtr, bias, alog, dsk)


def ssd_bwd(act, dtr, bias, alog, dsk, hsave, dy, d_inner, ncc, name):
    T = act.shape[0]
    G = dtr.shape[1]
    Q, N = CHUNK, N_STATE
    NC = T // Q
    P8 = d_inner // G
    x, bsp, csp, dt, par, hst, yd, bd = _ssd_specs(Q, P8, N, ncc, NC, d_inner // N, d_inner // N + G, True)
    dysp = pl.BlockSpec((Q, P8), x.index_map)

    def body(x_ref, b_ref, c_ref, dt_ref, bias_ref, alog_ref, dsk_ref, h_ref, dy_ref,
             dx_ref, db_ref, dc_ref, ddt_ref, dbias_ref, dalog_ref, ddsk_ref, dh_ref):
        d, s = pl.program_id(0), pl.program_id(2)

        @pl.when(s == 0)
        def _():
            dh_ref[...] = jnp.zeros_like(dh_ref)

        args = (x_ref[...], b_ref[...], c_ref[...], dt_ref[...], h_ref[...], bias_ref[...], alog_ref[...], dsk_ref[...])

        _, vjp = jax.vjp(functools.partial(_ssd_chunk, rev=d == 1), *args)
        dx, db, dc, ddt, dhin, dbias, dalog, ddsk = vjp((dy_ref[...], dh_ref[...]))
        dx_ref[...] = dx
        db_ref[...] = db
        dc_ref[...] = dc
        ddt_ref[...] = ddt
        dh_ref[...] = dhin
        _acc(dbias_ref, dbias, s == 0)
        _acc(dalog_ref, dalog, s == 0)
        _acc(ddsk_ref, ddsk, s == 0)

    GN = G * N
    return pl.pallas_call(
        body,
        out_shape=[_sds((2, T, d_inner), F32), _sds((2, T, GN), F32), _sds((2, T, GN), F32), _sds(dtr.shape, F32),
                   _sds(bias.shape, F32), _sds(bias.shape, F32), _sds(bias.shape, F32)],
        grid=(2, G, NC), in_specs=[x, bsp, csp, dt, par, par, par, hst, dysp], out_specs=[yd, bd, bd, dt, par, par, par],
        scratch_shapes=[pltpu.VMEM((P8, N), F32)], compiler_params=_cp(3), name=name)(act, act, act, dtr, bias, alog, dsk, hsave, dy)


def _gnorm_fn(yf, yb, z, g):
    return _rms((yf + yb) * _silu(z), g)


def gnorm_fwd(y2, proj, g, tr, name):
    _, T, C = y2.shape
    yf = pl.BlockSpec((None, tr, C), lambda i: (0, i, 0))
    yb = pl.BlockSpec((None, tr, C), lambda i: (1, i, 0))
    row = pl.BlockSpec((tr, C), lambda i: (i, 0))
    vec = pl.BlockSpec((1, C), lambda i: (0, 0))

    def body(yf_ref, yb_ref, z_ref, g_ref, o_ref):
        o_ref[...] = _gnorm_fn(yf_ref[...], yb_ref[...], z_ref[...], g_ref[...]).astype(BF16)

    return pl.pallas_call(body, out_shape=_sds((T, C), BF16), grid=(T // tr,), in_specs=[yf, yb, row, vec], out_specs=row,
                          compiler_params=_cp(1), name=name)(y2, y2, proj, g)


def gnorm_bwd(y2, proj, g, dyn, tr, name):
    _, T, C = y2.shape
    yf = pl.BlockSpec((None, tr, C), lambda i: (0, i, 0))
    yb = pl.BlockSpec((None, tr, C), lambda i: (1, i, 0))
    row = pl.BlockSpec((tr, C), lambda i: (i, 0))
    vec = pl.BlockSpec((1, C), lambda i: (0, 0))

    def body(yf_ref, yb_ref, z_ref, g_ref, d_ref, dy_ref, dz_ref, dg_ref):
        i = pl.program_id(0)
        _, vjp = jax.vjp(_gnorm_fn, yf_ref[...], yb_ref[...], z_ref[...], g_ref[...])
        dyf, _, dz, dg = vjp(d_ref[...].astype(F32))
        dy_ref[...] = dyf
        dz_ref[...] = dz.astype(BF16)
        _acc(dg_ref, dg, i == 0)

    return pl.pallas_call(body, out_shape=[_sds((T, C), F32), _sds((T, C), BF16), _sds((1, C), F32)], grid=(T // tr,),
                          in_specs=[yf, yb, row, vec, row], out_specs=[row, row, vec], compiler_params=_cp(1), name=name)(y2, y2, proj, g, dyn)


def _glu_fn(a):
    D = a.shape[1] // 2
    return a[:, :D] * jax.nn.sigmoid(a[:, D:])


def _ln_swish_fn(v, g, b):
    mu = jnp.mean(v, axis=-1, keepdims=True)
    xc = v - mu
    var = jnp.mean(xc * xc, axis=-1, keepdims=True)
    y = xc * lax.rsqrt(var + EPS) * g + b
    return y * jax.nn.sigmoid(y)


def _seg_pos(tr, seg, i):
    p = lax.broadcasted_iota(jnp.int32, (tr, 1), 0)
    s = jnp.where(i == 0, tr, seg)
    return p & (s - 1), s


def _dw_taps(v, w, pos, s, sign):
    tr = v.shape[0]
    K = w.shape[0]
    acc = jnp.zeros_like(v)
    for k in range(K):
        o = sign * (k - K // 2)
        q = pos + o
        ok = jnp.logical_and(q >= 0, q < s).astype(v.dtype)
        acc = acc + w[k:k + 1, :] * (pltpu.roll(v, (-o) % tr, 0) * ok)
    return acc


def _lane_blocks(v, ref):
    for c in range(v.shape[1] // 128):
        ref[c] = v[:, c * 128:(c + 1) * 128]


def _from_lane_blocks(ref):
    return jnp.concatenate([ref[c] for c in range(ref.shape[0])], axis=1)


def confmid_fwd(a, w3, b, lg, lb, seg, tr, name):
    T, D2 = a.shape
    D = D2 // 2
    nb, K, _ = w3.shape
    vec = pl.BlockSpec((1, D), lambda i: (0, 0))
    row = pl.BlockSpec((tr, D), lambda i: (i, 0))

    def body(a_ref, w_ref, b_ref, lg_ref, lb_ref, o_ref, v1_ref, s0_ref, s1_ref):
        i = pl.program_id(0)
        pos, s = _seg_pos(tr, seg, i)
        _lane_blocks(_glu_fn(a_ref[...]), s0_ref)

        def blk(c, carry):
            s1_ref[c] = _dw_taps(s0_ref[c], w_ref[c], pos, s, 1)
            return carry

        lax.fori_loop(0, nb, blk, 0)
        v1 = _from_lane_blocks(s1_ref) + b_ref[...]
        v1_ref[...] = v1
        o_ref[...] = _ln_swish_fn(v1, lg_ref[...], lb_ref[...]).astype(BF16)

    return pl.pallas_call(
        body, out_shape=[_sds((T, D), BF16), _sds((T, D), F32)], grid=(T // tr,),
        in_specs=[pl.BlockSpec((tr, D2), lambda i: (i, 0)), pl.BlockSpec((nb, K, 128), lambda i: (0, 0, 0)), vec, vec, vec],
        out_specs=[row, row], scratch_shapes=[pltpu.VMEM((nb, tr, 128), F32)] * 2, compiler_params=_cp(1), name=name)(a, w3, b, lg, lb)


def confmid_bwd(a, v1, w3, lg, lb, dv, seg, tr, name):
    T, D2 = a.shape
    D = D2 // 2
    nb, K, _ = w3.shape
    vec = pl.BlockSpec((1, D), lambda i: (0, 0))
    vec2 = pl.BlockSpec((1, D2), lambda i: (0, 0))
    wsp = pl.BlockSpec((nb, K, 128), lambda i: (0, 0, 0))
    row = pl.BlockSpec((tr, D), lambda i: (i, 0))

    def body(a_ref, v1_ref, w_ref, lg_ref, lb_ref, dv_ref, da_ref, dsum_ref, dw_ref, db_ref, dlg_ref, dlb_ref, s0_ref, s1_ref, s2_ref):
        i = pl.program_id(0)
        first = i == 0
        pos, s = _seg_pos(tr, seg, i)
        v0, glu_vjp = jax.vjp(_glu_fn, a_ref[...])
        _lane_blocks(v0, s0_ref)
        _, ln_vjp = jax.vjp(_ln_swish_fn, v1_ref[...], lg_ref[...], lb_ref[...])
        dv1, dlg, dlb = ln_vjp(dv_ref[...].astype(F32))
        _acc(db_ref, jnp.sum(dv1, axis=0, keepdims=True), first)
        _acc(dlg_ref, dlg, first)
        _acc(dlb_ref, dlb, first)
        _lane_blocks(dv1, s2_ref)

        @pl.when(first)
        def _():
            dw_ref[...] = jnp.zeros_like(dw_ref)

        def conv_t(c, carry):
            d1, v0c = s2_ref[c], s0_ref[c]
            s1_ref[c] = _dw_taps(d1, w_ref[c], pos, s, -1)
            for k in range(K):
                o = k - K // 2
                q = pos + o
                ok = jnp.logical_and(q >= 0, q < s).astype(F32)
                dw_ref[c, k:k + 1, :] += jnp.sum(d1 * (pltpu.roll(v0c, (-o) % tr, 0) * ok), axis=0, keepdims=True)
            return carry

        lax.fori_loop(0, nb, conv_t, 0)
        (da,) = glu_vjp(_from_lane_blocks(s1_ref))
        da_ref[...] = da.astype(BF16)
        _acc(dsum_ref, jnp.sum(da, axis=0, keepdims=True), first)

    return pl.pallas_call(
        body, out_shape=[_sds((T, D2), BF16), _sds((1, D2), F32), _sds((nb, K, 128), F32), _sds((1, D), F32), _sds((1, D), F32), _sds((1, D), F32)],
        grid=(T // tr,), in_specs=[pl.BlockSpec((tr, D2), lambda i: (i, 0)), row, wsp, vec, vec, row],
        out_specs=[pl.BlockSpec((tr, D2), lambda i: (i, 0)), vec2, wsp, vec, vec, vec], scratch_shapes=[pltpu.VMEM((nb, tr, 128), F32)] * 3,
        compiler_params=_cp(1), name=name)(a, v1, w3, lg, lb, dv)


def mod_fwd(rows, w, bsl, name):
    Ly, D, Nc = w.shape
    tn = _div_tile(Nc, 512)

    def body(r_ref, w_ref, b_ref, o_ref):
        s = _silu(r_ref[...]).astype(BF16)
        o_ref[...] = jnp.dot(s, w_ref[...].astype(BF16), preferred_element_type=F32) + b_ref[...]

    return pl.pallas_call(
        body, out_shape=_sds((Ly, 16, Nc), F32), grid=(Ly, Nc // tn),
        in_specs=[pl.BlockSpec((16, D), lambda l, j: (0, 0)), pl.BlockSpec((None, D, tn), lambda l, j: (l, 0, j)),
                  pl.BlockSpec((None, 1, tn), lambda l, j: (l, 0, j))],
        out_specs=pl.BlockSpec((None, 16, tn), lambda l, j: (l, 0, j)), compiler_params=_cp(2), name=name)(rows, w, bsl)


def mod_bwd(rows, w, dm, name):
    Ly, D, Nc = w.shape
    tn = _div_tile(Nc, 512)
    nj = Nc // tn

    def body(r_ref, w_ref, dm_ref, dw_ref, ds_ref):
        j = pl.program_id(1)
        s = _silu(r_ref[...]).astype(BF16)
        dmv = dm_ref[...].astype(BF16)
        dw_ref[...] = lax.dot_general(s, dmv, (((0,), (0,)), ((), ())), preferred_element_type=F32)
        _acc(ds_ref, lax.dot_general(dmv, w_ref[...].astype(BF16), (((1,), (1,)), ((), ())), preferred_element_type=F32), j == 0)

    return pl.pallas_call(
        body, out_shape=[_sds((Ly, D, Nc), F32), _sds((Ly, 16, D), F32)], grid=(Ly, nj),
        in_specs=[pl.BlockSpec((16, D), lambda l, j: (0, 0)), pl.BlockSpec((None, D, tn), lambda l, j: (l, 0, j)),
                  pl.BlockSpec((None, 16, tn), lambda l, j: (l, 0, j))],
        out_specs=[pl.BlockSpec((None, D, tn), lambda l, j: (l, 0, j)), pl.BlockSpec((None, 16, D), lambda l, j: (l, 0, 0))],
        compiler_params=_cp(2), name=name)(rows, w, dm)


def silu_grad(dsc, c, name):
    def body(d_ref, c_ref, o_ref):
        x = c_ref[...]
        s = jax.nn.sigmoid(x)
        o_ref[...] = d_ref[...] * (s * (1.0 + x * (1.0 - s)))

    return pl.pallas_call(body, out_shape=_sds(c.shape, F32), name=name)(dsc, c)


def _coords():
    return lax.axis_index("x"), lax.axis_index("y"), lax.axis_index("c")


def _flip(v, bit):
    return 1 - v if bit else v


def allgather8(x, name):
    R, C = x.shape

    def body(x_ref, o_ref, send_sems, recv_sems, local_sem):
        mx, my, mc = _coords()
        me = 4 * mx + 2 * my + mc
        mine = pltpu.make_async_copy(x_ref, o_ref.at[me], local_sem)
        mine.start()
        copies = []
        for k in range(1, 8):
            px, py, pc = _flip(mx, k & 4), _flip(my, k & 2), _flip(mc, k & 1)
            cp = pltpu.make_async_remote_copy(src_ref=x_ref, dst_ref=o_ref.at[me], send_sem=send_sems.at[k - 1],
                                              recv_sem=recv_sems.at[k - 1], device_id=(px, py, pc), device_id_type=MESH)
            cp.start()
            copies.append((cp, 4 * px + 2 * py + pc))
        for k, (cp, peer) in enumerate(copies):
            pltpu.make_async_remote_copy(src_ref=x_ref, dst_ref=o_ref.at[peer], send_sem=send_sems.at[k], recv_sem=recv_sems.at[k],
                                         device_id=(mx, my, mc), device_id_type=MESH).wait_recv()
        for cp, _ in copies:
            cp.wait_send()
        mine.wait()

    return pl.pallas_call(
        body, out_shape=_sds((8, R, C), F32), in_specs=[pl.BlockSpec(memory_space=pltpu.VMEM)],
        out_specs=pl.BlockSpec(memory_space=pltpu.VMEM),
        scratch_shapes=[pltpu.SemaphoreType.DMA((7,)), pltpu.SemaphoreType.DMA((7,)), pltpu.SemaphoreType.DMA],
        compiler_params=pltpu.CompilerParams(vmem_limit_bytes=VMEM_LIMIT), name=name)(x)


def chip_exchange(arrs, name):
    n = len(arrs)

    def src(ref, k):
        return ref.at[k]

    def body(*refs):
        ins, outs = refs[:n], refs[n:2 * n]
        send_sems, recv_sems, local_sems = refs[2 * n:]
        mx, my, mc = _coords()
        me = 2 * mx + my
        started = []
        for a in range(n):
            mine = pltpu.make_async_copy(src(ins[a], me), outs[a].at[me], local_sems.at[a])
            mine.start()
            started.append(mine)
        sends = []
        for a in range(n):
            for k in range(1, 4):
                px, py = _flip(mx, k & 2), _flip(my, k & 1)
                cp = pltpu.make_async_remote_copy(src_ref=src(ins[a], 2 * px + py), dst_ref=outs[a].at[me], send_sem=send_sems.at[3 * a + k - 1],
                                                  recv_sem=recv_sems.at[3 * a + k - 1], device_id=(px, py, mc), device_id_type=MESH)
                cp.start()
                sends.append((cp, a, k, 2 * px + py))
        for cp, a, k, peer in sends:
            pltpu.make_async_remote_copy(src_ref=src(ins[a], me), dst_ref=outs[a].at[peer], send_sem=send_sems.at[3 * a + k - 1],
                                         recv_sem=recv_sems.at[3 * a + k - 1], device_id=(mx, my, mc), device_id_type=MESH).wait_recv()
        for cp, *_ in sends:
            cp.wait_send()
        for mine in started:
            mine.wait()

    hbm = pl.BlockSpec(memory_space=pl.ANY)
    return pl.pallas_call(
        body, out_shape=[_sds(a.shape, a.dtype) for a in arrs], in_specs=[hbm] * n, out_specs=[hbm] * n,
        scratch_shapes=[pltpu.SemaphoreType.DMA((3 * n,)), pltpu.SemaphoreType.DMA((3 * n,)), pltpu.SemaphoreType.DMA((n,))],
        name=name)(*arrs)


def chip_allgather(arrs, name):
    n = len(arrs)

    def body(*refs):
        ins, outs, passed_outs = refs[:n], refs[n:2 * n], refs[2 * n:5 * n]
        send_sems, recv_sems, pass_send, pass_recv, local_sems = refs[5 * n:]
        mx, my, mc = _coords()
        me = 2 * mx + my
        started = []
        for a in range(n):
            mine = pltpu.make_async_copy(ins[a], outs[a].at[me], local_sems.at[a])
            mine.start()
            started.append(mine)
        sends = []
        for a in range(n):
            for k in range(1, 4):
                px, py = _flip(mx, k & 2), _flip(my, k & 1)
                cp = pltpu.make_async_remote_copy(src_ref=ins[a].at[mc], dst_ref=outs[a].at[me, mc],
                                                  send_sem=send_sems.at[3 * a + k - 1], recv_sem=recv_sems.at[3 * a + k - 1],
                                                  device_id=(px, py, mc), device_id_type=MESH)
                cp.start()
                sends.append((cp, a, 3 * a + k - 1, 2 * px + py))
        passed = []
        for cp, a, s, peer in sends:
            got = outs[a].at[peer, mc]
            pltpu.make_async_remote_copy(src_ref=got, dst_ref=got, send_sem=send_sems.at[s], recv_sem=recv_sems.at[s],
                                         device_id=(mx, my, mc), device_id_type=MESH).wait_recv()
            fw = pltpu.make_async_remote_copy(src_ref=got, dst_ref=passed_outs[s], send_sem=pass_send.at[s], recv_sem=pass_recv.at[s],
                                              device_id=(mx, my, 1 - mc), device_id_type=MESH)
            fw.start()
            passed.append(fw)
        for cp, a, s, peer in sends:
            pltpu.make_async_remote_copy(src_ref=passed_outs[s], dst_ref=passed_outs[s], send_sem=pass_send.at[s], recv_sem=pass_recv.at[s],
                                         device_id=(mx, my, mc), device_id_type=MESH).wait_recv()
        for cp, *_ in sends:
            cp.wait_send()
        for fw in passed:
            fw.wait_send()
        for mine in started:
            mine.wait()

    hbm = pl.BlockSpec(memory_space=pl.ANY)
    sems = pltpu.SemaphoreType.DMA((3 * n,))
    res = pl.pallas_call(
        body, out_shape=[_sds((4,) + a.shape, a.dtype) for a in arrs] + [_sds(a.shape[1:], a.dtype) for a in arrs for _ in range(3)],
        in_specs=[hbm] * n, out_specs=[hbm] * (4 * n), scratch_shapes=[sems, sems, sems, sems, pltpu.SemaphoreType.DMA((n,))],
        name=name)(*arrs)
    return res[:n], [res[n + 3 * a:n + 3 * a + 3] for a in range(n)]


def sibling_swap_halves(arrs, name):
    n = len(arrs)

    def body(*refs):
        ins, outs = refs[:n], refs[n:2 * n]
        send_sems, recv_sems = refs[2 * n:]
        mx, my, mc = _coords()
        cps = []
        for a in range(n):
            hl = arrs[a].shape[1] // 2
            cp = pltpu.make_async_remote_copy(src_ref=ins[a].at[pl.ds(0, 4), pl.ds((1 - mc) * hl, hl)], dst_ref=outs[a],
                                              send_sem=send_sems.at[a], recv_sem=recv_sems.at[a],
                                              device_id=(mx, my, 1 - mc), device_id_type=MESH)
            cp.start()
            cps.append(cp)
        for cp in cps:
            cp.wait()

    hbm = pl.BlockSpec(memory_space=pl.ANY)
    return pl.pallas_call(body, out_shape=[_sds((4, a.shape[1] // 2) + a.shape[2:], a.dtype) for a in arrs], in_specs=[hbm] * n,
                          out_specs=[hbm] * n, scratch_shapes=[pltpu.SemaphoreType.DMA((n,)), pltpu.SemaphoreType.DMA((n,))], name=name)(*arrs)


def sibling_exchange(arrs, name):
    n = len(arrs)

    def body(*refs):
        ins, outs = refs[:n], refs[n:2 * n]
        send_sems, recv_sems = refs[2 * n:]
        mx, my, mc = _coords()
        cps = []
        for a in range(n):
            cp = pltpu.make_async_remote_copy(src_ref=ins[a], dst_ref=outs[a], send_sem=send_sems.at[a], recv_sem=recv_sems.at[a],
                                              device_id=(mx, my, 1 - mc), device_id_type=MESH)
            cp.start()
            cps.append(cp)
        for cp in cps:
            cp.wait()

    hbm = pl.BlockSpec(memory_space=pl.ANY)
    return pl.pallas_call(body, out_shape=[_sds(a.shape, a.dtype) for a in arrs], in_specs=[hbm] * n, out_specs=[hbm] * n,
                          scratch_shapes=[pltpu.SemaphoreType.DMA((n,)), pltpu.SemaphoreType.DMA((n,))], name=name)(*arrs)


def add_pair(a, b, name, tr=512):
    R, C = a.shape
    tr = _div_tile(R, tr, 8)
    row = pl.BlockSpec((tr, C), lambda i: (i, 0))

    def body(a_ref, b_ref, o_ref):
        o_ref[...] = (a_ref[...].astype(F32) + b_ref[...].astype(F32)).astype(BF16)

    return pl.pallas_call(body, out_shape=_sds((R, C), BF16), grid=(R // tr,), in_specs=[row, row], out_specs=row,
                          compiler_params=_cp(1), name=name)(a, b)


def sum_slabs(x, name, tr=256):
    n, R, C = x.shape
    tr = _div_tile(R, tr, 8)

    def body(x_ref, o_ref):
        acc = x_ref[0].astype(F32)
        for k in range(1, n):
            acc = acc + x_ref[k].astype(F32)
        o_ref[...] = acc

    return pl.pallas_call(body, out_shape=_sds((R, C), F32), grid=(R // tr,), in_specs=[pl.BlockSpec((n, tr, C), lambda i: (0, i, 0))],
                          out_specs=pl.BlockSpec((tr, C), lambda i: (i, 0)), compiler_params=_cp(1), name=name)(x)


def adamw(w, g, m, v, name, tr=256):
    R, C = w.shape
    split = isinstance(g, tuple)
    nh = 2 if split else 1
    tr = _div_tile(R // nh, tr, 8)
    nt = R // nh // tr
    row = pl.BlockSpec((tr, C), lambda h, i: (h * nt + i, 0))
    part = pl.BlockSpec((tr, C), lambda h, i: (i, 0))
    ins = [w] + (list(g) if split else [g]) + [m, v]

    def body(*refs):
        w_ref = refs[0]
        if split:
            mine = pl.program_id(0) == lax.axis_index("c")
            g = jnp.where(mine, refs[1][...], refs[2][...])
        else:
            g = refs[1][...]
        m_ref, v_ref, go_ref, d_ref, mo_ref, vo_ref = refs[nh + 1:]
        mn = ADAM_B1 * m_ref[...] + (1.0 - ADAM_B1) * g
        vn = ADAM_B2 * v_ref[...] + (1.0 - ADAM_B2) * (g * g)
        m_hat = mn / (1.0 - ADAM_B1 ** ADAM_STEP)
        v_hat = vn / (1.0 - ADAM_B2 ** ADAM_STEP)
        go_ref[...] = g
        d_ref[...] = -ADAM_LR * (m_hat / (jnp.sqrt(v_hat) + ADAM_EPS) + ADAM_WD * w_ref[...])
        mo_ref[...] = mn
        vo_ref[...] = vn

    return pl.pallas_call(body, out_shape=[_sds((R, C), F32)] * 4, grid=(nh, nt), in_specs=[row] + [part] * nh + [row, row],
                          out_specs=[row] * 4, compiler_params=_cp(2), name=name)(*ins)


W_NAMES = ("c_ctx", "mod_w", "mod_b", "pre_mix_g", "post_mix_g", "pre_mlp_g", "post_mlp_g", "mlp_w1", "mlp_w2", "ssm_in_w",
           "ssm_conv_w", "ssm_conv_b", "ssm_a_log_f", "ssm_dt_bias_f", "ssm_d_f", "ssm_a_log_b", "ssm_dt_bias_b", "ssm_d_b",
           "ssm_norm_g", "ssm_out_w", "conf_pw1_w", "conf_pw1_b", "conf_dw_w", "conf_dw_b", "conf_ln_g", "conf_ln_b",
           "conf_pw2_w", "conf_pw2_b")
BIG = {"mlp_w1": "col", "mlp_w2": "row", "ssm_in_w": "col", "ssm_out_w": "row", "conf_pw1_w": "col", "conf_pw2_w": "row"}
SMALL_SHARDED = ("ssm_conv_w", "conf_pw1_b", "conf_dw_w", "conf_dw_b", "conf_ln_g", "conf_ln_b", "conf_pw2_b")
PACK_W = 1024


def _pack(arrs):
    flat = jnp.concatenate([a.reshape(-1).astype(F32) for a in arrs])
    n = flat.shape[0]
    tot = -(-n // (8 * PACK_W)) * (8 * PACK_W)
    return jnp.pad(flat, (0, tot - n)).reshape(tot // PACK_W, PACK_W)


def _unpack(buf, shapes):
    lead = buf.shape[:-2]
    flat = buf.reshape(lead + (-1,))
    out, off = [], 0
    for shp in shapes:
        n = 1
        for d in shp:
            n *= d
        out.append(flat[..., off:off + n].reshape(lead + tuple(shp)))
        off += n
    return out


def _full_from_chips(g, kind):
    if kind == "col":
        return jnp.moveaxis(g, 0, -2).reshape(g.shape[1:-1] + (4 * g.shape[-1],))
    return jnp.moveaxis(g, 0, 1).reshape((g.shape[1], 4 * g.shape[2]) + g.shape[3:])


def _chip_slabs(full, kind):
    if kind == "col":
        return jnp.moveaxis(full.reshape(full.shape[:-1] + (4, full.shape[-1] // 4)), -2, 0)
    return jnp.moveaxis(full.reshape((full.shape[0], 4, full.shape[1] // 4) + full.shape[2:]), 1, 0)


def _view2d(a):
    if a.ndim == 1:
        return a.reshape(1, -1)
    return a.reshape(-1, a.shape[-1])


def kernel(x, c, ctx, c_ctx, mod_w, mod_b, pre_mix_g, post_mix_g, pre_mlp_g, post_mlp_g, mlp_w1, mlp_w2, ssm_in_w, ssm_conv_w, ssm_conv_b, ssm_a_log_f, ssm_dt_bias_f, ssm_d_f, ssm_a_log_b, ssm_dt_bias_b, ssm_d_b, ssm_norm_g, ssm_out_w, conf_pw1_w, conf_pw1_b, conf_dw_w, conf_dw_b, conf_ln_g, conf_ln_b, conf_pw2_w, conf_pw2_b, loss_target, m_c_ctx, m_mod_w, m_mod_b, m_pre_mix_g, m_post_mix_g, m_pre_mlp_g, m_post_mlp_g, m_mlp_w1, m_mlp_w2, m_ssm_in_w, m_ssm_conv_w, m_ssm_conv_b, m_ssm_a_log_f, m_ssm_dt_bias_f, m_ssm_d_f, m_ssm_a_log_b, m_ssm_dt_bias_b, m_ssm_d_b, m_ssm_norm_g, m_ssm_out_w, m_conf_pw1_w, m_conf_pw1_b, m_conf_dw_w, m_conf_dw_b, m_conf_ln_g, m_conf_ln_b, m_conf_pw2_w, m_conf_pw2_b, v_c_ctx, v_mod_w, v_mod_b, v_pre_mix_g, v_post_mix_g, v_pre_mlp_g, v_post_mlp_g, v_mlp_w1, v_mlp_w2, v_ssm_in_w, v_ssm_conv_w, v_ssm_conv_b, v_ssm_a_log_f, v_ssm_dt_bias_f, v_ssm_d_f, v_ssm_a_log_b, v_ssm_dt_bias_b, v_ssm_d_b, v_ssm_norm_g, v_ssm_out_w, v_conf_pw1_w, v_conf_pw1_b, v_conf_dw_w, v_conf_dw_b, v_conf_ln_g, v_conf_ln_b, v_conf_pw2_w, v_conf_pw2_b):
    given = dict(locals())
    W = {n: given[n] for n in W_NAMES}
    L, D = x.shape[1], x.shape[2]
    Lc = ctx.shape[1]
    T = Lc + L
    depth = mod_w.shape[0]
    d_inner = ssm_norm_g.shape[1]
    H = ssm_a_log_f.shape[1]
    xbc = ssm_conv_b.shape[1]
    GN = (xbc - d_inner) // 2
    G = GN // N_STATE
    rows_grid = L // GRID_W
    tr = Lc
    ncc = Lc // CHUNK
    assert H == 8 * G and Lc % CHUNK == 0 and L % Lc == 0 and tr % GRID_W == 0 and tr % rows_grid == 0
    tc = _div_tile(GN, 512)
    assert d_inner % tc == 0
    mx, my, mc = _coords()
    chip = 2 * mx + my
    dev = 4 * mx + 2 * my + mc

    small_shapes = [(1, D)] + [W[n].shape for n in SMALL_SHARDED]
    got = allgather8(_pack([c] + [W[n] for n in SMALL_SHARDED]), "gather_small")
    parts = _unpack(got, small_shapes)
    c_all = parts[0].reshape(8, D)
    full_small = {n: jnp.concatenate([p[2 * k] for k in range(4)], axis=-1) for n, p in zip(SMALL_SHARDED, parts[1:])}

    cond = jnp.concatenate([c_all, c_ctx.reshape(1, D), jnp.zeros((7, D), F32)], axis=0)
    ncol = mod_w.shape[2]
    bsl = lax.dynamic_slice(mod_b, (0, chip * ncol), (depth, ncol)).reshape(depth, 1, ncol)
    m_loc = mod_fwd(cond, mod_w, bsl, "mod_fwd")
    m_all = allgather8(m_loc.reshape(depth * 16, ncol), "gather_mod").reshape(8, depth, 16, ncol)
    m_full = jnp.concatenate([m_all[2 * k] for k in range(4)], axis=-1)
    m_lat = lax.dynamic_slice(m_full, (0, dev, 0), (depth, 1, 6 * D))
    m2 = jnp.concatenate([m_full[:, 8:9], m_lat], axis=1)

    def six(i):
        return [m2[i, :, k * D:(k + 1) * D].reshape(2, 1, D) for k in range(6)]

    big_names = list(BIG)
    gathered, passed_on = chip_allgather([W[n].astype(BF16).reshape((2, W[n].shape[0] // 2) + W[n].shape[1:]) for n in big_names],
                                         "gather_weights")
    Wb = {}
    for n, g, ps in zip(big_names, gathered, passed_on):
        for k in (1, 2, 3):
            at = (2 * _flip(mx, k & 2) + _flip(my, k & 1), 1 - mc) + (0,) * (g.ndim - 2)
            g = lax.dynamic_update_slice(g, ps[k - 1][None, None], at)
        g = g.reshape((4,) + W[n].shape)
        Wb[n] = [_full_from_chips(g[:, j:j + 1], BIG[n])[0] for j in range(W[n].shape[0])]

    def to_scan(u):
        lat = u[Lc:].reshape(rows_grid, GRID_W, u.shape[1]).swapaxes(0, 1).reshape(L, u.shape[1])
        return jnp.concatenate([u[:Lc], lat], axis=0)

    def from_scan(u):
        lat = u[Lc:].reshape(GRID_W, rows_grid, u.shape[1]).swapaxes(0, 1).reshape(L, u.shape[1])
        return jnp.concatenate([u[:Lc], lat], axis=0)

    def ssm_params(j):
        def two(f, b):
            return jnp.stack([f[j], b[j]]).reshape(2, G, 1, 8)
        return two(ssm_dt_bias_f, ssm_dt_bias_b), two(ssm_a_log_f, ssm_a_log_b), two(ssm_d_f, ssm_d_b)

    def dw3_of(j):
        w = full_small["conf_dw_w"][j]
        return w.reshape(w.shape[0], D // 128, 128).swapaxes(0, 1)

    h = jnp.concatenate([ctx[0], x[0]], axis=0)
    saved = []
    for i in range(depth):
        kind, j = i % 2, i // 2
        col_major = (j % 2) == 1
        sh1, sc1, g1, sh2, sc2, g2 = six(i)
        s = {"h": h}
        u = prenorm_fwd(h, pre_mix_g[i][None], sh1, sc1, tr, f"prenorm_mix{i}")
        if col_major:
            u = to_scan(u)
        s["u"] = u
        if kind == 0:
            proj = mm(u, Wb["ssm_in_w"][j], "nn", F32, name=f"ssm_in{i}")
            pre, act = conv5_fwd(proj, full_small["ssm_conv_w"][j], ssm_conv_b[j][None], d_inner // tc, tr, tc, f"ssm_conv{i}")
            dtr = proj[:, d_inner + xbc:].reshape(T, 2, G, 8).transpose(1, 2, 0, 3)
            bias, alog, dsk = ssm_params(j)
            y2, hsave = ssd_fwd(act, dtr, bias, alog, dsk, d_inner, ncc, f"ssd_fwd{i}")
            yn = gnorm_fwd(y2, proj, ssm_norm_g[j][None], CHUNK, f"ssm_gnorm{i}")
            out = mm(yn, Wb["ssm_out_w"][j], "nn", F32, name=f"ssm_out{i}")
            s.update(proj=proj, pre=pre, act=act, dtr=dtr, y2=y2, hsave=hsave, yn=yn)
        else:
            seg = rows_grid if col_major else GRID_W
            a = mm(u, Wb["conf_pw1_w"][j], "nn", F32, bias=full_small["conf_pw1_b"][j][None], name=f"conf_pw1_{i}")
            v, v1 = confmid_fwd(a, dw3_of(j), full_small["conf_dw_b"][j][None], full_small["conf_ln_g"][j][None],
                                full_small["conf_ln_b"][j][None], seg, tr, f"conf_mid{i}")
            out = mm(v, Wb["conf_pw2_w"][j], "nn", F32, bias=full_small["conf_pw2_b"][j][None], name=f"conf_pw2_{i}")
            s.update(a=a, v=v, v1=v1, seg=seg)
        if col_major:
            out = from_scan(out)
        h1 = post_fwd(h, out, post_mix_g[i][None], g1, tr, f"post_mix{i}")
        u2 = prenorm_fwd(h1, pre_mlp_g[i][None], sh2, sc2, tr, f"prenorm_mlp{i}")
        hid, actm = mm(u2, Wb["mlp_w1"][i], "nn", F32, relu2=True, name=f"mlp_up{i}")
        f = mm(actm, Wb["mlp_w2"][i], "nn", F32, name=f"mlp_down{i}")
        h = post_fwd(h1, f, post_mlp_g[i][None], g2, tr, f"post_mlp{i}")
        s.update(out=out, h1=h1, u2=u2, hid=hid, actm=actm, f=f)
        saved.append(s)

    loss_blk, Gr = loss_head(h, loss_target[0], tr, "loss_head")
    loss = lax.psum(loss_blk[0, 0], ("x", "y", "c"))

    gb = {n: [None] * W[n].shape[0] for n in BIG}
    gs = {n: [None] * W[n].shape[0] for n in W_NAMES if n not in BIG and n not in ("c_ctx", "mod_w", "mod_b")}
    dmod = [None] * depth
    for i in reversed(range(depth)):
        kind, j = i % 2, i // 2
        col_major = (j % 2) == 1
        sh1, sc1, g1, sh2, sc2, g2 = six(i)
        s = saved[i]
        df, gs["post_mlp_g"][i], dg2, _ = post_bwd(s["f"], post_mlp_g[i][None], g2, Gr, tr, f"post_mlp_bwd{i}")
        gb["mlp_w2"][i] = mm(s["actm"], df, "tn", BF16, name=f"mlp_down_wg{i}")
        dhid = mm(df, Wb["mlp_w2"][i], "nt", BF16, mul_relu=s["hid"], name=f"mlp_down_dg{i}")
        gb["mlp_w1"][i] = mm(s["u2"], dhid, "tn", BF16, name=f"mlp_up_wg{i}")
        du2 = mm(dhid, Wb["mlp_w1"][i], "nt", F32, name=f"mlp_up_dg{i}")
        Gr, gs["pre_mlp_g"][i], dsh2, dsc2 = prenorm_bwd(s["h1"], pre_mlp_g[i][None], sh2, sc2, du2, Gr, tr, f"prenorm_mlp_bwd{i}")
        dout, gs["post_mix_g"][i], dg1, dout_sum = post_bwd(s["out"], post_mix_g[i][None], g1, Gr, tr, f"post_mix_bwd{i}")
        if col_major:
            dout = to_scan(dout)
        if kind == 0:
            gb["ssm_out_w"][j] = mm(s["yn"], dout, "tn", BF16, name=f"ssm_out_wg{i}")
            dyn = mm(dout, Wb["ssm_out_w"][j], "nt", F32, name=f"ssm_out_dg{i}")
            dys, dz, gs["ssm_norm_g"][j] = gnorm_bwd(s["y2"], s["proj"], ssm_norm_g[j][None], dyn, CHUNK, f"ssm_gnorm_bwd{i}")
            bias, alog, dsk = ssm_params(j)
            dx2, db2, dc2, ddtr, dbias, dalog, ddsk = ssd_bwd(s["act"], s["dtr"], bias, alog, dsk, s["hsave"], dys, d_inner, ncc, f"ssd_bwd{i}")
            cw = full_small["ssm_conv_w"][j]
            nx, nb_ = d_inner // tc, GN // tc
            dxx, dwx, dbx = conv5_bwd(dx2, s["pre"], s["proj"], cw, 0, nx, tr, tc, f"ssm_conv_bwd_x{i}")
            dxb, dwb, dbb = conv5_bwd(db2, s["pre"], s["proj"], cw, nx, 2 * nx, tr, tc, f"ssm_conv_bwd_b{i}")
            dxc, dwc, dbc = conv5_bwd(dc2, s["pre"], s["proj"], cw, nx + nb_, 2 * nx + nb_, tr, tc, f"ssm_conv_bwd_c{i}")
            gs["ssm_conv_w"][j] = jnp.concatenate([dwx, dwb, dwc], axis=1)
            gs["ssm_conv_b"][j] = jnp.concatenate([dbx, dbb, dbc], axis=1)[0]
            ddt = ddtr.transpose(2, 0, 1, 3).reshape(T, 2 * H).astype(BF16)
            dproj = jnp.concatenate([dz, dxx, dxb, dxc, ddt], axis=1)
            gb["ssm_in_w"][j] = mm(s["u"], dproj, "tn", BF16, name=f"ssm_in_wg{i}")
            du = mm(dproj, Wb["ssm_in_w"][j], "nt", F32, name=f"ssm_in_dg{i}")
            for nm, val in (("ssm_dt_bias", dbias), ("ssm_a_log", dalog), ("ssm_d", ddsk)):
                gs[nm + "_f"][j] = val[0].reshape(H)
                gs[nm + "_b"][j] = val[1].reshape(H)
        else:
            gb["conf_pw2_w"][j] = mm(s["v"], dout, "tn", BF16, name=f"conf_pw2_wg{i}")
            gs["conf_pw2_b"][j] = dout_sum[0]
            dv = mm(dout, Wb["conf_pw2_w"][j], "nt", F32, name=f"conf_pw2_dg{i}")
            da, da_sum, dw3, ddb, dlg, dlb = confmid_bwd(s["a"], s["v1"], dw3_of(j), full_small["conf_ln_g"][j][None],
                                                          full_small["conf_ln_b"][j][None], dv, s["seg"], tr, f"conf_mid_bwd{i}")
            gs["conf_pw1_b"][j] = da_sum[0]
            gs["conf_dw_w"][j] = dw3.swapaxes(0, 1).reshape(dw3.shape[1], D)
            gs["conf_dw_b"][j], gs["conf_ln_g"][j], gs["conf_ln_b"][j] = ddb[0], dlg[0], dlb[0]
            gb["conf_pw1_w"][j] = mm(s["u"], da, "tn", BF16, name=f"conf_pw1_wg{i}")
            du = mm(da, Wb["conf_pw1_w"][j], "nt", F32, name=f"conf_pw1_dg{i}")
        if col_major:
            du = from_scan(du)
        Gr, gs["pre_mix_g"][i], dsh1, dsc1 = prenorm_bwd(s["h"], pre_mix_g[i][None], sh1, sc1, du, Gr, tr, f"prenorm_mix_bwd{i}")
        dmod[i] = jnp.concatenate([t.reshape(2, D) for t in (dsh1, dsc1, dg1, dsh2, dsc2, dg2)], axis=1)
    grad_x = Gr[Lc:][None]

    small_names = list(gs)
    small_local = [jnp.stack([t.reshape(W[n].shape[1:] if n not in SMALL_SHARDED else t.shape) for t in gs[n]]) for n in small_names]
    small_shapes = [t.shape for t in small_local] + [(depth, 2, 6 * D)]
    got = allgather8(_pack(small_local + [jnp.stack(dmod)]), "gather_small_grads")
    summed = _unpack(sum_slabs(got, "sum_small_grads"), small_shapes)
    grads = {}
    for n, t in zip(small_names, summed[:-1]):
        if n in SMALL_SHARDED:
            w = W[n].shape[-1]
            t = lax.dynamic_slice_in_dim(t, chip * w, w, axis=t.ndim - 1)
        grads[n] = t
    grads["mod_b"] = summed[-1][:, 0] + summed[-1][:, 1]
    dm_all = _unpack(got, small_shapes)[-1]
    dm_ctx = sum_slabs(dm_all[:, :, 0], "sum_dmod_ctx")
    dm_rows = jnp.concatenate([dm_all[:, :, 1].swapaxes(0, 1), dm_ctx[:, None], jnp.zeros((depth, 7, 6 * D), F32)], axis=1)
    dm_mine = lax.dynamic_slice_in_dim(dm_rows, chip * ncol, ncol, axis=2)
    grads["mod_w"], dcond = mod_bwd(cond, mod_w, dm_mine, "mod_bwd")
    dcc = sum_slabs(dcond[:, 8:9], "sum_dcond_layers")
    dcc_all = allgather8(jnp.pad(dcc, ((0, 7), (0, 0))), "gather_dcond")
    dcc_sum = sum_slabs(dcc_all[0::2, 0:1], "sum_dcond_chips")
    grads["c_ctx"] = silu_grad(dcc_sum, c_ctx.reshape(1, D), "c_ctx_grad").reshape(D)

    slabs = [_chip_slabs(jnp.stack(gb[n]), BIG[n]) for n in big_names]
    theirs = sibling_swap_halves(slabs, "sibling_swap_grads")
    chip_part = []
    for n, s, t in zip(big_names, slabs, theirs):
        hl = t.shape[1]
        own = lax.dynamic_slice_in_dim(s, mc * hl, hl, axis=1)
        chip_part.append(add_pair(_view2d(own), _view2d(t), f"add_cores_{n}").reshape(t.shape))
    recv = chip_exchange(chip_part, "scatter_grads")
    half = [sum_slabs(r.reshape((4, -1, r.shape[-1])), f"sum_grads_{n}") for n, r in zip(big_names, recv)]
    for n, mine, theirs in zip(big_names, half, sibling_exchange(half, "sibling_grads")):
        grads[n] = (mine, theirs)

    res = {}
    for n in W_NAMES:
        w2 = _view2d(W[n])
        cols = w2.shape[1]
        g = grads[n] if isinstance(grads[n], tuple) else _view2d(grads[n])
        outs = adamw(w2, g, _view2d(given["m_" + n]), _view2d(given["v_" + n]), f"adamw_{n}", tr=max(8, (262144 // cols) // 8 * 8))
        res[n] = [o.reshape(W[n].shape) for o in outs]
    return (loss, grad_x, *[res[n][0] for n in W_NAMES], *[res[n][1] for n in W_NAMES], *[res[n][2] for n in W_NAMES],
            *[res[n][3] for n in W_NAMES])
```

```python
import functools

import jax
import jax.numpy as jnp
from jax import lax
from jax.experimental import pallas as pl
from jax.experimental.pallas import tpu as pltpu

GRID_W = 64
CHUNK = 128
N_STATE = 128
EPS = 1e-6
ADAM_LR, ADAM_B1, ADAM_B2, ADAM_EPS, ADAM_WD, ADAM_STEP = 0.001, 0.9, 0.999, 1e-08, 0.01, 10
VMEM_LIMIT = 56 * 1024 * 1024
MM_OPERAND_BYTES = 44 * 1024 * 1024
F32, BF16 = jnp.float32, jnp.bfloat16
HI = lax.Precision.HIGHEST
SPREAD = lax.Precision.HIGH
MESH = pl.DeviceIdType.MESH


def _cp(n_grid):
    return pltpu.CompilerParams(dimension_semantics=("arbitrary",) * n_grid, vmem_limit_bytes=VMEM_LIMIT)


def _sds(shape, dtype):
    return jax.ShapeDtypeStruct(tuple(shape), dtype)


def _div_tile(n, target, unit=128):
    best = None
    t = unit
    while t <= min(n, target):
        if n % t == 0:
            best = t
        t += unit
    return best if best is not None else n


def _rms(x, g):
    return x * lax.rsqrt(jnp.mean(x * x, axis=-1, keepdims=True) + EPS) * g


def _silu(x):
    return x * jax.nn.sigmoid(x)


def mm(a, b, mode, out_dtype, *, bias=None, relu2=False, mul_relu=None, name, tm=768, tn=1152, tk=2048):
    if mode == "nn":
        (M, C), (_, N) = a.shape, b.shape
    elif mode == "nt":
        (M, C), (N, _) = a.shape, b.shape
    else:
        (C, M), (_, N) = a.shape, b.shape
    if mode == "tn":
        tm, tn, tk = _div_tile(M, 1024), _div_tile(N, tn), _div_tile(C, 4224, 8)
    else:
        tm = _div_tile(M, tm, 8)
        if C > tk and C % 128 == 0 and 4 * C * (tm + 512) <= MM_OPERAND_BYTES:
            tn, tk = _div_tile(N, 512), C
        else:
            tn, tk = _div_tile(N, tn), _div_tile(C, 2 * tk if C > tk else tk)
    nk = C // tk
    a_spec = {"nn": pl.BlockSpec((tm, tk), lambda i, j, k: (i, k)), "nt": pl.BlockSpec((tm, tk), lambda i, j, k: (i, k)),
              "tn": pl.BlockSpec((tk, tm), lambda i, j, k: (k, i))}[mode]
    b_spec = {"nn": pl.BlockSpec((tk, tn), lambda i, j, k: (k, j)), "nt": pl.BlockSpec((tn, tk), lambda i, j, k: (j, k)),
              "tn": pl.BlockSpec((tk, tn), lambda i, j, k: (k, j))}[mode]
    dims = {"nn": (((1,), (0,)), ((), ())), "nt": (((1,), (1,)), ((), ())), "tn": (((0,), (0,)), ((), ()))}[mode]
    ins, specs = [a, b], [a_spec, b_spec]
    if bias is not None:
        ins.append(bias)
        specs.append(pl.BlockSpec((1, tn), lambda i, j, k: (0, j)))
    if mul_relu is not None:
        ins.append(mul_relu)
        specs.append(pl.BlockSpec((tm, tn), lambda i, j, k: (i, j)))
    o_spec = pl.BlockSpec((tm, tn), lambda i, j, k: (i, j))
    outs, out_specs = [_sds((M, N), out_dtype)], [o_spec]
    if relu2:
        outs.append(_sds((M, N), BF16))
        out_specs.append(o_spec)

    def body(*refs):
        a_ref, b_ref = refs[0], refs[1]
        pos = 2
        bias_ref = mr_ref = None
        if bias is not None:
            bias_ref = refs[pos]
            pos += 1
        if mul_relu is not None:
            mr_ref = refs[pos]
            pos += 1
        o_ref = refs[pos]
        o2_ref = refs[pos + 1] if relu2 else None
        part = lax.dot_general(a_ref[...].astype(BF16), b_ref[...].astype(BF16), dims, preferred_element_type=F32)

        def finish(r):
            if bias_ref is not None:
                r = r + bias_ref[...]
            if mr_ref is not None:
                r = r * (2.0 * jnp.maximum(mr_ref[...], 0.0))
            o_ref[...] = r.astype(o_ref.dtype)
            if o2_ref is not None:
                q = jnp.maximum(r, 0.0)
                o2_ref[...] = (q * q).astype(BF16)

        if nk == 1:
            finish(part)
        else:
            acc_ref = refs[-1]
            k = pl.program_id(2)

            @pl.when(k == 0)
            def _():
                acc_ref[...] = part

            @pl.when(jnp.logical_and(k > 0, k < nk - 1))
            def _():
                acc_ref[...] += part

            @pl.when(k == nk - 1)
            def _():
                finish(acc_ref[...] + part)

    res = pl.pallas_call(body, out_shape=outs, grid=(M // tm, N // tn, nk), in_specs=specs, out_specs=out_specs,
                         scratch_shapes=[pltpu.VMEM((tm, tn), F32)] if nk > 1 else [], compiler_params=_cp(3), name=name)(*ins)
    return res if relu2 else res[0]


def _seg_spec(D):
    return pl.BlockSpec((None, 1, D), lambda i: (jnp.minimum(i, 1), 0, 0))


def _prenorm_fn(h, g, sh, sc):
    return _rms(h, g) * (1.0 + sc) + sh


def prenorm_fwd(h, g, sh, sc, tr, name):
    T, D = h.shape
    row = pl.BlockSpec((tr, D), lambda i: (i, 0))
    vec = pl.BlockSpec((1, D), lambda i: (0, 0))

    def body(h_ref, g_ref, sh_ref, sc_ref, u_ref):
        u_ref[...] = _prenorm_fn(h_ref[...], g_ref[...], sh_ref[...], sc_ref[...]).astype(BF16)

    return pl.pallas_call(body, out_shape=_sds((T, D), BF16), grid=(T // tr,), in_specs=[row, vec, _seg_spec(D), _seg_spec(D)],
                          out_specs=row, compiler_params=_cp(1), name=name)(h, g, sh, sc)


def _acc(ref, val, first):
    @pl.when(first)
    def _():
        ref[...] = val

    @pl.when(jnp.logical_not(first))
    def _():
        ref[...] += val


def prenorm_bwd(h, g, sh, sc, du, G, tr, name):
    T, D = h.shape
    row = pl.BlockSpec((tr, D), lambda i: (i, 0))
    vec = pl.BlockSpec((1, D), lambda i: (0, 0))

    def body(h_ref, g_ref, sh_ref, sc_ref, du_ref, G_ref, Go_ref, dg_ref, dsh_ref, dsc_ref):
        i = pl.program_id(0)
        _, vjp = jax.vjp(_prenorm_fn, h_ref[...], g_ref[...], sh_ref[...], sc_ref[...])
        dh, dg, dsh, dsc = vjp(du_ref[...].astype(F32))
        Go_ref[...] = G_ref[...] + dh
        _acc(dg_ref, dg, i == 0)
        _acc(dsh_ref, dsh, i <= 1)
        _acc(dsc_ref, dsc, i <= 1)

    return pl.pallas_call(
        body, out_shape=[_sds((T, D), F32), _sds((1, D), F32), _sds((2, 1, D), F32), _sds((2, 1, D), F32)], grid=(T // tr,),
        in_specs=[row, vec, _seg_spec(D), _seg_spec(D), row, row], out_specs=[row, vec, _seg_spec(D), _seg_spec(D)],
        compiler_params=_cp(1), name=name)(h, g, sh, sc, du, G)


def _post_fn(y, gp, gate):
    return gate * _rms(y, gp)


def post_fwd(h, y, gp, gate, tr, name):
    T, D = h.shape
    row = pl.BlockSpec((tr, D), lambda i: (i, 0))
    vec = pl.BlockSpec((1, D), lambda i: (0, 0))

    def body(h_ref, y_ref, gp_ref, gate_ref, o_ref):
        o_ref[...] = h_ref[...] + _post_fn(y_ref[...], gp_ref[...], gate_ref[...])

    return pl.pallas_call(body, out_shape=_sds((T, D), F32), grid=(T // tr,), in_specs=[row, row, vec, _seg_spec(D)],
                          out_specs=row, compiler_params=_cp(1), name=name)(h, y, gp, gate)


def post_bwd(y, gp, gate, G, tr, name):
    T, D = y.shape
    row = pl.BlockSpec((tr, D), lambda i: (i, 0))
    vec = pl.BlockSpec((1, D), lambda i: (0, 0))

    def body(y_ref, gp_ref, gate_ref, G_ref, dy_ref, dgp_ref, dgate_ref, dsum_ref):
        i = pl.program_id(0)
        _, vjp = jax.vjp(_post_fn, y_ref[...], gp_ref[...], gate_ref[...])
        dy, dgp, dgate = vjp(G_ref[...])
        dy_ref[...] = dy.astype(BF16)
        _acc(dgp_ref, dgp, i == 0)
        _acc(dgate_ref, dgate, i <= 1)
        _acc(dsum_ref, jnp.sum(dy, axis=0, keepdims=True), i == 0)

    return pl.pallas_call(
        body, out_shape=[_sds((T, D), BF16), _sds((1, D), F32), _sds((2, 1, D), F32), _sds((1, D), F32)], grid=(T // tr,),
        in_specs=[row, vec, _seg_spec(D), row], out_specs=[row, vec, _seg_spec(D), vec], compiler_params=_cp(1), name=name)(y, gp, gate, G)


def loss_head(h, target, tr, name):
    T, D = h.shape
    row = pl.BlockSpec((tr, D), lambda i: (i, 0))
    trow = pl.BlockSpec((tr, D), lambda i: (jnp.maximum(i - 1, 0), 0))

    def body(h_ref, t_ref, loss_ref, G_ref):
        i = pl.program_id(0)

        @pl.when(i == 0)
        def _():
            loss_ref[...] = jnp.zeros_like(loss_ref)
            G_ref[...] = jnp.zeros_like(G_ref)

        @pl.when(i > 0)
        def _():
            e = h_ref[...] - t_ref[...]
            G_ref[...] = e * (1.0 / D)
            loss_ref[...] += jnp.sum(e * e) * (0.5 / D)

    return pl.pallas_call(body, out_shape=[_sds((8, 128), F32), _sds((T, D), F32)], grid=(T // tr,), in_specs=[row, trow],
                          out_specs=[pl.BlockSpec((8, 128), lambda i: (0, 0)), row], compiler_params=_cp(1), name=name)(h, target)


def _halo_specs(tr, tc, T, col0, lead=()):
    n8 = tr // 8
    nl = len(lead)
    cur = pl.BlockSpec(lead + (tr, tc), lambda j, i: (0,) * nl + (i, col0 + j))
    prev = pl.BlockSpec(lead + (8, tc), lambda j, i: (0,) * nl + (jnp.maximum(i * n8 - 1, 0), col0 + j))
    nxt = pl.BlockSpec(lead + (8, tc), lambda j, i: (0,) * nl + (jnp.minimum((i + 1) * n8, T // 8 - 1), col0 + j))
    return [cur, prev, nxt]


def _with_halo(cur, prev, nxt, i, nt):
    keep_prev = (i >= 2).astype(cur.dtype)
    keep_next = jnp.logical_and(i >= 1, i < nt - 1).astype(cur.dtype)
    return jnp.concatenate([prev * keep_prev, cur, nxt * keep_next], axis=0)


def _shift_rows(ext, o, tr):
    n = ext.shape[0]
    return pltpu.roll(ext, (-o) % n, 0)[8:8 + tr]


def conv5_fwd(proj, w, b, col0, tr, tc, name):
    T = proj.shape[0]
    K, C = w.shape
    nt = T // tr

    def body(x_ref, xp_ref, xn_ref, w_ref, b_ref, pre_ref, act_ref):
        i = pl.program_id(1)
        ext = _with_halo(x_ref[...], xp_ref[...], xn_ref[...], i, nt)
        wv = w_ref[...]
        acc = jnp.zeros((tr, tc), F32) + b_ref[...]
        for k in range(K):
            acc = acc + wv[k:k + 1, :] * _shift_rows(ext, k - K // 2, tr)
        pre_ref[...] = acc
        act_ref[...] = _silu(acc)

    out = pl.BlockSpec((tr, tc), lambda j, i: (i, j))
    return pl.pallas_call(
        body, out_shape=[_sds((T, C), F32), _sds((T, C), F32)], grid=(C // tc, nt),
        in_specs=_halo_specs(tr, tc, T, col0) + [pl.BlockSpec((K, tc), lambda j, i: (0, j)), pl.BlockSpec((1, tc), lambda j, i: (0, j))],
        out_specs=[out, out], compiler_params=_cp(2), name=name)(proj, proj, proj, w, b)


def conv5_bwd(dact, pre, proj, w, colp, colx, tr, tc, name):
    _, T, Cp = dact.shape
    K = w.shape[0]
    nt = T // tr

    def body(d_ref, dp_ref, dn_ref, p_ref, pp_ref, pn_ref, x_ref, xp_ref, xn_ref, w_ref, dx_ref, dw_ref, db_ref):
        i = pl.program_id(1)

        def dpre_of(d, p):
            s = jax.nn.sigmoid(p)
            return (d[0] + d[1]) * (s * (1.0 + p * (1.0 - s)))

        dext = _with_halo(dpre_of(d_ref[...], p_ref[...]), dpre_of(dp_ref[...], pp_ref[...]), dpre_of(dn_ref[...], pn_ref[...]), i, nt)
        xext = _with_halo(x_ref[...], xp_ref[...], xn_ref[...], i, nt)
        dcur = dext[8:8 + tr]
        wv = w_ref[...]
        dx = jnp.zeros((tr, tc), F32)
        for k in range(K):
            o = k - K // 2
            dx = dx + wv[k:k + 1, :] * _shift_rows(dext, -o, tr)
            _acc(dw_ref.at[k:k + 1, :], jnp.sum(dcur * _shift_rows(xext, o, tr), axis=0, keepdims=True), i == 0)
        dx_ref[...] = dx.astype(BF16)
        _acc(db_ref, jnp.sum(dcur, axis=0, keepdims=True), i == 0)

    out = pl.BlockSpec((tr, tc), lambda j, i: (i, j))
    return pl.pallas_call(
        body, out_shape=[_sds((T, Cp), BF16), _sds((K, Cp), F32), _sds((1, Cp), F32)], grid=(Cp // tc, nt),
        in_specs=_halo_specs(tr, tc, T, 0, lead=(2,)) + _halo_specs(tr, tc, T, colp) + _halo_specs(tr, tc, T, colx)
        + [pl.BlockSpec((K, tc), lambda j, i: (0, colp + j))],
        out_specs=[out, pl.BlockSpec((K, tc), lambda j, i: (0, j)), pl.BlockSpec((1, tc), lambda j, i: (0, j))],
        compiler_params=_cp(2), name=name)(dact, dact, dact, pre, pre, pre, proj, proj, proj, w)


def _head_blocks(nh, Q):
    return (lax.broadcasted_iota(jnp.int32, (nh, nh * Q), 0) == lax.broadcasted_iota(jnp.int32, (nh, nh * Q), 1) // Q).astype(F32)


@jax.custom_vjp
def _seg_all(cs):
    Q, nh = cs.shape
    blk = _head_blocks(nh, Q)
    lhs = jnp.concatenate([cs, jnp.ones((Q, nh), F32)], axis=1)
    rhs = jnp.concatenate([blk, -blk * jnp.concatenate([cs.T] * nh, axis=1)], axis=0)
    return jnp.dot(lhs, rhs, precision=SPREAD, preferred_element_type=F32)


def _seg_all_fwd(cs):
    return _seg_all(cs), None


def _seg_all_bwd(_, d):
    Q, nq = d.shape
    nh = nq // Q
    d_row = lax.dot_general(d, _head_blocks(nh, Q), (((1,), (1,)), ((), ())), precision=SPREAD, preferred_element_type=F32)
    col = jnp.sum(d, axis=0, keepdims=True)
    d_col = jnp.concatenate([col[:, j * Q:(j + 1) * Q] for j in range(nh)], axis=0).T
    return (d_row - d_col,)


_seg_all.defvjp(_seg_all_fwd, _seg_all_bwd)


def _ssd_chunk(xg, bg, cg, dtr, hin, bias, alog, dsk, rev):
    Q, P8 = xg.shape
    nh = dtr.shape[1]
    P = P8 // nh
    N = bg.shape[1]
    dt = jax.nn.softplus(dtr + bias)
    da = dt * (-jnp.exp(alog))
    r_i = lax.broadcasted_iota(jnp.int32, (Q, Q), 0)
    c_i = lax.broadcasted_iota(jnp.int32, (Q, Q), 1)
    mask = jnp.where(rev, c_i - r_i, r_i - c_i) >= 0
    cs = jnp.dot(mask.astype(F32), da, precision=HI, preferred_element_type=F32)
    cs_t = cs.T
    expand = (lax.broadcasted_iota(jnp.int32, (nh, P8), 0) == lax.broadcasted_iota(jnp.int32, (nh, P8), 1) // P).astype(F32)

    def over_lanes(v):
        return jnp.dot(v, expand, precision=SPREAD, preferred_element_type=F32)

    tot = jnp.where(rev, cs[0:1, :], cs[Q - 1:Q, :])
    dt_x, cs_x, tot_x, dsk_x = over_lanes(dt), over_lanes(cs), over_lanes(tot), over_lanes(dsk)
    decay = jnp.exp(jnp.where(jnp.concatenate([mask] * nh, axis=1), _seg_all(cs), -jnp.inf))
    scores = lax.dot_general(cg.astype(BF16), bg.astype(BF16), (((1,), (1,)), ((), ())), preferred_element_type=F32)
    m_all = (jnp.concatenate([scores] * nh, axis=1) * decay).astype(BF16)
    xdt = xg * dt_x
    xdt_b = xdt.astype(BF16)
    xde = (xdt * jnp.exp(tot_x - cs_x)).astype(BF16)
    e_in = jnp.exp(cs_x)
    low = lax.broadcasted_iota(jnp.int32, (1, 2 * P), 1) < P
    cb, bb = cg.astype(BF16), bg.astype(BF16)
    zero = jnp.zeros((), BF16)
    ys, sts = [], []
    for p in range(nh // 2):
        sl = slice(2 * p * P, 2 * (p + 1) * P)
        xp = xdt_b[:, sl]
        y = jnp.dot(m_all[:, 2 * p * Q:(2 * p + 1) * Q], jnp.where(low, xp, zero), preferred_element_type=F32)
        y = y + jnp.dot(m_all[:, (2 * p + 1) * Q:(2 * p + 2) * Q], jnp.where(low, zero, xp), preferred_element_type=F32)
        y = y + lax.dot_general(cb, hin[sl, :].astype(BF16), (((1,), (1,)), ((), ())), preferred_element_type=F32) * e_in[:, sl]
        ys.append(y)
        sts.append(lax.dot_general(xde[:, sl], bb, (((0,), (0,)), ((), ())), preferred_element_type=F32))
    y = jnp.concatenate(ys, axis=1) + dsk_x * xg
    tot_c = jnp.where(rev, cs_t[:, 0:1], cs_t[:, Q - 1:Q])
    etot = lax.dot_general(expand, jnp.broadcast_to(jnp.exp(tot_c), (nh, N)), (((0,), (0,)), ((), ())), precision=SPREAD,
                           preferred_element_type=F32)
    return y, etot * hin + jnp.concatenate(sts, axis=0)


def _ssd_specs(Q, P8, N, ncc, NC, xcol_b, xcol_c, back):
    def chunk(d, s):
        s = (NC - 1 - s) if back else s
        return jnp.where(d == 0, s, jnp.where(s < ncc, ncc - 1 - s, ncc + NC - 1 - s))

    def step(s):
        return (NC - 1 - s) if back else s

    x = pl.BlockSpec((Q, P8), lambda d, g, s: (chunk(d, s), g))
    bsp = pl.BlockSpec((Q, N), lambda d, g, s: (chunk(d, s), xcol_b + g))
    csp = pl.BlockSpec((Q, N), lambda d, g, s: (chunk(d, s), xcol_c + g))
    dt = pl.BlockSpec((None, None, Q, 8), lambda d, g, s: (d, g, chunk(d, s), 0))
    par = pl.BlockSpec((None, None, 1, 8), lambda d, g, s: (d, g, 0, 0))
    hst = pl.BlockSpec((None, None, None, P8, N), lambda d, g, s: (d, g, step(s), 0, 0))
    yd = pl.BlockSpec((None, Q, P8), lambda d, g, s: (d, chunk(d, s), g))
    bd = pl.BlockSpec((None, Q, N), lambda d, g, s: (d, chunk(d, s), g))
    return x, bsp, csp, dt, par, hst, yd, bd


def ssd_fwd(act, dtr, bias, alog, dsk, d_inner, ncc, name):
    T = act.shape[0]
    G = dtr.shape[1]
    Q, N = CHUNK, N_STATE
    NC = T // Q
    P8 = d_inner // G
    x, bsp, csp, dt, par, hst, yd, _ = _ssd_specs(Q, P8, N, ncc, NC, d_inner // N, d_inner // N + G, False)

    def body(x_ref, b_ref, c_ref, dt_ref, bias_ref, alog_ref, dsk_ref, y_ref, h_ref, st_ref):
        d, s = pl.program_id(0), pl.program_id(2)

        @pl.when(s == 0)
        def _():
            st_ref[...] = jnp.zeros_like(st_ref)

        hin = st_ref[...]
        h_ref[...] = hin
        y, ho = _ssd_chunk(x_ref[...], b_ref[...], c_ref[...], dt_ref[...], hin, bias_ref[...], alog_ref[...], dsk_ref[...], d == 1)
        y_ref[...] = y
        st_ref[...] = ho

    return pl.pallas_call(
        body, out_shape=[_sds((2, T, d_inner), F32), _sds((2, G, NC, P8, N), F32)], grid=(2, G, NC),
        in_specs=[x, bsp, csp, dt, par, par, par], out_specs=[yd, hst], scratch_shapes=[pltpu.VMEM((P8, N), F32)],
        compiler_params=_cp(3), name=name)(act, act, act, dtr, bias, alog, dsk)


def ssd_bwd(act, dtr, bias, alog, dsk, hsave, dy, d_inner, ncc, name):
    T = act.shape[0]
    G = dtr.shape[1]
    Q, N = CHUNK, N_STATE
    NC = T // Q
    P8 = d_inner // G
    x, bsp, csp, dt, par, hst, yd, bd = _ssd_specs(Q, P8, N, ncc, NC, d_inner // N, d_inner // N + G, True)
    dysp = pl.BlockSpec((Q, P8), x.index_map)

    def body(x_ref, b_ref, c_ref, dt_ref, bias_ref, alog_ref, dsk_ref, h_ref, dy_ref,
             dx_ref, db_ref, dc_ref, ddt_ref, dbias_ref, dalog_ref, ddsk_ref, dh_ref):
        d, s = pl.program_id(0), pl.program_id(2)

        @pl.when(s == 0)
        def _():
            dh_ref[...] = jnp.zeros_like(dh_ref)

        args = (x_ref[...], b_ref[...], c_ref[...], dt_ref[...], h_ref[...], bias_ref[...], alog_ref[...], dsk_ref[...])

        _, vjp = jax.vjp(functools.partial(_ssd_chunk, rev=d == 1), *args)
        dx, db, dc, ddt, dhin, dbias, dalog, ddsk = vjp((dy_ref[...], dh_ref[...]))
        dx_ref[...] = dx
        db_ref[...] = db
        dc_ref[...] = dc
        ddt_ref[...] = ddt
        dh_ref[...] = dhin
        _acc(dbias_ref, dbias, s == 0)
        _acc(dalog_ref, dalog, s == 0)
        _acc(ddsk_ref, ddsk, s == 0)

    GN = G * N
    return pl.pallas_call(
        body,
        out_shape=[_sds((2, T, d_inner), F32), _sds((2, T, GN), F32), _sds((2, T, GN), F32), _sds(dtr.shape, F32),
                   _sds(bias.shape, F32), _sds(bias.shape, F32), _sds(bias.shape, F32)],
        grid=(2, G, NC), in_specs=[x, bsp, csp, dt, par, par, par, hst, dysp], out_specs=[yd, bd, bd, dt, par, par, par],
        scratch_shapes=[pltpu.VMEM((P8, N), F32)], compiler_params=_cp(3), name=name)(act, act, act, dtr, bias, alog, dsk, hsave, dy)


def _gnorm_fn(yf, yb, z, g):
    return _rms((yf + yb) * _silu(z), g)


def gnorm_fwd(y2, proj, g, tr, name):
    _, T, C = y2.shape
    yf = pl.BlockSpec((None, tr, C), lambda i: (0, i, 0))
    yb = pl.BlockSpec((None, tr, C), lambda i: (1, i, 0))
    row = pl.BlockSpec((tr, C), lambda i: (i, 0))
    vec = pl.BlockSpec((1, C), lambda i: (0, 0))

    def body(yf_ref, yb_ref, z_ref, g_ref, o_ref):
        o_ref[...] = _gnorm_fn(yf_ref[...], yb_ref[...], z_ref[...], g_ref[...]).astype(BF16)

    return pl.pallas_call(body, out_shape=_sds((T, C), BF16), grid=(T // tr,), in_specs=[yf, yb, row, vec], out_specs=row,
                          compiler_params=_cp(1), name=name)(y2, y2, proj, g)


def gnorm_bwd(y2, proj, g, dyn, tr, name):
    _, T, C = y2.shape
    yf = pl.BlockSpec((None, tr, C), lambda i: (0, i, 0))
    yb = pl.BlockSpec((None, tr, C), lambda i: (1, i, 0))
    row = pl.BlockSpec((tr, C), lambda i: (i, 0))
    vec = pl.BlockSpec((1, C), lambda i: (0, 0))

    def body(yf_ref, yb_ref, z_ref, g_ref, d_ref, dy_ref, dz_ref, dg_ref):
        i = pl.program_id(0)
        _, vjp = jax.vjp(_gnorm_fn, yf_ref[...], yb_ref[...], z_ref[...], g_ref[...])
        dyf, _, dz, dg = vjp(d_ref[...].astype(F32))
        dy_ref[...] = dyf
        dz_ref[...] = dz.astype(BF16)
        _acc(dg_ref, dg, i == 0)

    return pl.pallas_call(body, out_shape=[_sds((T, C), F32), _sds((T, C), BF16), _sds((1, C), F32)], grid=(T // tr,),
                          in_specs=[yf, yb, row, vec, row], out_specs=[row, row, vec], compiler_params=_cp(1), name=name)(y2, y2, proj, g, dyn)


def _glu_fn(a):
    D = a.shape[1] // 2
    return a[:, :D] * jax.nn.sigmoid(a[:, D:])


def _ln_swish_fn(v, g, b):
    mu = jnp.mean(v, axis=-1, keepdims=True)
    xc = v - mu
    var = jnp.mean(xc * xc, axis=-1, keepdims=True)
    y = xc * lax.rsqrt(var + EPS) * g + b
    return y * jax.nn.sigmoid(y)


def _seg_pos(tr, seg, i):
    p = lax.broadcasted_iota(jnp.int32, (tr, 1), 0)
    s = jnp.where(i == 0, tr, seg)
    return p & (s - 1), s


def _dw_taps(v, w, pos, s, sign):
    tr = v.shape[0]
    K = w.shape[0]
    acc = jnp.zeros_like(v)
    for k in range(K):
        o = sign * (k - K // 2)
        q = pos + o
        ok = jnp.logical_and(q >= 0, q < s).astype(v.dtype)
        acc = acc + w[k:k + 1, :] * (pltpu.roll(v, (-o) % tr, 0) * ok)
    return acc


def _lane_blocks(v, ref):
    for c in range(v.shape[1] // 128):
        ref[c] = v[:, c * 128:(c + 1) * 128]


def _from_lane_blocks(ref):
    return jnp.concatenate([ref[c] for c in range(ref.shape[0])], axis=1)


def confmid_fwd(a, w3, b, lg, lb, seg, tr, name):
    T, D2 = a.shape
    D = D2 // 2
    nb, K, _ = w3.shape
    vec = pl.BlockSpec((1, D), lambda i: (0, 0))
    row = pl.BlockSpec((tr, D), lambda i: (i, 0))

    def body(a_ref, w_ref, b_ref, lg_ref, lb_ref, o_ref, v1_ref, s0_ref, s1_ref):
        i = pl.program_id(0)
        pos, s = _seg_pos(tr, seg, i)
        _lane_blocks(_glu_fn(a_ref[...]), s0_ref)

        def blk(c, carry):
            s1_ref[c] = _dw_taps(s0_ref[c], w_ref[c], pos, s, 1)
            return carry

        lax.fori_loop(0, nb, blk, 0)
        v1 = _from_lane_blocks(s1_ref) + b_ref[...]
        v1_ref[...] = v1
        o_ref[...] = _ln_swish_fn(v1, lg_ref[...], lb_ref[...]).astype(BF16)

    return pl.pallas_call(
        body, out_shape=[_sds((T, D), BF16), _sds((T, D), F32)], grid=(T // tr,),
        in_specs=[pl.BlockSpec((tr, D2), lambda i: (i, 0)), pl.BlockSpec((nb, K, 128), lambda i: (0, 0, 0)), vec, vec, vec],
        out_specs=[row, row], scratch_shapes=[pltpu.VMEM((nb, tr, 128), F32)] * 2, compiler_params=_cp(1), name=name)(a, w3, b, lg, lb)


def confmid_bwd(a, v1, w3, lg, lb, dv, seg, tr, name):
    T, D2 = a.shape
    D = D2 // 2
    nb, K, _ = w3.shape
    vec = pl.BlockSpec((1, D), lambda i: (0, 0))
    vec2 = pl.BlockSpec((1, D2), lambda i: (0, 0))
    wsp = pl.BlockSpec((nb, K, 128), lambda i: (0, 0, 0))
    row = pl.BlockSpec((tr, D), lambda i: (i, 0))

    def body(a_ref, v1_ref, w_ref, lg_ref, lb_ref, dv_ref, da_ref, dsum_ref, dw_ref, db_ref, dlg_ref, dlb_ref, s0_ref, s1_ref, s2_ref):
        i = pl.program_id(0)
        first = i == 0
        pos, s = _seg_pos(tr, seg, i)
        v0, glu_vjp = jax.vjp(_glu_fn, a_ref[...])
        _lane_blocks(v0, s0_ref)
        _, ln_vjp = jax.vjp(_ln_swish_fn, v1_ref[...], lg_ref[...], lb_ref[...])
        dv1, dlg, dlb = ln_vjp(dv_ref[...].astype(F32))
        _acc(db_ref, jnp.sum(dv1, axis=0, keepdims=True), first)
        _acc(dlg_ref, dlg, first)
        _acc(dlb_ref, dlb, first)
        _lane_blocks(dv1, s2_ref)

        @pl.when(first)
        def _():
            dw_ref[...] = jnp.zeros_like(dw_ref)

        def conv_t(c, carry):
            d1, v0c, wv = s2_ref[c], s0_ref[c], w_ref[c]
            acc = jnp.zeros_like(d1)
            for k in range(K):
                o = k - K // 2
                q = pos + o
                dm = d1 * jnp.logical_and(q >= 0, q < s).astype(F32)
                acc = acc + wv[k:k + 1, :] * pltpu.roll(dm, o % tr, 0)
                dw_ref[c, k:k + 1, :] += jnp.sum(dm * pltpu.roll(v0c, (-o) % tr, 0), axis=0, keepdims=True)
            s1_ref[c] = acc
            return carry

        lax.fori_loop(0, nb, conv_t, 0)
        (da,) = glu_vjp(_from_lane_blocks(s1_ref))
        da_ref[...] = da.astype(BF16)
        _acc(dsum_ref, jnp.sum(da, axis=0, keepdims=True), first)

    return pl.pallas_call(
        body, out_shape=[_sds((T, D2), BF16), _sds((1, D2), F32), _sds((nb, K, 128), F32), _sds((1, D), F32), _sds((1, D), F32), _sds((1, D), F32)],
        grid=(T // tr,), in_specs=[pl.BlockSpec((tr, D2), lambda i: (i, 0)), row, wsp, vec, vec, row],
        out_specs=[pl.BlockSpec((tr, D2), lambda i: (i, 0)), vec2, wsp, vec, vec, vec], scratch_shapes=[pltpu.VMEM((nb, tr, 128), F32)] * 3,
        compiler_params=_cp(1), name=name)(a, v1, w3, lg, lb, dv)


def mod_fwd(rows, w, bsl, name):
    Ly, D, Nc = w.shape
    tn = _div_tile(Nc, 512)

    def body(r_ref, w_ref, b_ref, o_ref):
        s = _silu(r_ref[...]).astype(BF16)
        o_ref[...] = jnp.dot(s, w_ref[...].astype(BF16), preferred_element_type=F32) + b_ref[...]

    return pl.pallas_call(
        body, out_shape=_sds((Ly, 16, Nc), F32), grid=(Ly, Nc // tn),
        in_specs=[pl.BlockSpec((16, D), lambda l, j: (0, 0)), pl.BlockSpec((None, D, tn), lambda l, j: (l, 0, j)),
                  pl.BlockSpec((None, 1, tn), lambda l, j: (l, 0, j))],
        out_specs=pl.BlockSpec((None, 16, tn), lambda l, j: (l, 0, j)), compiler_params=_cp(2), name=name)(rows, w, bsl)


def mod_bwd(rows, w, dm, name):
    Ly, D, Nc = w.shape
    tn = _div_tile(Nc, 512)
    nj = Nc // tn

    def body(r_ref, w_ref, dm_ref, dw_ref, ds_ref):
        j = pl.program_id(1)
        s = _silu(r_ref[...]).astype(BF16)
        dmv = dm_ref[...].astype(BF16)
        dw_ref[...] = lax.dot_general(s, dmv, (((0,), (0,)), ((), ())), preferred_element_type=F32)
        _acc(ds_ref, lax.dot_general(dmv, w_ref[...].astype(BF16), (((1,), (1,)), ((), ())), preferred_element_type=F32), j == 0)

    return pl.pallas_call(
        body, out_shape=[_sds((Ly, D, Nc), F32), _sds((Ly, 16, D), F32)], grid=(Ly, nj),
        in_specs=[pl.BlockSpec((16, D), lambda l, j: (0, 0)), pl.BlockSpec((None, D, tn), lambda l, j: (l, 0, j)),
                  pl.BlockSpec((None, 16, tn), lambda l, j: (l, 0, j))],
        out_specs=[pl.BlockSpec((None, D, tn), lambda l, j: (l, 0, j)), pl.BlockSpec((None, 16, D), lambda l, j: (l, 0, 0))],
        compiler_params=_cp(2), name=name)(rows, w, dm)


def silu_grad(dsc, c, name):
    def body(d_ref, c_ref, o_ref):
        x = c_ref[...]
        s = jax.nn.sigmoid(x)
        o_ref[...] = d_ref[...] * (s * (1.0 + x * (1.0 - s)))

    return pl.pallas_call(body, out_shape=_sds(c.shape, F32), name=name)(dsc, c)


def _coords():
    return lax.axis_index("x"), lax.axis_index("y"), lax.axis_index("c")


def _flip(v, bit):
    return 1 - v if bit else v


def allgather8(x, name):
    R, C = x.shape

    def body(x_ref, o_ref, send_sems, recv_sems, local_sem):
        mx, my, mc = _coords()
        me = 4 * mx + 2 * my + mc
        mine = pltpu.make_async_copy(x_ref, o_ref.at[me], local_sem)
        mine.start()
        copies = []
        for k in range(1, 8):
            px, py, pc = _flip(mx, k & 4), _flip(my, k & 2), _flip(mc, k & 1)
            cp = pltpu.make_async_remote_copy(src_ref=x_ref, dst_ref=o_ref.at[me], send_sem=send_sems.at[k - 1],
                                              recv_sem=recv_sems.at[k - 1], device_id=(px, py, pc), device_id_type=MESH)
            cp.start()
            copies.append((cp, 4 * px + 2 * py + pc))
        for k, (cp, peer) in enumerate(copies):
            pltpu.make_async_remote_copy(src_ref=x_ref, dst_ref=o_ref.at[peer], send_sem=send_sems.at[k], recv_sem=recv_sems.at[k],
                                         device_id=(mx, my, mc), device_id_type=MESH).wait_recv()
        for cp, _ in copies:
            cp.wait_send()
        mine.wait()

    return pl.pallas_call(
        body, out_shape=_sds((8, R, C), F32), in_specs=[pl.BlockSpec(memory_space=pltpu.VMEM)],
        out_specs=pl.BlockSpec(memory_space=pltpu.VMEM),
        scratch_shapes=[pltpu.SemaphoreType.DMA((7,)), pltpu.SemaphoreType.DMA((7,)), pltpu.SemaphoreType.DMA],
        compiler_params=pltpu.CompilerParams(vmem_limit_bytes=VMEM_LIMIT), name=name)(x)


def chip_exchange(arrs, name):
    n = len(arrs)

    def src(ref, k):
        return ref.at[k]

    def body(*refs):
        ins, outs = refs[:n], refs[n:2 * n]
        send_sems, recv_sems = refs[2 * n:]
        mx, my, mc = _coords()
        me = 2 * mx + my
        sends = []
        for a in range(n):
            for k in range(1, 4):
                px, py = _flip(mx, k & 2), _flip(my, k & 1)
                cp = pltpu.make_async_remote_copy(src_ref=src(ins[a], 2 * px + py), dst_ref=outs[a].at[me], send_sem=send_sems.at[3 * a + k - 1],
                                                  recv_sem=recv_sems.at[3 * a + k - 1], device_id=(px, py, mc), device_id_type=MESH)
                cp.start()
                sends.append((cp, a, k, 2 * px + py))
        for cp, a, k, peer in sends:
            pltpu.make_async_remote_copy(src_ref=src(ins[a], me), dst_ref=outs[a].at[peer], send_sem=send_sems.at[3 * a + k - 1],
                                         recv_sem=recv_sems.at[3 * a + k - 1], device_id=(mx, my, mc), device_id_type=MESH).wait_recv()
        for cp, *_ in sends:
            cp.wait_send()

    hbm = pl.BlockSpec(memory_space=pl.ANY)
    return pl.pallas_call(
        body, out_shape=[_sds(a.shape, a.dtype) for a in arrs], in_specs=[hbm] * n, out_specs=[hbm] * n,
        scratch_shapes=[pltpu.SemaphoreType.DMA((3 * n,)), pltpu.SemaphoreType.DMA((3 * n,))], name=name)(*arrs)


def chip_allgather(arrs, name):
    n = len(arrs)

    def body(*refs):
        ins, outs, passed_outs = refs[:n], refs[n:2 * n], refs[2 * n:5 * n]
        send_sems, recv_sems, pass_send, pass_recv = refs[5 * n:]
        mx, my, mc = _coords()
        me = 2 * mx + my
        sends = []
        for a in range(n):
            for k in range(1, 4):
                px, py = _flip(mx, k & 2), _flip(my, k & 1)
                cp = pltpu.make_async_remote_copy(src_ref=ins[a].at[mc], dst_ref=outs[a].at[me, mc],
                                                  send_sem=send_sems.at[3 * a + k - 1], recv_sem=recv_sems.at[3 * a + k - 1],
                                                  device_id=(px, py, mc), device_id_type=MESH)
                cp.start()
                sends.append((cp, a, 3 * a + k - 1, 2 * px + py))
        passed = []
        for cp, a, s, peer in sends:
            got = outs[a].at[peer, mc]
            pltpu.make_async_remote_copy(src_ref=got, dst_ref=got, send_sem=send_sems.at[s], recv_sem=recv_sems.at[s],
                                         device_id=(mx, my, mc), device_id_type=MESH).wait_recv()
            fw = pltpu.make_async_remote_copy(src_ref=got, dst_ref=passed_outs[s], send_sem=pass_send.at[s], recv_sem=pass_recv.at[s],
                                              device_id=(mx, my, 1 - mc), device_id_type=MESH)
            fw.start()
            passed.append(fw)
        for cp, a, s, peer in sends:
            pltpu.make_async_remote_copy(src_ref=passed_outs[s], dst_ref=passed_outs[s], send_sem=pass_send.at[s], recv_sem=pass_recv.at[s],
                                         device_id=(mx, my, mc), device_id_type=MESH).wait_recv()
        for cp, *_ in sends:
            cp.wait_send()
        for fw in passed:
            fw.wait_send()

    hbm = pl.BlockSpec(memory_space=pl.ANY)
    sems = pltpu.SemaphoreType.DMA((3 * n,))
    res = pl.pallas_call(
        body, out_shape=[_sds((4,) + a.shape, a.dtype) for a in arrs] + [_sds(a.shape[1:], a.dtype) for a in arrs for _ in range(3)],
        in_specs=[hbm] * n, out_specs=[hbm] * (4 * n), scratch_shapes=[sems, sems, sems, sems], name=name)(*arrs)
    return res[:n], [res[n + 3 * a:n + 3 * a + 3] for a in range(n)]


def sibling_swap_halves(arrs, name):
    n = len(arrs)

    def body(*refs):
        ins, outs = refs[:n], refs[n:2 * n]
        send_sems, recv_sems = refs[2 * n:]
        mx, my, mc = _coords()
        cps = []
        for a in range(n):
            hl = arrs[a].shape[1] // 2
            cp = pltpu.make_async_remote_copy(src_ref=ins[a].at[pl.ds(0, 4), pl.ds((1 - mc) * hl, hl)], dst_ref=outs[a],
                                              send_sem=send_sems.at[a], recv_sem=recv_sems.at[a],
                                              device_id=(mx, my, 1 - mc), device_id_type=MESH)
            cp.start()
            cps.append(cp)
        for cp in cps:
            cp.wait()

    hbm = pl.BlockSpec(memory_space=pl.ANY)
    return pl.pallas_call(body, out_shape=[_sds((4, a.shape[1] // 2) + a.shape[2:], a.dtype) for a in arrs], in_specs=[hbm] * n,
                          out_specs=[hbm] * n, scratch_shapes=[pltpu.SemaphoreType.DMA((n,)), pltpu.SemaphoreType.DMA((n,))], name=name)(*arrs)


def sibling_exchange(arrs, name):
    n = len(arrs)

    def body(*refs):
        ins, outs = refs[:n], refs[n:2 * n]
        send_sems, recv_sems = refs[2 * n:]
        mx, my, mc = _coords()
        cps = []
        for a in range(n):
            cp = pltpu.make_async_remote_copy(src_ref=ins[a], dst_ref=outs[a], send_sem=send_sems.at[a], recv_sem=recv_sems.at[a],
                                              device_id=(mx, my, 1 - mc), device_id_type=MESH)
            cp.start()
            cps.append(cp)
        for cp in cps:
            cp.wait()

    hbm = pl.BlockSpec(memory_space=pl.ANY)
    return pl.pallas_call(body, out_shape=[_sds(a.shape, a.dtype) for a in arrs], in_specs=[hbm] * n, out_specs=[hbm] * n,
                          scratch_shapes=[pltpu.SemaphoreType.DMA((n,)), pltpu.SemaphoreType.DMA((n,))], name=name)(*arrs)


def add_pair(a, b, name, tr=512):
    R, C = a.shape
    tr = _div_tile(R, tr, 8)
    row = pl.BlockSpec((tr, C), lambda i: (i, 0))

    def body(a_ref, b_ref, o_ref):
        o_ref[...] = (a_ref[...].astype(F32) + b_ref[...].astype(F32)).astype(BF16)

    return pl.pallas_call(body, out_shape=_sds((R, C), BF16), grid=(R // tr,), in_specs=[row, row], out_specs=row,
                          compiler_params=_cp(1), name=name)(a, b)


def sum_slabs(x, name, tr=256):
    n, R, C = x.shape
    tr = _div_tile(R, tr, 8)

    def body(x_ref, o_ref):
        acc = x_ref[0].astype(F32)
        for k in range(1, n):
            acc = acc + x_ref[k].astype(F32)
        o_ref[...] = acc

    return pl.pallas_call(body, out_shape=_sds((R, C), F32), grid=(R // tr,), in_specs=[pl.BlockSpec((n, tr, C), lambda i: (0, i, 0))],
                          out_specs=pl.BlockSpec((tr, C), lambda i: (i, 0)), compiler_params=_cp(1), name=name)(x)


def adamw(w, g, m, v, name, tr=256):
    R, C = w.shape
    split = isinstance(g, tuple)
    nh = 2 if split else 1
    tr = _div_tile(R // nh, tr, 8)
    nt = R // nh // tr
    row = pl.BlockSpec((tr, C), lambda h, i: (h * nt + i, 0))
    part = pl.BlockSpec((tr, C), lambda h, i: (i, 0))
    ins = [w] + (list(g) if split else [g]) + [m, v]

    def body(*refs):
        w_ref = refs[0]
        if split:
            mine = pl.program_id(0) == lax.axis_index("c")
            g = jnp.where(mine, refs[1][...], refs[2][...])
        else:
            g = refs[1][...]
        m_ref, v_ref, go_ref, d_ref, mo_ref, vo_ref = refs[nh + 1:]
        mn = ADAM_B1 * m_ref[...] + (1.0 - ADAM_B1) * g
        vn = ADAM_B2 * v_ref[...] + (1.0 - ADAM_B2) * (g * g)
        m_hat = mn / (1.0 - ADAM_B1 ** ADAM_STEP)
        v_hat = vn / (1.0 - ADAM_B2 ** ADAM_STEP)
        go_ref[...] = g
        d_ref[...] = -ADAM_LR * (m_hat / (jnp.sqrt(v_hat) + ADAM_EPS) + ADAM_WD * w_ref[...])
        mo_ref[...] = mn
        vo_ref[...] = vn

    return pl.pallas_call(body, out_shape=[_sds((R, C), F32)] * 4, grid=(nh, nt), in_specs=[row] + [part] * nh + [row, row],
                          out_specs=[row] * 4, compiler_params=_cp(2), name=name)(*ins)


W_NAMES = ("c_ctx", "mod_w", "mod_b", "pre_mix_g", "post_mix_g", "pre_mlp_g", "post_mlp_g", "mlp_w1", "mlp_w2", "ssm_in_w",
           "ssm_conv_w", "ssm_conv_b", "ssm_a_log_f", "ssm_dt_bias_f", "ssm_d_f", "ssm_a_log_b", "ssm_dt_bias_b", "ssm_d_b",
           "ssm_norm_g", "ssm_out_w", "conf_pw1_w", "conf_pw1_b", "conf_dw_w", "conf_dw_b", "conf_ln_g", "conf_ln_b",
           "conf_pw2_w", "conf_pw2_b")
BIG = {"mlp_w1": "col", "mlp_w2": "row", "ssm_in_w": "col", "ssm_out_w": "row", "conf_pw1_w": "col", "conf_pw2_w": "row"}
SMALL_SHARDED = ("ssm_conv_w", "conf_pw1_b", "conf_dw_w", "conf_dw_b", "conf_ln_g", "conf_ln_b", "conf_pw2_b")
PACK_W = 1024


def _pack(arrs):
    flat = jnp.concatenate([a.reshape(-1).astype(F32) for a in arrs])
    n = flat.shape[0]
    tot = -(-n // (8 * PACK_W)) * (8 * PACK_W)
    return jnp.pad(flat, (0, tot - n)).reshape(tot // PACK_W, PACK_W)


def _unpack(buf, shapes):
    lead = buf.shape[:-2]
    flat = buf.reshape(lead + (-1,))
    out, off = [], 0
    for shp in shapes:
        n = 1
        for d in shp:
            n *= d
        out.append(flat[..., off:off + n].reshape(lead + tuple(shp)))
        off += n
    return out


def _full_from_chips(g, kind):
    if kind == "col":
        return jnp.moveaxis(g, 0, -2).reshape(g.shape[1:-1] + (4 * g.shape[-1],))
    return jnp.moveaxis(g, 0, 1).reshape((g.shape[1], 4 * g.shape[2]) + g.shape[3:])


def _chip_slabs(full, kind):
    if kind == "col":
        return jnp.moveaxis(full.reshape(full.shape[:-1] + (4, full.shape[-1] // 4)), -2, 0)
    return jnp.moveaxis(full.reshape((full.shape[0], 4, full.shape[1] // 4) + full.shape[2:]), 1, 0)


def _view2d(a):
    if a.ndim == 1:
        return a.reshape(1, -1)
    return a.reshape(-1, a.shape[-1])


def kernel(x, c, ctx, c_ctx, mod_w, mod_b, pre_mix_g, post_mix_g, pre_mlp_g, post_mlp_g, mlp_w1, mlp_w2, ssm_in_w, ssm_conv_w, ssm_conv_b, ssm_a_log_f, ssm_dt_bias_f, ssm_d_f, ssm_a_log_b, ssm_dt_bias_b, ssm_d_b, ssm_norm_g, ssm_out_w, conf_pw1_w, conf_pw1_b, conf_dw_w, conf_dw_b, conf_ln_g, conf_ln_b, conf_pw2_w, conf_pw2_b, loss_target, m_c_ctx, m_mod_w, m_mod_b, m_pre_mix_g, m_post_mix_g, m_pre_mlp_g, m_post_mlp_g, m_mlp_w1, m_mlp_w2, m_ssm_in_w, m_ssm_conv_w, m_ssm_conv_b, m_ssm_a_log_f, m_ssm_dt_bias_f, m_ssm_d_f, m_ssm_a_log_b, m_ssm_dt_bias_b, m_ssm_d_b, m_ssm_norm_g, m_ssm_out_w, m_conf_pw1_w, m_conf_pw1_b, m_conf_dw_w, m_conf_dw_b, m_conf_ln_g, m_conf_ln_b, m_conf_pw2_w, m_conf_pw2_b, v_c_ctx, v_mod_w, v_mod_b, v_pre_mix_g, v_post_mix_g, v_pre_mlp_g, v_post_mlp_g, v_mlp_w1, v_mlp_w2, v_ssm_in_w, v_ssm_conv_w, v_ssm_conv_b, v_ssm_a_log_f, v_ssm_dt_bias_f, v_ssm_d_f, v_ssm_a_log_b, v_ssm_dt_bias_b, v_ssm_d_b, v_ssm_norm_g, v_ssm_out_w, v_conf_pw1_w, v_conf_pw1_b, v_conf_dw_w, v_conf_dw_b, v_conf_ln_g, v_conf_ln_b, v_conf_pw2_w, v_conf_pw2_b):
    given = dict(locals())
    W = {n: given[n] for n in W_NAMES}
    L, D = x.shape[1], x.shape[2]
    Lc = ctx.shape[1]
    T = Lc + L
    depth = mod_w.shape[0]
    d_inner = ssm_norm_g.shape[1]
    H = ssm_a_log_f.shape[1]
    xbc = ssm_conv_b.shape[1]
    GN = (xbc - d_inner) // 2
    G = GN // N_STATE
    rows_grid = L // GRID_W
    tr = Lc
    ncc = Lc // CHUNK
    assert H == 8 * G and Lc % CHUNK == 0 and L % Lc == 0 and tr % GRID_W == 0 and tr % rows_grid == 0
    tc = _div_tile(GN, 512)
    assert d_inner % tc == 0
    mx, my, mc = _coords()
    chip = 2 * mx + my
    dev = 4 * mx + 2 * my + mc

    small_shapes = [(1, D)] + [W[n].shape for n in SMALL_SHARDED]
    got = allgather8(_pack([c] + [W[n] for n in SMALL_SHARDED]), "gather_small")
    parts = _unpack(got, small_shapes)
    c_all = parts[0].reshape(8, D)
    full_small = {n: jnp.concatenate([p[2 * k] for k in range(4)], axis=-1) for n, p in zip(SMALL_SHARDED, parts[1:])}

    cond = jnp.concatenate([c_all, c_ctx.reshape(1, D), jnp.zeros((7, D), F32)], axis=0)
    ncol = mod_w.shape[2]
    bsl = lax.dynamic_slice(mod_b, (0, chip * ncol), (depth, ncol)).reshape(depth, 1, ncol)
    m_loc = mod_fwd(cond, mod_w, bsl, "mod_fwd")
    m_all = allgather8(m_loc.reshape(depth * 16, ncol), "gather_mod").reshape(8, depth, 16, ncol)
    m_full = jnp.concatenate([m_all[2 * k] for k in range(4)], axis=-1)
    m_lat = lax.dynamic_slice(m_full, (0, dev, 0), (depth, 1, 6 * D))
    m2 = jnp.concatenate([m_full[:, 8:9], m_lat], axis=1)

    def six(i):
        return [m2[i, :, k * D:(k + 1) * D].reshape(2, 1, D) for k in range(6)]

    big_names = list(BIG)
    shards = [W[n].astype(BF16).reshape((2, W[n].shape[0] // 2) + W[n].shape[1:]) for n in big_names]
    gathered, passed_on = chip_allgather(shards, "gather_weights")
    Wb = {}
    for n, own, g, ps in zip(big_names, shards, gathered, passed_on):
        g = lax.dynamic_update_slice(g, own[None], (chip,) + (0,) * own.ndim)
        for k in (1, 2, 3):
            at = (2 * _flip(mx, k & 2) + _flip(my, k & 1), 1 - mc) + (0,) * (g.ndim - 2)
            g = lax.dynamic_update_slice(g, ps[k - 1][None, None], at)
        g = g.reshape((4,) + W[n].shape)
        Wb[n] = [_full_from_chips(g[:, j:j + 1], BIG[n])[0] for j in range(W[n].shape[0])]

    def to_scan(u):
        lat = u[Lc:].reshape(rows_grid, GRID_W, u.shape[1]).swapaxes(0, 1).reshape(L, u.shape[1])
        return jnp.concatenate([u[:Lc], lat], axis=0)

    def from_scan(u):
        lat = u[Lc:].reshape(GRID_W, rows_grid, u.shape[1]).swapaxes(0, 1).reshape(L, u.shape[1])
        return jnp.concatenate([u[:Lc], lat], axis=0)

    def ssm_params(j):
        def two(f, b):
            return jnp.stack([f[j], b[j]]).reshape(2, G, 1, 8)
        return two(ssm_dt_bias_f, ssm_dt_bias_b), two(ssm_a_log_f, ssm_a_log_b), two(ssm_d_f, ssm_d_b)

    def dw3_of(j):
        w = full_small["conf_dw_w"][j]
        return w.reshape(w.shape[0], D // 128, 128).swapaxes(0, 1)

    h = jnp.concatenate([ctx[0], x[0]], axis=0)
    saved = []
    for i in range(depth):
        kind, j = i % 2, i // 2
        col_major = (j % 2) == 1
        sh1, sc1, g1, sh2, sc2, g2 = six(i)
        s = {"h": h}
        u = prenorm_fwd(h, pre_mix_g[i][None], sh1, sc1, tr, f"prenorm_mix{i}")
        if col_major:
            u = to_scan(u)
        s["u"] = u
        if kind == 0:
            proj = mm(u, Wb["ssm_in_w"][j], "nn", F32, name=f"ssm_in{i}")
            pre, act = conv5_fwd(proj, full_small["ssm_conv_w"][j], ssm_conv_b[j][None], d_inner // tc, tr, tc, f"ssm_conv{i}")
            dtr = proj[:, d_inner + xbc:].reshape(T, 2, G, 8).transpose(1, 2, 0, 3)
            bias, alog, dsk = ssm_params(j)
            y2, hsave = ssd_fwd(act, dtr, bias, alog, dsk, d_inner, ncc, f"ssd_fwd{i}")
            yn = gnorm_fwd(y2, proj, ssm_norm_g[j][None], CHUNK, f"ssm_gnorm{i}")
            out = mm(yn, Wb["ssm_out_w"][j], "nn", F32, name=f"ssm_out{i}")
            s.update(proj=proj, pre=pre, act=act, dtr=dtr, y2=y2, hsave=hsave, yn=yn)
        else:
            seg = rows_grid if col_major else GRID_W
            a = mm(u, Wb["conf_pw1_w"][j], "nn", F32, bias=full_small["conf_pw1_b"][j][None], name=f"conf_pw1_{i}")
            v, v1 = confmid_fwd(a, dw3_of(j), full_small["conf_dw_b"][j][None], full_small["conf_ln_g"][j][None],
                                full_small["conf_ln_b"][j][None], seg, tr, f"conf_mid{i}")
            out = mm(v, Wb["conf_pw2_w"][j], "nn", F32, bias=full_small["conf_pw2_b"][j][None], name=f"conf_pw2_{i}")
            s.update(a=a, v=v, v1=v1, seg=seg)
        if col_major:
            out = from_scan(out)
        h1 = post_fwd(h, out, post_mix_g[i][None], g1, tr, f"post_mix{i}")
        u2 = prenorm_fwd(h1, pre_mlp_g[i][None], sh2, sc2, tr, f"prenorm_mlp{i}")
        hid, actm = mm(u2, Wb["mlp_w1"][i], "nn", F32, relu2=True, name=f"mlp_up{i}")
        f = mm(actm, Wb["mlp_w2"][i], "nn", F32, name=f"mlp_down{i}")
        h = post_fwd(h1, f, post_mlp_g[i][None], g2, tr, f"post_mlp{i}")
        s.update(out=out, h1=h1, u2=u2, hid=hid, actm=actm, f=f)
        saved.append(s)

    loss_blk, Gr = loss_head(h, loss_target[0], tr, "loss_head")
    loss = lax.psum(loss_blk[0, 0], ("x", "y", "c"))

    gb = {n: [None] * W[n].shape[0] for n in BIG}
    gs = {n: [None] * W[n].shape[0] for n in W_NAMES if n not in BIG and n not in ("c_ctx", "mod_w", "mod_b")}
    dmod = [None] * depth
    for i in reversed(range(depth)):
        kind, j = i % 2, i // 2
        col_major = (j % 2) == 1
        sh1, sc1, g1, sh2, sc2, g2 = six(i)
        s = saved[i]
        df, gs["post_mlp_g"][i], dg2, _ = post_bwd(s["f"], post_mlp_g[i][None], g2, Gr, tr, f"post_mlp_bwd{i}")
        gb["mlp_w2"][i] = mm(s["actm"], df, "tn", BF16, name=f"mlp_down_wg{i}")
        dhid = mm(df, Wb["mlp_w2"][i], "nt", BF16, mul_relu=s["hid"], name=f"mlp_down_dg{i}")
        gb["mlp_w1"][i] = mm(s["u2"], dhid, "tn", BF16, name=f"mlp_up_wg{i}")
        du2 = mm(dhid, Wb["mlp_w1"][i], "nt", F32, name=f"mlp_up_dg{i}")
        Gr, gs["pre_mlp_g"][i], dsh2, dsc2 = prenorm_bwd(s["h1"], pre_mlp_g[i][None], sh2, sc2, du2, Gr, tr, f"prenorm_mlp_bwd{i}")
        dout, gs["post_mix_g"][i], dg1, dout_sum = post_bwd(s["out"], post_mix_g[i][None], g1, Gr, tr, f"post_mix_bwd{i}")
        if col_major:
            dout = to_scan(dout)
        if kind == 0:
            gb["ssm_out_w"][j] = mm(s["yn"], dout, "tn", BF16, name=f"ssm_out_wg{i}")
            dyn = mm(dout, Wb["ssm_out_w"][j], "nt", F32, name=f"ssm_out_dg{i}")
            dys, dz, gs["ssm_norm_g"][j] = gnorm_bwd(s["y2"], s["proj"], ssm_norm_g[j][None], dyn, CHUNK, f"ssm_gnorm_bwd{i}")
            bias, alog, dsk = ssm_params(j)
            dx2, db2, dc2, ddtr, dbias, dalog, ddsk = ssd_bwd(s["act"], s["dtr"], bias, alog, dsk, s["hsave"], dys, d_inner, ncc, f"ssd_bwd{i}")
            cw = full_small["ssm_conv_w"][j]
            nx, nb_ = d_inner // tc, GN // tc
            dxx, dwx, dbx = conv5_bwd(dx2, s["pre"], s["proj"], cw, 0, nx, tr, tc, f"ssm_conv_bwd_x{i}")
            dxb, dwb, dbb = conv5_bwd(db2, s["pre"], s["proj"], cw, nx, 2 * nx, tr, tc, f"ssm_conv_bwd_b{i}")
            dxc, dwc, dbc = conv5_bwd(dc2, s["pre"], s["proj"], cw, nx + nb_, 2 * nx + nb_, tr, tc, f"ssm_conv_bwd_c{i}")
            gs["ssm_conv_w"][j] = jnp.concatenate([dwx, dwb, dwc], axis=1)
            gs["ssm_conv_b"][j] = jnp.concatenate([dbx, dbb, dbc], axis=1)[0]
            ddt = ddtr.transpose(2, 0, 1, 3).reshape(T, 2 * H).astype(BF16)
            dproj = jnp.concatenate([dz, dxx, dxb, dxc, ddt], axis=1)
            gb["ssm_in_w"][j] = mm(s["u"], dproj, "tn", BF16, name=f"ssm_in_wg{i}")
            du = mm(dproj, Wb["ssm_in_w"][j], "nt", F32, name=f"ssm_in_dg{i}")
            for nm, val in (("ssm_dt_bias", dbias), ("ssm_a_log", dalog), ("ssm_d", ddsk)):
                gs[nm + "_f"][j] = val[0].reshape(H)
                gs[nm + "_b"][j] = val[1].reshape(H)
        else:
            gb["conf_pw2_w"][j] = mm(s["v"], dout, "tn", BF16, name=f"conf_pw2_wg{i}")
            gs["conf_pw2_b"][j] = dout_sum[0]
            dv = mm(dout, Wb["conf_pw2_w"][j], "nt", F32, name=f"conf_pw2_dg{i}")
            da, da_sum, dw3, ddb, dlg, dlb = confmid_bwd(s["a"], s["v1"], dw3_of(j), full_small["conf_ln_g"][j][None],
                                                          full_small["conf_ln_b"][j][None], dv, s["seg"], tr, f"conf_mid_bwd{i}")
            gs["conf_pw1_b"][j] = da_sum[0]
            gs["conf_dw_w"][j] = dw3.swapaxes(0, 1).reshape(dw3.shape[1], D)
            gs["conf_dw_b"][j], gs["conf_ln_g"][j], gs["conf_ln_b"][j] = ddb[0], dlg[0], dlb[0]
            gb["conf_pw1_w"][j] = mm(s["u"], da, "tn", BF16, name=f"conf_pw1_wg{i}")
            du = mm(da, Wb["conf_pw1_w"][j], "nt", F32, name=f"conf_pw1_dg{i}")
        if col_major:
            du = from_scan(du)
        Gr, gs["pre_mix_g"][i], dsh1, dsc1 = prenorm_bwd(s["h"], pre_mix_g[i][None], sh1, sc1, du, Gr, tr, f"prenorm_mix_bwd{i}")
        dmod[i] = jnp.concatenate([t.reshape(2, D) for t in (dsh1, dsc1, dg1, dsh2, dsc2, dg2)], axis=1)
    grad_x = Gr[Lc:][None]

    small_names = list(gs)
    small_local = [jnp.stack([t.reshape(W[n].shape[1:] if n not in SMALL_SHARDED else t.shape) for t in gs[n]]) for n in small_names]
    small_shapes = [t.shape for t in small_local] + [(depth, 2, 6 * D)]
    got = allgather8(_pack(small_local + [jnp.stack(dmod)]), "gather_small_grads")
    summed = _unpack(sum_slabs(got, "sum_small_grads"), small_shapes)
    grads = {}
    for n, t in zip(small_names, summed[:-1]):
        if n in SMALL_SHARDED:
            w = W[n].shape[-1]
            t = lax.dynamic_slice_in_dim(t, chip * w, w, axis=t.ndim - 1)
        grads[n] = t
    grads["mod_b"] = summed[-1][:, 0] + summed[-1][:, 1]
    dm_all = _unpack(got, small_shapes)[-1]
    dm_ctx = sum_slabs(dm_all[:, :, 0], "sum_dmod_ctx")
    dm_rows = jnp.concatenate([dm_all[:, :, 1].swapaxes(0, 1), dm_ctx[:, None], jnp.zeros((depth, 7, 6 * D), F32)], axis=1)
    dm_mine = lax.dynamic_slice_in_dim(dm_rows, chip * ncol, ncol, axis=2)
    grads["mod_w"], dcond = mod_bwd(cond, mod_w, dm_mine, "mod_bwd")
    dcc = sum_slabs(dcond[:, 8:9], "sum_dcond_layers")
    dcc_all = allgather8(jnp.pad(dcc, ((0, 7), (0, 0))), "gather_dcond")
    dcc_sum = sum_slabs(dcc_all[0::2, 0:1], "sum_dcond_chips")
    grads["c_ctx"] = silu_grad(dcc_sum, c_ctx.reshape(1, D), "c_ctx_grad").reshape(D)

    slabs = [_chip_slabs(jnp.stack(gb[n]), BIG[n]) for n in big_names]
    theirs = sibling_swap_halves(slabs, "sibling_swap_grads")
    chip_part = []
    for n, s, t in zip(big_names, slabs, theirs):
        hl = t.shape[1]
        own = lax.dynamic_slice_in_dim(s, mc * hl, hl, axis=1)
        chip_part.append(add_pair(_view2d(own), _view2d(t), f"add_cores_{n}").reshape(t.shape))
    recv = chip_exchange(chip_part, "scatter_grads")
    recv = [lax.dynamic_update_slice(r, lax.dynamic_slice_in_dim(p, chip, 1, axis=0), (chip,) + (0,) * (r.ndim - 1))
            for r, p in zip(recv, chip_part)]
    half = [sum_slabs(r.reshape((4, -1, r.shape[-1])), f"sum_grads_{n}") for n, r in zip(big_names, recv)]
    for n, mine, theirs in zip(big_names, half, sibling_exchange(half, "sibling_grads")):
        grads[n] = (mine, theirs)

    res = {}
    for n in W_NAMES:
        w2 = _view2d(W[n])
        cols = w2.shape[1]
        g = grads[n] if isinstance(grads[n], tuple) else _view2d(grads[n])
        outs = adamw(w2, g, _view2d(given["m_" + n]), _view2d(given["v_" + n]), f"adamw_{n}", tr=max(8, (262144 // cols) // 8 * 8))
        res[n] = [o.reshape(W[n].shape) for o in outs]
    return (loss, grad_x, *[res[n][0] for n in W_NAMES], *[res[n][1] for n in W_NAMES], *[res[n][2] for n in W_NAMES],
            *[res[n][3] for n in W_NAMES])
```

```python
import functools

import jax
import jax.numpy as jnp
from jax import lax
from jax.experimental import pallas as pl
from jax.experimental.pallas import tpu as pltpu

GRID_W = 64
CHUNK = 128
N_STATE = 128
EPS = 1e-6
ADAM_LR, ADAM_B1, ADAM_B2, ADAM_EPS, ADAM_WD, ADAM_STEP = 0.001, 0.9, 0.999, 1e-08, 0.01, 10
VMEM_LIMIT = 56 * 1024 * 1024
MM_OPERAND_BYTES = 44 * 1024 * 1024
F32, BF16 = jnp.float32, jnp.bfloat16
HI = lax.Precision.HIGHEST
SPREAD = lax.Precision.HIGH
MESH = pl.DeviceIdType.MESH


def _cp(n_grid):
    return pltpu.CompilerParams(dimension_semantics=("arbitrary",) * n_grid, vmem_limit_bytes=VMEM_LIMIT)


def _sds(shape, dtype):
    return jax.ShapeDtypeStruct(tuple(shape), dtype)


def _div_tile(n, target, unit=128):
    best = None
    t = unit
    while t <= min(n, target):
        if n % t == 0:
            best = t
        t += unit
    return best if best is not None else n


def _rms(x, g):
    return x * lax.rsqrt(jnp.mean(x * x, axis=-1, keepdims=True) + EPS) * g


def _silu(x):
    return x * jax.nn.sigmoid(x)


def mm(a, b, mode, out_dtype, *, bias=None, relu2=False, mul_relu=None, name, tm=768, tn=1152, tk=2048):
    if mode == "nn":
        (M, C), (_, N) = a.shape, b.shape
    elif mode == "nt":
        (M, C), (N, _) = a.shape, b.shape
    else:
        (C, M), (_, N) = a.shape, b.shape
    if mode == "tn":
        tm, tn, tk = _div_tile(M, 1024), _div_tile(N, tn), _div_tile(C, 4224, 8)
    else:
        tm = _div_tile(M, tm, 8)
        if C > tk and C % 128 == 0 and 4 * C * (tm + 512) <= MM_OPERAND_BYTES:
            tn, tk = _div_tile(N, 512), C
        else:
            tn, tk = _div_tile(N, tn), _div_tile(C, 2 * tk if C > tk else tk)
    nk = C // tk
    a_spec = {"nn": pl.BlockSpec((tm, tk), lambda i, j, k: (i, k)), "nt": pl.BlockSpec((tm, tk), lambda i, j, k: (i, k)),
              "tn": pl.BlockSpec((tk, tm), lambda i, j, k: (k, i))}[mode]
    b_spec = {"nn": pl.BlockSpec((tk, tn), lambda i, j, k: (k, j)), "nt": pl.BlockSpec((tn, tk), lambda i, j, k: (j, k)),
              "tn": pl.BlockSpec((tk, tn), lambda i, j, k: (k, j))}[mode]
    dims = {"nn": (((1,), (0,)), ((), ())), "nt": (((1,), (1,)), ((), ())), "tn": (((0,), (0,)), ((), ()))}[mode]
    ins, specs = [a, b], [a_spec, b_spec]
    if bias is not None:
        ins.append(bias)
        specs.append(pl.BlockSpec((1, tn), lambda i, j, k: (0, j)))
    if mul_relu is not None:
        ins.append(mul_relu)
        specs.append(pl.BlockSpec((tm, tn), lambda i, j, k: (i, j)))
    o_spec = pl.BlockSpec((tm, tn), lambda i, j, k: (i, j))
    outs, out_specs = [_sds((M, N), out_dtype)], [o_spec]
    if relu2:
        outs.append(_sds((M, N), BF16))
        out_specs.append(o_spec)

    def body(*refs):
        a_ref, b_ref = refs[0], refs[1]
        pos = 2
        bias_ref = mr_ref = None
        if bias is not None:
            bias_ref = refs[pos]
            pos += 1
        if mul_relu is not None:
            mr_ref = refs[pos]
            pos += 1
        o_ref = refs[pos]
        o2_ref = refs[pos + 1] if relu2 else None
        part = lax.dot_general(a_ref[...].astype(BF16), b_ref[...].astype(BF16), dims, preferred_element_type=F32)

        def finish(r):
            if bias_ref is not None:
                r = r + bias_ref[...]
            if mr_ref is not None:
                r = r * (2.0 * jnp.maximum(mr_ref[...], 0.0))
            o_ref[...] = r.astype(o_ref.dtype)
            if o2_ref is not None:
                q = jnp.maximum(r, 0.0)
                o2_ref[...] = (q * q).astype(BF16)

        if nk == 1:
            finish(part)
        else:
            acc_ref = refs[-1]
            k = pl.program_id(2)

            @pl.when(k == 0)
            def _():
                acc_ref[...] = part

            @pl.when(jnp.logical_and(k > 0, k < nk - 1))
            def _():
                acc_ref[...] += part

            @pl.when(k == nk - 1)
            def _():
                finish(acc_ref[...] + part)

    res = pl.pallas_call(body, out_shape=outs, grid=(M // tm, N // tn, nk), in_specs=specs, out_specs=out_specs,
                         scratch_shapes=[pltpu.VMEM((tm, tn), F32)] if nk > 1 else [], compiler_params=_cp(3), name=name)(*ins)
    return res if relu2 else res[0]


def _seg_spec(D):
    return pl.BlockSpec((None, 1, D), lambda i: (jnp.minimum(i, 1), 0, 0))


def _prenorm_fn(h, g, sh, sc):
    return _rms(h, g) * (1.0 + sc) + sh


def prenorm_fwd(h, g, sh, sc, tr, name):
    T, D = h.shape
    row = pl.BlockSpec((tr, D), lambda i: (i, 0))
    vec = pl.BlockSpec((1, D), lambda i: (0, 0))

    def body(h_ref, g_ref, sh_ref, sc_ref, u_ref):
        u_ref[...] = _prenorm_fn(h_ref[...], g_ref[...], sh_ref[...], sc_ref[...]).astype(BF16)

    return pl.pallas_call(body, out_shape=_sds((T, D), BF16), grid=(T // tr,), in_specs=[row, vec, _seg_spec(D), _seg_spec(D)],
                          out_specs=row, compiler_params=_cp(1), name=name)(h, g, sh, sc)


def _acc(ref, val, first):
    @pl.when(first)
    def _():
        ref[...] = val

    @pl.when(jnp.logical_not(first))
    def _():
        ref[...] += val


def prenorm_bwd(h, g, sh, sc, du, G, tr, name):
    T, D = h.shape
    row = pl.BlockSpec((tr, D), lambda i: (i, 0))
    vec = pl.BlockSpec((1, D), lambda i: (0, 0))

    def body(h_ref, g_ref, sh_ref, sc_ref, du_ref, G_ref, Go_ref, dg_ref, dsh_ref, dsc_ref):
        i = pl.program_id(0)
        _, vjp = jax.vjp(_prenorm_fn, h_ref[...], g_ref[...], sh_ref[...], sc_ref[...])
        dh, dg, dsh, dsc = vjp(du_ref[...].astype(F32))
        Go_ref[...] = G_ref[...] + dh
        _acc(dg_ref, dg, i == 0)
        _acc(dsh_ref, dsh, i <= 1)
        _acc(dsc_ref, dsc, i <= 1)

    return pl.pallas_call(
        body, out_shape=[_sds((T, D), F32), _sds((1, D), F32), _sds((2, 1, D), F32), _sds((2, 1, D), F32)], grid=(T // tr,),
        in_specs=[row, vec, _seg_spec(D), _seg_spec(D), row, row], out_specs=[row, vec, _seg_spec(D), _seg_spec(D)],
        compiler_params=_cp(1), name=name)(h, g, sh, sc, du, G)


def _post_fn(y, gp, gate):
    return gate * _rms(y, gp)


def post_fwd(h, y, gp, gate, tr, name):
    T, D = h.shape
    row = pl.BlockSpec((tr, D), lambda i: (i, 0))
    vec = pl.BlockSpec((1, D), lambda i: (0, 0))

    def body(h_ref, y_ref, gp_ref, gate_ref, o_ref):
        o_ref[...] = h_ref[...] + _post_fn(y_ref[...], gp_ref[...], gate_ref[...])

    return pl.pallas_call(body, out_shape=_sds((T, D), F32), grid=(T // tr,), in_specs=[row, row, vec, _seg_spec(D)],
                          out_specs=row, compiler_params=_cp(1), name=name)(h, y, gp, gate)


def post_bwd(y, gp, gate, G, tr, name):
    T, D = y.shape
    row = pl.BlockSpec((tr, D), lambda i: (i, 0))
    vec = pl.BlockSpec((1, D), lambda i: (0, 0))

    def body(y_ref, gp_ref, gate_ref, G_ref, dy_ref, dgp_ref, dgate_ref, dsum_ref):
        i = pl.program_id(0)
        _, vjp = jax.vjp(_post_fn, y_ref[...], gp_ref[...], gate_ref[...])
        dy, dgp, dgate = vjp(G_ref[...])
        dy_ref[...] = dy.astype(BF16)
        _acc(dgp_ref, dgp, i == 0)
        _acc(dgate_ref, dgate, i <= 1)
        _acc(dsum_ref, jnp.sum(dy, axis=0, keepdims=True), i == 0)

    return pl.pallas_call(
        body, out_shape=[_sds((T, D), BF16), _sds((1, D), F32), _sds((2, 1, D), F32), _sds((1, D), F32)], grid=(T // tr,),
        in_specs=[row, vec, _seg_spec(D), row], out_specs=[row, vec, _seg_spec(D), vec], compiler_params=_cp(1), name=name)(y, gp, gate, G)


def loss_head(h, target, tr, name):
    T, D = h.shape
    row = pl.BlockSpec((tr, D), lambda i: (i, 0))
    trow = pl.BlockSpec((tr, D), lambda i: (jnp.maximum(i - 1, 0), 0))

    def body(h_ref, t_ref, loss_ref, G_ref):
        i = pl.program_id(0)

        @pl.when(i == 0)
        def _():
            loss_ref[...] = jnp.zeros_like(loss_ref)
            G_ref[...] = jnp.zeros_like(G_ref)

        @pl.when(i > 0)
        def _():
            e = h_ref[...] - t_ref[...]
            G_ref[...] = e * (1.0 / D)
            loss_ref[...] += jnp.sum(e * e) * (0.5 / D)

    return pl.pallas_call(body, out_shape=[_sds((8, 128), F32), _sds((T, D), F32)], grid=(T // tr,), in_specs=[row, trow],
                          out_specs=[pl.BlockSpec((8, 128), lambda i: (0, 0)), row], compiler_params=_cp(1), name=name)(h, target)


def _halo_specs(tr, tc, T, col0, lead=()):
    n8 = tr // 8
    nl = len(lead)
    cur = pl.BlockSpec(lead + (tr, tc), lambda j, i: (0,) * nl + (i, col0 + j))
    prev = pl.BlockSpec(lead + (8, tc), lambda j, i: (0,) * nl + (jnp.maximum(i * n8 - 1, 0), col0 + j))
    nxt = pl.BlockSpec(lead + (8, tc), lambda j, i: (0,) * nl + (jnp.minimum((i + 1) * n8, T // 8 - 1), col0 + j))
    return [cur, prev, nxt]


def _with_halo(cur, prev, nxt, i, nt):
    keep_prev = (i >= 2).astype(cur.dtype)
    keep_next = jnp.logical_and(i >= 1, i < nt - 1).astype(cur.dtype)
    return jnp.concatenate([prev * keep_prev, cur, nxt * keep_next], axis=0)


def _shift_rows(ext, o, tr):
    n = ext.shape[0]
    return pltpu.roll(ext, (-o) % n, 0)[8:8 + tr]


def conv5_fwd(proj, w, b, col0, tr, tc, name):
    T = proj.shape[0]
    K, C = w.shape
    nt = T // tr

    def body(x_ref, xp_ref, xn_ref, w_ref, b_ref, pre_ref, act_ref):
        i = pl.program_id(1)
        ext = _with_halo(x_ref[...], xp_ref[...], xn_ref[...], i, nt)
        wv = w_ref[...]
        acc = jnp.zeros((tr, tc), F32) + b_ref[...]
        for k in range(K):
            acc = acc + wv[k:k + 1, :] * _shift_rows(ext, k - K // 2, tr)
        pre_ref[...] = acc
        act_ref[...] = _silu(acc)

    out = pl.BlockSpec((tr, tc), lambda j, i: (i, j))
    return pl.pallas_call(
        body, out_shape=[_sds((T, C), F32), _sds((T, C), F32)], grid=(C // tc, nt),
        in_specs=_halo_specs(tr, tc, T, col0) + [pl.BlockSpec((K, tc), lambda j, i: (0, j)), pl.BlockSpec((1, tc), lambda j, i: (0, j))],
        out_specs=[out, out], compiler_params=_cp(2), name=name)(proj, proj, proj, w, b)


def conv5_bwd(dact, pre, proj, w, colp, colx, tr, tc, name):
    _, T, Cp = dact.shape
    K = w.shape[0]
    nt = T // tr

    def body(d_ref, dp_ref, dn_ref, p_ref, pp_ref, pn_ref, x_ref, xp_ref, xn_ref, w_ref, dx_ref, dw_ref, db_ref):
        i = pl.program_id(1)

        def dpre_of(d, p):
            s = jax.nn.sigmoid(p)
            return (d[0] + d[1]) * (s * (1.0 + p * (1.0 - s)))

        dext = _with_halo(dpre_of(d_ref[...], p_ref[...]), dpre_of(dp_ref[...], pp_ref[...]), dpre_of(dn_ref[...], pn_ref[...]), i, nt)
        xext = _with_halo(x_ref[...], xp_ref[...], xn_ref[...], i, nt)
        dcur = dext[8:8 + tr]
        wv = w_ref[...]
        dx = jnp.zeros((tr, tc), F32)
        for k in range(K):
            o = k - K // 2
            dx = dx + wv[k:k + 1, :] * _shift_rows(dext, -o, tr)
            _acc(dw_ref.at[k:k + 1, :], jnp.sum(dcur * _shift_rows(xext, o, tr), axis=0, keepdims=True), i == 0)
        dx_ref[...] = dx.astype(BF16)
        _acc(db_ref, jnp.sum(dcur, axis=0, keepdims=True), i == 0)

    out = pl.BlockSpec((tr, tc), lambda j, i: (i, j))
    return pl.pallas_call(
        body, out_shape=[_sds((T, Cp), BF16), _sds((K, Cp), F32), _sds((1, Cp), F32)], grid=(Cp // tc, nt),
        in_specs=_halo_specs(tr, tc, T, 0, lead=(2,)) + _halo_specs(tr, tc, T, colp) + _halo_specs(tr, tc, T, colx)
        + [pl.BlockSpec((K, tc), lambda j, i: (0, colp + j))],
        out_specs=[out, pl.BlockSpec((K, tc), lambda j, i: (0, j)), pl.BlockSpec((1, tc), lambda j, i: (0, j))],
        compiler_params=_cp(2), name=name)(dact, dact, dact, pre, pre, pre, proj, proj, proj, w)


def _head_blocks(nh, Q):
    return (lax.broadcasted_iota(jnp.int32, (nh, nh * Q), 0) == lax.broadcasted_iota(jnp.int32, (nh, nh * Q), 1) // Q).astype(F32)


@jax.custom_vjp
def _seg_all(cs):
    Q, nh = cs.shape
    blk = _head_blocks(nh, Q)
    lhs = jnp.concatenate([cs, jnp.ones((Q, nh), F32)], axis=1)
    rhs = jnp.concatenate([blk, -blk * jnp.concatenate([cs.T] * nh, axis=1)], axis=0)
    return jnp.dot(lhs, rhs, precision=SPREAD, preferred_element_type=F32)


def _seg_all_fwd(cs):
    return _seg_all(cs), None


def _seg_all_bwd(_, d):
    Q, nq = d.shape
    nh = nq // Q
    d_row = lax.dot_general(d, _head_blocks(nh, Q), (((1,), (1,)), ((), ())), precision=SPREAD, preferred_element_type=F32)
    col = jnp.sum(d, axis=0, keepdims=True)
    d_col = jnp.concatenate([col[:, j * Q:(j + 1) * Q] for j in range(nh)], axis=0).T
    return (d_row - d_col,)


_seg_all.defvjp(_seg_all_fwd, _seg_all_bwd)


def _ssd_chunk(xg, bg, cg, dtr, hin, bias, alog, dsk, rev):
    Q, P8 = xg.shape
    nh = dtr.shape[1]
    P = P8 // nh
    N = bg.shape[1]
    dt = jax.nn.softplus(dtr + bias)
    da = dt * (-jnp.exp(alog))
    r_i = lax.broadcasted_iota(jnp.int32, (Q, Q), 0)
    c_i = lax.broadcasted_iota(jnp.int32, (Q, Q), 1)
    mask = jnp.where(rev, c_i - r_i, r_i - c_i) >= 0
    cs = jnp.dot(mask.astype(F32), da, precision=HI, preferred_element_type=F32)
    cs_t = cs.T
    expand = (lax.broadcasted_iota(jnp.int32, (nh, P8), 0) == lax.broadcasted_iota(jnp.int32, (nh, P8), 1) // P).astype(F32)

    def over_lanes(v):
        return jnp.dot(v, expand, precision=SPREAD, preferred_element_type=F32)

    tot = jnp.where(rev, cs[0:1, :], cs[Q - 1:Q, :])
    dt_x, cs_x, tot_x, dsk_x = over_lanes(dt), over_lanes(cs), over_lanes(tot), over_lanes(dsk)
    decay = jnp.exp(jnp.where(jnp.concatenate([mask] * nh, axis=1), _seg_all(cs), -jnp.inf))
    scores = lax.dot_general(cg.astype(BF16), bg.astype(BF16), (((1,), (1,)), ((), ())), preferred_element_type=F32)
    m_all = (jnp.concatenate([scores] * nh, axis=1) * decay).astype(BF16)
    xdt = xg * dt_x
    xdt_b = xdt.astype(BF16)
    xde = (xdt * jnp.exp(tot_x - cs_x)).astype(BF16)
    e_in = jnp.exp(cs_x)
    low = lax.broadcasted_iota(jnp.int32, (1, 2 * P), 1) < P
    cb, bb = cg.astype(BF16), bg.astype(BF16)
    zero = jnp.zeros((), BF16)
    ys, sts = [], []
    for p in range(nh // 2):
        sl = slice(2 * p * P, 2 * (p + 1) * P)
        xp = xdt_b[:, sl]
        y = jnp.dot(m_all[:, 2 * p * Q:(2 * p + 1) * Q], jnp.where(low, xp, zero), preferred_element_type=F32)
        y = y + jnp.dot(m_all[:, (2 * p + 1) * Q:(2 * p + 2) * Q], jnp.where(low, zero, xp), preferred_element_type=F32)
        y = y + lax.dot_general(cb, hin[sl, :].astype(BF16), (((1,), (1,)), ((), ())), preferred_element_type=F32) * e_in[:, sl]
        ys.append(y)
        sts.append(lax.dot_general(xde[:, sl], bb, (((0,), (0,)), ((), ())), preferred_element_type=F32))
    y = jnp.concatenate(ys, axis=1) + dsk_x * xg
    tot_c = jnp.where(rev, cs_t[:, 0:1], cs_t[:, Q - 1:Q])
    etot = lax.dot_general(expand, jnp.broadcast_to(jnp.exp(tot_c), (nh, N)), (((0,), (0,)), ((), ())), precision=SPREAD,
                           preferred_element_type=F32)
    return y, etot * hin + jnp.concatenate(sts, axis=0)


SSD_GROUPS_PER_STEP = 4


def _ssd_specs(Q, P8, N, ncc, NC, gpb, d_inner, GN, back):
    def chunk(d, s):
        s = (NC - 1 - s) if back else s
        return jnp.where(d == 0, s, jnp.where(s < ncc, ncc - 1 - s, ncc + NC - 1 - s))

    def step(s):
        return (NC - 1 - s) if back else s

    nb = gpb * N
    xcol_b, xcol_c = d_inner // nb, (d_inner + GN) // nb
    x = pl.BlockSpec((Q, gpb * P8), lambda d, g, s: (chunk(d, s), g))
    bsp = pl.BlockSpec((Q, nb), lambda d, g, s: (chunk(d, s), xcol_b + g))
    csp = pl.BlockSpec((Q, nb), lambda d, g, s: (chunk(d, s), xcol_c + g))
    dt = pl.BlockSpec((None, gpb, Q, 8), lambda d, g, s: (d, g, chunk(d, s), 0))
    par = pl.BlockSpec((None, gpb, 1, 8), lambda d, g, s: (d, g, 0, 0))
    hst = pl.BlockSpec((None, gpb, None, P8, N), lambda d, g, s: (d, g, step(s), 0, 0))
    yd = pl.BlockSpec((None, Q, gpb * P8), lambda d, g, s: (d, chunk(d, s), g))
    bd = pl.BlockSpec((None, Q, nb), lambda d, g, s: (d, chunk(d, s), g))
    return x, bsp, csp, dt, par, hst, yd, bd


def _ssd_dims(act, dtr, d_inner):
    T, G = act.shape[0], dtr.shape[1]
    gpb = min(SSD_GROUPS_PER_STEP, G)
    GN = G * N_STATE
    assert G % gpb == 0 and d_inner % (gpb * N_STATE) == 0 and GN % (gpb * N_STATE) == 0
    return T, G, gpb, GN, T // CHUNK, d_inner // G


def ssd_fwd(act, dtr, bias, alog, dsk, d_inner, ncc, name):
    Q, N = CHUNK, N_STATE
    T, G, gpb, GN, NC, P8 = _ssd_dims(act, dtr, d_inner)
    x, bsp, csp, dt, par, hst, yd, _ = _ssd_specs(Q, P8, N, ncc, NC, gpb, d_inner, GN, False)

    def body(x_ref, b_ref, c_ref, dt_ref, bias_ref, alog_ref, dsk_ref, y_ref, h_ref, st_ref):
        d, s = pl.program_id(0), pl.program_id(2)

        @pl.when(s == 0)
        def _():
            st_ref[...] = jnp.zeros_like(st_ref)

        for q in range(gpb):
            xs, ns = slice(q * P8, (q + 1) * P8), slice(q * N, (q + 1) * N)
            hin = st_ref[q]
            h_ref[q] = hin
            y, ho = _ssd_chunk(x_ref[:, xs], b_ref[:, ns], c_ref[:, ns], dt_ref[q], hin, bias_ref[q], alog_ref[q], dsk_ref[q], d == 1)
            y_ref[:, xs] = y
            st_ref[q] = ho

    return pl.pallas_call(
        body, out_shape=[_sds((2, T, d_inner), F32), _sds((2, G, NC, P8, N), F32)], grid=(2, G // gpb, NC),
        in_specs=[x, bsp, csp, dt, par, par, par], out_specs=[yd, hst], scratch_shapes=[pltpu.VMEM((gpb, P8, N), F32)],
        compiler_params=_cp(3), name=name)(act, act, act, dtr, bias, alog, dsk)


def ssd_bwd(act, dtr, bias, alog, dsk, hsave, dy, d_inner, ncc, name):
    Q, N = CHUNK, N_STATE
    T, G, gpb, GN, NC, P8 = _ssd_dims(act, dtr, d_inner)
    x, bsp, csp, dt, par, hst, yd, bd = _ssd_specs(Q, P8, N, ncc, NC, gpb, d_inner, GN, True)
    dysp = pl.BlockSpec((Q, gpb * P8), x.index_map)

    def body(x_ref, b_ref, c_ref, dt_ref, bias_ref, alog_ref, dsk_ref, h_ref, dy_ref,
             dx_ref, db_ref, dc_ref, ddt_ref, dbias_ref, dalog_ref, ddsk_ref, dh_ref):
        d, s = pl.program_id(0), pl.program_id(2)

        @pl.when(s == 0)
        def _():
            dh_ref[...] = jnp.zeros_like(dh_ref)

        for q in range(gpb):
            xs, ns = slice(q * P8, (q + 1) * P8), slice(q * N, (q + 1) * N)
            args = (x_ref[:, xs], b_ref[:, ns], c_ref[:, ns], dt_ref[q], h_ref[q], bias_ref[q], alog_ref[q], dsk_ref[q])
            _, vjp = jax.vjp(functools.partial(_ssd_chunk, rev=d == 1), *args)
            dx, db, dc, ddt, dhin, dbias, dalog, ddsk = vjp((dy_ref[:, xs], dh_ref[q]))
            dx_ref[:, xs] = dx
            db_ref[:, ns] = db
            dc_ref[:, ns] = dc
            ddt_ref[q] = ddt
            dh_ref[q] = dhin
            _acc(dbias_ref.at[q], dbias, s == 0)
            _acc(dalog_ref.at[q], dalog, s == 0)
            _acc(ddsk_ref.at[q], ddsk, s == 0)

    return pl.pallas_call(
        body,
        out_shape=[_sds((2, T, d_inner), F32), _sds((2, T, GN), F32), _sds((2, T, GN), F32), _sds(dtr.shape, F32),
                   _sds(bias.shape, F32), _sds(bias.shape, F32), _sds(bias.shape, F32)],
        grid=(2, G // gpb, NC), in_specs=[x, bsp, csp, dt, par, par, par, hst, dysp], out_specs=[yd, bd, bd, dt, par, par, par],
        scratch_shapes=[pltpu.VMEM((gpb, P8, N), F32)], compiler_params=_cp(3), name=name)(act, act, act, dtr, bias, alog, dsk, hsave, dy)


def _gnorm_fn(yf, yb, z, g):
    return _rms((yf + yb) * _silu(z), g)


def gnorm_fwd(y2, proj, g, tr, name):
    _, T, C = y2.shape
    yf = pl.BlockSpec((None, tr, C), lambda i: (0, i, 0))
    yb = pl.BlockSpec((None, tr, C), lambda i: (1, i, 0))
    row = pl.BlockSpec((tr, C), lambda i: (i, 0))
    vec = pl.BlockSpec((1, C), lambda i: (0, 0))

    def body(yf_ref, yb_ref, z_ref, g_ref, o_ref):
        o_ref[...] = _gnorm_fn(yf_ref[...], yb_ref[...], z_ref[...], g_ref[...]).astype(BF16)

    return pl.pallas_call(body, out_shape=_sds((T, C), BF16), grid=(T // tr,), in_specs=[yf, yb, row, vec], out_specs=row,
                          compiler_params=_cp(1), name=name)(y2, y2, proj, g)


def gnorm_bwd(y2, proj, g, dyn, tr, name):
    _, T, C = y2.shape
    yf = pl.BlockSpec((None, tr, C), lambda i: (0, i, 0))
    yb = pl.BlockSpec((None, tr, C), lambda i: (1, i, 0))
    row = pl.BlockSpec((tr, C), lambda i: (i, 0))
    vec = pl.BlockSpec((1, C), lambda i: (0, 0))

    def body(yf_ref, yb_ref, z_ref, g_ref, d_ref, dy_ref, dz_ref, dg_ref):
        i = pl.program_id(0)
        _, vjp = jax.vjp(_gnorm_fn, yf_ref[...], yb_ref[...], z_ref[...], g_ref[...])
        dyf, _, dz, dg = vjp(d_ref[...].astype(F32))
        dy_ref[...] = dyf
        dz_ref[...] = dz.astype(BF16)
        _acc(dg_ref, dg, i == 0)

    return pl.pallas_call(body, out_shape=[_sds((T, C), F32), _sds((T, C), BF16), _sds((1, C), F32)], grid=(T // tr,),
                          in_specs=[yf, yb, row, vec, row], out_specs=[row, row, vec], compiler_params=_cp(1), name=name)(y2, y2, proj, g, dyn)


def _glu_fn(a):
    D = a.shape[1] // 2
    return a[:, :D] * jax.nn.sigmoid(a[:, D:])


def _ln_swish_fn(v, g, b):
    mu = jnp.mean(v, axis=-1, keepdims=True)
    xc = v - mu
    var = jnp.mean(xc * xc, axis=-1, keepdims=True)
    y = xc * lax.rsqrt(var + EPS) * g + b
    return y * jax.nn.sigmoid(y)


def _seg_pos(tr, seg, i):
    p = lax.broadcasted_iota(jnp.int32, (tr, 1), 0)
    s = jnp.where(i == 0, tr, seg)
    return p & (s - 1), s


def _dw_taps(v, w, pos, s, sign):
    tr = v.shape[0]
    K = w.shape[0]
    acc = jnp.zeros_like(v)
    for k in range(K):
        o = sign * (k - K // 2)
        q = pos + o
        ok = jnp.logical_and(q >= 0, q < s).astype(v.dtype)
        acc = acc + w[k:k + 1, :] * (pltpu.roll(v, (-o) % tr, 0) * ok)
    return acc


def _lane_blocks(v, ref):
    for c in range(v.shape[1] // 128):
        ref[c] = v[:, c * 128:(c + 1) * 128]


def _from_lane_blocks(ref):
    return jnp.concatenate([ref[c] for c in range(ref.shape[0])], axis=1)


def confmid_fwd(a, w3, b, lg, lb, seg, tr, name):
    T, D2 = a.shape
    D = D2 // 2
    nb, K, _ = w3.shape
    vec = pl.BlockSpec((1, D), lambda i: (0, 0))
    row = pl.BlockSpec((tr, D), lambda i: (i, 0))

    def body(a_ref, w_ref, b_ref, lg_ref, lb_ref, o_ref, v1_ref, s0_ref, s1_ref):
        i = pl.program_id(0)
        pos, s = _seg_pos(tr, seg, i)
        _lane_blocks(_glu_fn(a_ref[...]), s0_ref)

        def blk(c, carry):
            s1_ref[c] = _dw_taps(s0_ref[c], w_ref[c], pos, s, 1)
            return carry

        lax.fori_loop(0, nb, blk, 0)
        v1 = _from_lane_blocks(s1_ref) + b_ref[...]
        v1_ref[...] = v1
        o_ref[...] = _ln_swish_fn(v1, lg_ref[...], lb_ref[...]).astype(BF16)

    return pl.pallas_call(
        body, out_shape=[_sds((T, D), BF16), _sds((T, D), F32)], grid=(T // tr,),
        in_specs=[pl.BlockSpec((tr, D2), lambda i: (i, 0)), pl.BlockSpec((nb, K, 128), lambda i: (0, 0, 0)), vec, vec, vec],
        out_specs=[row, row], scratch_shapes=[pltpu.VMEM((nb, tr, 128), F32)] * 2, compiler_params=_cp(1), name=name)(a, w3, b, lg, lb)


def confmid_bwd(a, v1, w3, lg, lb, dv, seg, tr, name):
    T, D2 = a.shape
    D = D2 // 2
    nb, K, _ = w3.shape
    vec = pl.BlockSpec((1, D), lambda i: (0, 0))
    vec2 = pl.BlockSpec((1, D2), lambda i: (0, 0))
    wsp = pl.BlockSpec((nb, K, 128), lambda i: (0, 0, 0))
    row = pl.BlockSpec((tr, D), lambda i: (i, 0))

    def body(a_ref, v1_ref, w_ref, lg_ref, lb_ref, dv_ref, da_ref, dsum_ref, dw_ref, db_ref, dlg_ref, dlb_ref, s0_ref, s1_ref, s2_ref):
        i = pl.program_id(0)
        first = i == 0
        pos, s = _seg_pos(tr, seg, i)
        v0, glu_vjp = jax.vjp(_glu_fn, a_ref[...])
        _lane_blocks(v0, s0_ref)
        _, ln_vjp = jax.vjp(_ln_swish_fn, v1_ref[...], lg_ref[...], lb_ref[...])
        dv1, dlg, dlb = ln_vjp(dv_ref[...].astype(F32))
        _acc(db_ref, jnp.sum(dv1, axis=0, keepdims=True), first)
        _acc(dlg_ref, dlg, first)
        _acc(dlb_ref, dlb, first)
        _lane_blocks(dv1, s2_ref)

        @pl.when(first)
        def _():
            dw_ref[...] = jnp.zeros_like(dw_ref)

        def conv_t(c, carry):
            d1, v0c = s2_ref[c], s0_ref[c]
            s1_ref[c] = _dw_taps(d1, w_ref[c], pos, s, -1)
            for k in range(K):
                o = k - K // 2
                q = pos + o
                ok = jnp.logical_and(q >= 0, q < s).astype(F32)
                dw_ref[c, k:k + 1, :] += jnp.sum(d1 * (pltpu.roll(v0c, (-o) % tr, 0) * ok), axis=0, keepdims=True)
            return carry

        lax.fori_loop(0, nb, conv_t, 0)
        (da,) = glu_vjp(_from_lane_blocks(s1_ref))
        da_ref[...] = da.astype(BF16)
        _acc(dsum_ref, jnp.sum(da, axis=0, keepdims=True), first)

    return pl.pallas_call(
        body, out_shape=[_sds((T, D2), BF16), _sds((1, D2), F32), _sds((nb, K, 128), F32), _sds((1, D), F32), _sds((1, D), F32), _sds((1, D), F32)],
        grid=(T // tr,), in_specs=[pl.BlockSpec((tr, D2), lambda i: (i, 0)), row, wsp, vec, vec, row],
        out_specs=[pl.BlockSpec((tr, D2), lambda i: (i, 0)), vec2, wsp, vec, vec, vec], scratch_shapes=[pltpu.VMEM((nb, tr, 128), F32)] * 3,
        compiler_params=_cp(1), name=name)(a, v1, w3, lg, lb, dv)


def mod_fwd(rows, w, bsl, name):
    Ly, D, Nc = w.shape
    tn = _div_tile(Nc, 512)

    def body(r_ref, w_ref, b_ref, o_ref):
        s = _silu(r_ref[...]).astype(BF16)
        o_ref[...] = jnp.dot(s, w_ref[...].astype(BF16), preferred_element_type=F32) + b_ref[...]

    return pl.pallas_call(
        body, out_shape=_sds((Ly, 16, Nc), F32), grid=(Ly, Nc // tn),
        in_specs=[pl.BlockSpec((16, D), lambda l, j: (0, 0)), pl.BlockSpec((None, D, tn), lambda l, j: (l, 0, j)),
                  pl.BlockSpec((None, 1, tn), lambda l, j: (l, 0, j))],
        out_specs=pl.BlockSpec((None, 16, tn), lambda l, j: (l, 0, j)), compiler_params=_cp(2), name=name)(rows, w, bsl)


def mod_bwd(rows, w, dm, name):
    Ly, D, Nc = w.shape
    tn = _div_tile(Nc, 512)
    nj = Nc // tn

    def body(r_ref, w_ref, dm_ref, dw_ref, ds_ref):
        j = pl.program_id(1)
        s = _silu(r_ref[...]).astype(BF16)
        dmv = dm_ref[...].astype(BF16)
        dw_ref[...] = lax.dot_general(s, dmv, (((0,), (0,)), ((), ())), preferred_element_type=F32)
        _acc(ds_ref, lax.dot_general(dmv, w_ref[...].astype(BF16), (((1,), (1,)), ((), ())), preferred_element_type=F32), j == 0)

    return pl.pallas_call(
        body, out_shape=[_sds((Ly, D, Nc), F32), _sds((Ly, 16, D), F32)], grid=(Ly, nj),
        in_specs=[pl.BlockSpec((16, D), lambda l, j: (0, 0)), pl.BlockSpec((None, D, tn), lambda l, j: (l, 0, j)),
                  pl.BlockSpec((None, 16, tn), lambda l, j: (l, 0, j))],
        out_specs=[pl.BlockSpec((None, D, tn), lambda l, j: (l, 0, j)), pl.BlockSpec((None, 16, D), lambda l, j: (l, 0, 0))],
        compiler_params=_cp(2), name=name)(rows, w, dm)


def silu_grad(dsc, c, name):
    def body(d_ref, c_ref, o_ref):
        x = c_ref[...]
        s = jax.nn.sigmoid(x)
        o_ref[...] = d_ref[...] * (s * (1.0 + x * (1.0 - s)))

    return pl.pallas_call(body, out_shape=_sds(c.shape, F32), name=name)(dsc, c)


def _coords():
    return lax.axis_index("x"), lax.axis_index("y"), lax.axis_index("c")


def _flip(v, bit):
    return 1 - v if bit else v


def allgather8(x, name):
    R, C = x.shape

    def body(x_ref, o_ref, send_sems, recv_sems, local_sem):
        mx, my, mc = _coords()
        me = 4 * mx + 2 * my + mc
        mine = pltpu.make_async_copy(x_ref, o_ref.at[me], local_sem)
        mine.start()
        copies = []
        for k in range(1, 8):
            px, py, pc = _flip(mx, k & 4), _flip(my, k & 2), _flip(mc, k & 1)
            cp = pltpu.make_async_remote_copy(src_ref=x_ref, dst_ref=o_ref.at[me], send_sem=send_sems.at[k - 1],
                                              recv_sem=recv_sems.at[k - 1], device_id=(px, py, pc), device_id_type=MESH)
            cp.start()
            copies.append((cp, 4 * px + 2 * py + pc))
        for k, (cp, peer) in enumerate(copies):
            pltpu.make_async_remote_copy(src_ref=x_ref, dst_ref=o_ref.at[peer], send_sem=send_sems.at[k], recv_sem=recv_sems.at[k],
                                         device_id=(mx, my, mc), device_id_type=MESH).wait_recv()
        for cp, _ in copies:
            cp.wait_send()
        mine.wait()

    return pl.pallas_call(
        body, out_shape=_sds((8, R, C), F32), in_specs=[pl.BlockSpec(memory_space=pltpu.VMEM)],
        out_specs=pl.BlockSpec(memory_space=pltpu.VMEM),
        scratch_shapes=[pltpu.SemaphoreType.DMA((7,)), pltpu.SemaphoreType.DMA((7,)), pltpu.SemaphoreType.DMA],
        compiler_params=pltpu.CompilerParams(vmem_limit_bytes=VMEM_LIMIT), name=name)(x)


def chip_exchange(arrs, name):
    n = len(arrs)

    def src(ref, k):
        return ref.at[k]

    def body(*refs):
        ins, outs = refs[:n], refs[n:2 * n]
        send_sems, recv_sems = refs[2 * n:]
        mx, my, mc = _coords()
        me = 2 * mx + my
        sends = []
        for a in range(n):
            for k in range(1, 4):
                px, py = _flip(mx, k & 2), _flip(my, k & 1)
                cp = pltpu.make_async_remote_copy(src_ref=src(ins[a], 2 * px + py), dst_ref=outs[a].at[me], send_sem=send_sems.at[3 * a + k - 1],
                                                  recv_sem=recv_sems.at[3 * a + k - 1], device_id=(px, py, mc), device_id_type=MESH)
                cp.start()
                sends.append((cp, a, k, 2 * px + py))
        for cp, a, k, peer in sends:
            pltpu.make_async_remote_copy(src_ref=src(ins[a], me), dst_ref=outs[a].at[peer], send_sem=send_sems.at[3 * a + k - 1],
                                         recv_sem=recv_sems.at[3 * a + k - 1], device_id=(mx, my, mc), device_id_type=MESH).wait_recv()
        for cp, *_ in sends:
            cp.wait_send()

    hbm = pl.BlockSpec(memory_space=pl.ANY)
    return pl.pallas_call(
        body, out_shape=[_sds(a.shape, a.dtype) for a in arrs], in_specs=[hbm] * n, out_specs=[hbm] * n,
        scratch_shapes=[pltpu.SemaphoreType.DMA((3 * n,)), pltpu.SemaphoreType.DMA((3 * n,))], name=name)(*arrs)


def chip_allgather(arrs, name):
    n = len(arrs)

    def body(*refs):
        ins, outs, passed_outs = refs[:n], refs[n:2 * n], refs[2 * n:5 * n]
        send_sems, recv_sems, pass_send, pass_recv = refs[5 * n:]
        mx, my, mc = _coords()
        me = 2 * mx + my
        sends = []
        for a in range(n):
            for k in range(1, 4):
                px, py = _flip(mx, k & 2), _flip(my, k & 1)
                cp = pltpu.make_async_remote_copy(src_ref=ins[a].at[mc], dst_ref=outs[a].at[me, mc],
                                                  send_sem=send_sems.at[3 * a + k - 1], recv_sem=recv_sems.at[3 * a + k - 1],
                                                  device_id=(px, py, mc), device_id_type=MESH)
                cp.start()
                sends.append((cp, a, 3 * a + k - 1, 2 * px + py))
        passed = []
        for cp, a, s, peer in sends:
            got = outs[a].at[peer, mc]
            pltpu.make_async_remote_copy(src_ref=got, dst_ref=got, send_sem=send_sems.at[s], recv_sem=recv_sems.at[s],
                                         device_id=(mx, my, mc), device_id_type=MESH).wait_recv()
            fw = pltpu.make_async_remote_copy(src_ref=got, dst_ref=passed_outs[s], send_sem=pass_send.at[s], recv_sem=pass_recv.at[s],
                                              device_id=(mx, my, 1 - mc), device_id_type=MESH)
            fw.start()
            passed.append(fw)
        for cp, a, s, peer in sends:
            pltpu.make_async_remote_copy(src_ref=passed_outs[s], dst_ref=passed_outs[s], send_sem=pass_send.at[s], recv_sem=pass_recv.at[s],
                                         device_id=(mx, my, mc), device_id_type=MESH).wait_recv()
        for cp, *_ in sends:
            cp.wait_send()
        for fw in passed:
            fw.wait_send()

    hbm = pl.BlockSpec(memory_space=pl.ANY)
    sems = pltpu.SemaphoreType.DMA((3 * n,))
    res = pl.pallas_call(
        body, out_shape=[_sds((4,) + a.shape, a.dtype) for a in arrs] + [_sds(a.shape[1:], a.dtype) for a in arrs for _ in range(3)],
        in_specs=[hbm] * n, out_specs=[hbm] * (4 * n), scratch_shapes=[sems, sems, sems, sems], name=name)(*arrs)
    return res[:n], [res[n + 3 * a:n + 3 * a + 3] for a in range(n)]


def sibling_swap_halves(arrs, name):
    n = len(arrs)

    def body(*refs):
        ins, outs = refs[:n], refs[n:2 * n]
        send_sems, recv_sems = refs[2 * n:]
        mx, my, mc = _coords()
        cps = []
        for a in range(n):
            hl = arrs[a].shape[1] // 2
            cp = pltpu.make_async_remote_copy(src_ref=ins[a].at[pl.ds(0, 4), pl.ds((1 - mc) * hl, hl)], dst_ref=outs[a],
                                              send_sem=send_sems.at[a], recv_sem=recv_sems.at[a],
                                              device_id=(mx, my, 1 - mc), device_id_type=MESH)
            cp.start()
            cps.append(cp)
        for cp in cps:
            cp.wait()

    hbm = pl.BlockSpec(memory_space=pl.ANY)
    return pl.pallas_call(body, out_shape=[_sds((4, a.shape[1] // 2) + a.shape[2:], a.dtype) for a in arrs], in_specs=[hbm] * n,
                          out_specs=[hbm] * n, scratch_shapes=[pltpu.SemaphoreType.DMA((n,)), pltpu.SemaphoreType.DMA((n,))], name=name)(*arrs)


def sibling_exchange(arrs, name):
    n = len(arrs)

    def body(*refs):
        ins, outs = refs[:n], refs[n:2 * n]
        send_sems, recv_sems = refs[2 * n:]
        mx, my, mc = _coords()
        cps = []
        for a in range(n):
            cp = pltpu.make_async_remote_copy(src_ref=ins[a], dst_ref=outs[a], send_sem=send_sems.at[a], recv_sem=recv_sems.at[a],
                                              device_id=(mx, my, 1 - mc), device_id_type=MESH)
            cp.start()
            cps.append(cp)
        for cp in cps:
            cp.wait()

    hbm = pl.BlockSpec(memory_space=pl.ANY)
    return pl.pallas_call(body, out_shape=[_sds(a.shape, a.dtype) for a in arrs], in_specs=[hbm] * n, out_specs=[hbm] * n,
                          scratch_shapes=[pltpu.SemaphoreType.DMA((n,)), pltpu.SemaphoreType.DMA((n,))], name=name)(*arrs)


def add_pair(a, b, name, tr=512):
    R, C = a.shape
    tr = _div_tile(R, tr, 8)
    row = pl.BlockSpec((tr, C), lambda i: (i, 0))

    def body(a_ref, b_ref, o_ref):
        o_ref[...] = (a_ref[...].astype(F32) + b_ref[...].astype(F32)).astype(BF16)

    return pl.pallas_call(body, out_shape=_sds((R, C), BF16), grid=(R // tr,), in_specs=[row, row], out_specs=row,
                          compiler_params=_cp(1), name=name)(a, b)


def sum_slabs(x, name, tr=256):
    n, R, C = x.shape
    tr = _div_tile(R, tr, 8)

    def body(x_ref, o_ref):
        acc = x_ref[0].astype(F32)
        for k in range(1, n):
            acc = acc + x_ref[k].astype(F32)
        o_ref[...] = acc

    return pl.pallas_call(body, out_shape=_sds((R, C), F32), grid=(R // tr,), in_specs=[pl.BlockSpec((n, tr, C), lambda i: (0, i, 0))],
                          out_specs=pl.BlockSpec((tr, C), lambda i: (i, 0)), compiler_params=_cp(1), name=name)(x)


def adamw(w, g, m, v, name, tr=256):
    R, C = w.shape
    split = isinstance(g, tuple)
    nh = 2 if split else 1
    tr = _div_tile(R // nh, tr, 8)
    nt = R // nh // tr
    row = pl.BlockSpec((tr, C), lambda h, i: (h * nt + i, 0))
    part = pl.BlockSpec((tr, C), lambda h, i: (i, 0))
    ins = [w] + (list(g) if split else [g]) + [m, v]

    def body(*refs):
        w_ref = refs[0]
        if split:
            mine = pl.program_id(0) == lax.axis_index("c")
            g = jnp.where(mine, refs[1][...], refs[2][...])
        else:
            g = refs[1][...]
        m_ref, v_ref, go_ref, d_ref, mo_ref, vo_ref = refs[nh + 1:]
        mn = ADAM_B1 * m_ref[...] + (1.0 - ADAM_B1) * g
        vn = ADAM_B2 * v_ref[...] + (1.0 - ADAM_B2) * (g * g)
        m_hat = mn / (1.0 - ADAM_B1 ** ADAM_STEP)
        v_hat = vn / (1.0 - ADAM_B2 ** ADAM_STEP)
        go_ref[...] = g
        d_ref[...] = -ADAM_LR * (m_hat / (jnp.sqrt(v_hat) + ADAM_EPS) + ADAM_WD * w_ref[...])
        mo_ref[...] = mn
        vo_ref[...] = vn

    return pl.pallas_call(body, out_shape=[_sds((R, C), F32)] * 4, grid=(nh, nt), in_specs=[row] + [part] * nh + [row, row],
                          out_specs=[row] * 4, compiler_params=_cp(2), name=name)(*ins)


W_NAMES = ("c_ctx", "mod_w", "mod_b", "pre_mix_g", "post_mix_g", "pre_mlp_g", "post_mlp_g", "mlp_w1", "mlp_w2", "ssm_in_w",
           "ssm_conv_w", "ssm_conv_b", "ssm_a_log_f", "ssm_dt_bias_f", "ssm_d_f", "ssm_a_log_b", "ssm_dt_bias_b", "ssm_d_b",
           "ssm_norm_g", "ssm_out_w", "conf_pw1_w", "conf_pw1_b", "conf_dw_w", "conf_dw_b", "conf_ln_g", "conf_ln_b",
           "conf_pw2_w", "conf_pw2_b")
BIG = {"mlp_w1": "col", "mlp_w2": "row", "ssm_in_w": "col", "ssm_out_w": "row", "conf_pw1_w": "col", "conf_pw2_w": "row"}
SMALL_SHARDED = ("ssm_conv_w", "conf_pw1_b", "conf_dw_w", "conf_dw_b", "conf_ln_g", "conf_ln_b", "conf_pw2_b")
PACK_W = 1024


def _pack(arrs):
    flat = jnp.concatenate([a.reshape(-1).astype(F32) for a in arrs])
    n = flat.shape[0]
    tot = -(-n // (8 * PACK_W)) * (8 * PACK_W)
    return jnp.pad(flat, (0, tot - n)).reshape(tot // PACK_W, PACK_W)


def _unpack(buf, shapes):
    lead = buf.shape[:-2]
    flat = buf.reshape(lead + (-1,))
    out, off = [], 0
    for shp in shapes:
        n = 1
        for d in shp:
            n *= d
        out.append(flat[..., off:off + n].reshape(lead + tuple(shp)))
        off += n
    return out


def _full_from_chips(g, kind):
    if kind == "col":
        return jnp.moveaxis(g, 0, -2).reshape(g.shape[1:-1] + (4 * g.shape[-1],))
    return jnp.moveaxis(g, 0, 1).reshape((g.shape[1], 4 * g.shape[2]) + g.shape[3:])


def _chip_slabs(full, kind):
    if kind == "col":
        return jnp.moveaxis(full.reshape(full.shape[:-1] + (4, full.shape[-1] // 4)), -2, 0)
    return jnp.moveaxis(full.reshape((full.shape[0], 4, full.shape[1] // 4) + full.shape[2:]), 1, 0)


def _view2d(a):
    if a.ndim == 1:
        return a.reshape(1, -1)
    return a.reshape(-1, a.shape[-1])


def kernel(x, c, ctx, c_ctx, mod_w, mod_b, pre_mix_g, post_mix_g, pre_mlp_g, post_mlp_g, mlp_w1, mlp_w2, ssm_in_w, ssm_conv_w, ssm_conv_b, ssm_a_log_f, ssm_dt_bias_f, ssm_d_f, ssm_a_log_b, ssm_dt_bias_b, ssm_d_b, ssm_norm_g, ssm_out_w, conf_pw1_w, conf_pw1_b, conf_dw_w, conf_dw_b, conf_ln_g, conf_ln_b, conf_pw2_w, conf_pw2_b, loss_target, m_c_ctx, m_mod_w, m_mod_b, m_pre_mix_g, m_post_mix_g, m_pre_mlp_g, m_post_mlp_g, m_mlp_w1, m_mlp_w2, m_ssm_in_w, m_ssm_conv_w, m_ssm_conv_b, m_ssm_a_log_f, m_ssm_dt_bias_f, m_ssm_d_f, m_ssm_a_log_b, m_ssm_dt_bias_b, m_ssm_d_b, m_ssm_norm_g, m_ssm_out_w, m_conf_pw1_w, m_conf_pw1_b, m_conf_dw_w, m_conf_dw_b, m_conf_ln_g, m_conf_ln_b, m_conf_pw2_w, m_conf_pw2_b, v_c_ctx, v_mod_w, v_mod_b, v_pre_mix_g, v_post_mix_g, v_pre_mlp_g, v_post_mlp_g, v_mlp_w1, v_mlp_w2, v_ssm_in_w, v_ssm_conv_w, v_ssm_conv_b, v_ssm_a_log_f, v_ssm_dt_bias_f, v_ssm_d_f, v_ssm_a_log_b, v_ssm_dt_bias_b, v_ssm_d_b, v_ssm_norm_g, v_ssm_out_w, v_conf_pw1_w, v_conf_pw1_b, v_conf_dw_w, v_conf_dw_b, v_conf_ln_g, v_conf_ln_b, v_conf_pw2_w, v_conf_pw2_b):
    given = dict(locals())
    W = {n: given[n] for n in W_NAMES}
    L, D = x.shape[1], x.shape[2]
    Lc = ctx.shape[1]
    T = Lc + L
    depth = mod_w.shape[0]
    d_inner = ssm_norm_g.shape[1]
    H = ssm_a_log_f.shape[1]
    xbc = ssm_conv_b.shape[1]
    GN = (xbc - d_inner) // 2
    G = GN // N_STATE
    rows_grid = L // GRID_W
    tr = Lc
    ncc = Lc // CHUNK
    assert H == 8 * G and Lc % CHUNK == 0 and L % Lc == 0 and tr % GRID_W == 0 and tr % rows_grid == 0
    tc = _div_tile(GN, 512)
    assert d_inner % tc == 0
    mx, my, mc = _coords()
    chip = 2 * mx + my
    dev = 4 * mx + 2 * my + mc

    small_shapes = [(1, D)] + [W[n].shape for n in SMALL_SHARDED]
    got = allgather8(_pack([c] + [W[n] for n in SMALL_SHARDED]), "gather_small")
    parts = _unpack(got, small_shapes)
    c_all = parts[0].reshape(8, D)
    full_small = {n: jnp.concatenate([p[2 * k] for k in range(4)], axis=-1) for n, p in zip(SMALL_SHARDED, parts[1:])}

    cond = jnp.concatenate([c_all, c_ctx.reshape(1, D), jnp.zeros((7, D), F32)], axis=0)
    ncol = mod_w.shape[2]
    bsl = lax.dynamic_slice(mod_b, (0, chip * ncol), (depth, ncol)).reshape(depth, 1, ncol)
    m_loc = mod_fwd(cond, mod_w, bsl, "mod_fwd")
    m_all = allgather8(m_loc.reshape(depth * 16, ncol), "gather_mod").reshape(8, depth, 16, ncol)
    m_full = jnp.concatenate([m_all[2 * k] for k in range(4)], axis=-1)
    m_lat = lax.dynamic_slice(m_full, (0, dev, 0), (depth, 1, 6 * D))
    m2 = jnp.concatenate([m_full[:, 8:9], m_lat], axis=1)

    def six(i):
        return [m2[i, :, k * D:(k + 1) * D].reshape(2, 1, D) for k in range(6)]

    big_names = list(BIG)
    shards = [W[n].astype(BF16).reshape((2, W[n].shape[0] // 2) + W[n].shape[1:]) for n in big_names]
    gathered, passed_on = chip_allgather(shards, "gather_weights")
    Wb = {}
    for n, own, g, ps in zip(big_names, shards, gathered, passed_on):
        g = lax.dynamic_update_slice(g, own[None], (chip,) + (0,) * own.ndim)
        for k in (1, 2, 3):
            at = (2 * _flip(mx, k & 2) + _flip(my, k & 1), 1 - mc) + (0,) * (g.ndim - 2)
            g = lax.dynamic_update_slice(g, ps[k - 1][None, None], at)
        g = g.reshape((4,) + W[n].shape)
        Wb[n] = [_full_from_chips(g[:, j:j + 1], BIG[n])[0] for j in range(W[n].shape[0])]

    def to_scan(u):
        lat = u[Lc:].reshape(rows_grid, GRID_W, u.shape[1]).swapaxes(0, 1).reshape(L, u.shape[1])
        return jnp.concatenate([u[:Lc], lat], axis=0)

    def from_scan(u):
        lat = u[Lc:].reshape(GRID_W, rows_grid, u.shape[1]).swapaxes(0, 1).reshape(L, u.shape[1])
        return jnp.concatenate([u[:Lc], lat], axis=0)

    def ssm_params(j):
        def two(f, b):
            return jnp.stack([f[j], b[j]]).reshape(2, G, 1, 8)
        return two(ssm_dt_bias_f, ssm_dt_bias_b), two(ssm_a_log_f, ssm_a_log_b), two(ssm_d_f, ssm_d_b)

    def dw3_of(j):
        w = full_small["conf_dw_w"][j]
        return w.reshape(w.shape[0], D // 128, 128).swapaxes(0, 1)

    h = jnp.concatenate([ctx[0], x[0]], axis=0)
    saved = []
    for i in range(depth):
        kind, j = i % 2, i // 2
        col_major = (j % 2) == 1
        sh1, sc1, g1, sh2, sc2, g2 = six(i)
        s = {"h": h}
        u = prenorm_fwd(h, pre_mix_g[i][None], sh1, sc1, tr, f"prenorm_mix{i}")
        if col_major:
            u = to_scan(u)
        s["u"] = u
        if kind == 0:
            proj = mm(u, Wb["ssm_in_w"][j], "nn", F32, name=f"ssm_in{i}")
            pre, act = conv5_fwd(proj, full_small["ssm_conv_w"][j], ssm_conv_b[j][None], d_inner // tc, tr, tc, f"ssm_conv{i}")
            dtr = proj[:, d_inner + xbc:].reshape(T, 2, G, 8).transpose(1, 2, 0, 3)
            bias, alog, dsk = ssm_params(j)
            y2, hsave = ssd_fwd(act, dtr, bias, alog, dsk, d_inner, ncc, f"ssd_fwd{i}")
            yn = gnorm_fwd(y2, proj, ssm_norm_g[j][None], CHUNK, f"ssm_gnorm{i}")
            out = mm(yn, Wb["ssm_out_w"][j], "nn", F32, name=f"ssm_out{i}")
            s.update(proj=proj, pre=pre, act=act, dtr=dtr, y2=y2, hsave=hsave, yn=yn)
        else:
            seg = rows_grid if col_major else GRID_W
            a = mm(u, Wb["conf_pw1_w"][j], "nn", F32, bias=full_small["conf_pw1_b"][j][None], name=f"conf_pw1_{i}")
            v, v1 = confmid_fwd(a, dw3_of(j), full_small["conf_dw_b"][j][None], full_small["conf_ln_g"][j][None],
                                full_small["conf_ln_b"][j][None], seg, tr, f"conf_mid{i}")
            out = mm(v, Wb["conf_pw2_w"][j], "nn", F32, bias=full_small["conf_pw2_b"][j][None], name=f"conf_pw2_{i}")
            s.update(a=a, v=v, v1=v1, seg=seg)
        if col_major:
            out = from_scan(out)
        h1 = post_fwd(h, out, post_mix_g[i][None], g1, tr, f"post_mix{i}")
        u2 = prenorm_fwd(h1, pre_mlp_g[i][None], sh2, sc2, tr, f"prenorm_mlp{i}")
        hid, actm = mm(u2, Wb["mlp_w1"][i], "nn", F32, relu2=True, name=f"mlp_up{i}")
        f = mm(actm, Wb["mlp_w2"][i], "nn", F32, name=f"mlp_down{i}")
        h = post_fwd(h1, f, post_mlp_g[i][None], g2, tr, f"post_mlp{i}")
        s.update(out=out, h1=h1, u2=u2, hid=hid, actm=actm, f=f)
        saved.append(s)

    loss_blk, Gr = loss_head(h, loss_target[0], tr, "loss_head")
    loss = lax.psum(loss_blk[0, 0], ("x", "y", "c"))

    gb = {n: [None] * W[n].shape[0] for n in BIG}
    gs = {n: [None] * W[n].shape[0] for n in W_NAMES if n not in BIG and n not in ("c_ctx", "mod_w", "mod_b")}
    dmod = [None] * depth
    for i in reversed(range(depth)):
        kind, j = i % 2, i // 2
        col_major = (j % 2) == 1
        sh1, sc1, g1, sh2, sc2, g2 = six(i)
        s = saved[i]
        df, gs["post_mlp_g"][i], dg2, _ = post_bwd(s["f"], post_mlp_g[i][None], g2, Gr, tr, f"post_mlp_bwd{i}")
        gb["mlp_w2"][i] = mm(s["actm"], df, "tn", BF16, name=f"mlp_down_wg{i}")
        dhid = mm(df, Wb["mlp_w2"][i], "nt", BF16, mul_relu=s["hid"], name=f"mlp_down_dg{i}")
        gb["mlp_w1"][i] = mm(s["u2"], dhid, "tn", BF16, name=f"mlp_up_wg{i}")
        du2 = mm(dhid, Wb["mlp_w1"][i], "nt", F32, name=f"mlp_up_dg{i}")
        Gr, gs["pre_mlp_g"][i], dsh2, dsc2 = prenorm_bwd(s["h1"], pre_mlp_g[i][None], sh2, sc2, du2, Gr, tr, f"prenorm_mlp_bwd{i}")
        dout, gs["post_mix_g"][i], dg1, dout_sum = post_bwd(s["out"], post_mix_g[i][None], g1, Gr, tr, f"post_mix_bwd{i}")
        if col_major:
            dout = to_scan(dout)
        if kind == 0:
            gb["ssm_out_w"][j] = mm(s["yn"], dout, "tn", BF16, name=f"ssm_out_wg{i}")
            dyn = mm(dout, Wb["ssm_out_w"][j], "nt", F32, name=f"ssm_out_dg{i}")
            dys, dz, gs["ssm_norm_g"][j] = gnorm_bwd(s["y2"], s["proj"], ssm_norm_g[j][None], dyn, CHUNK, f"ssm_gnorm_bwd{i}")
            bias, alog, dsk = ssm_params(j)
            dx2, db2, dc2, ddtr, dbias, dalog, ddsk = ssd_bwd(s["act"], s["dtr"], bias, alog, dsk, s["hsave"], dys, d_inner, ncc, f"ssd_bwd{i}")
            cw = full_small["ssm_conv_w"][j]
            nx, nb_ = d_inner // tc, GN // tc
            dxx, dwx, dbx = conv5_bwd(dx2, s["pre"], s["proj"], cw, 0, nx, tr, tc, f"ssm_conv_bwd_x{i}")
            dxb, dwb, dbb = conv5_bwd(db2, s["pre"], s["proj"], cw, nx, 2 * nx, tr, tc, f"ssm_conv_bwd_b{i}")
            dxc, dwc, dbc = conv5_bwd(dc2, s["pre"], s["proj"], cw, nx + nb_, 2 * nx + nb_, tr, tc, f"ssm_conv_bwd_c{i}")
            gs["ssm_conv_w"][j] = jnp.concatenate([dwx, dwb, dwc], axis=1)
            gs["ssm_conv_b"][j] = jnp.concatenate([dbx, dbb, dbc], axis=1)[0]
            ddt = ddtr.transpose(2, 0, 1, 3).reshape(T, 2 * H).astype(BF16)
            dproj = jnp.concatenate([dz, dxx, dxb, dxc, ddt], axis=1)
            gb["ssm_in_w"][j] = mm(s["u"], dproj, "tn", BF16, name=f"ssm_in_wg{i}")
            du = mm(dproj, Wb["ssm_in_w"][j], "nt", F32, name=f"ssm_in_dg{i}")
            for nm, val in (("ssm_dt_bias", dbias), ("ssm_a_log", dalog), ("ssm_d", ddsk)):
                gs[nm + "_f"][j] = val[0].reshape(H)
                gs[nm + "_b"][j] = val[1].reshape(H)
        else:
            gb["conf_pw2_w"][j] = mm(s["v"], dout, "tn", BF16, name=f"conf_pw2_wg{i}")
            gs["conf_pw2_b"][j] = dout_sum[0]
            dv = mm(dout, Wb["conf_pw2_w"][j], "nt", F32, name=f"conf_pw2_dg{i}")
            da, da_sum, dw3, ddb, dlg, dlb = confmid_bwd(s["a"], s["v1"], dw3_of(j), full_small["conf_ln_g"][j][None],
                                                          full_small["conf_ln_b"][j][None], dv, s["seg"], tr, f"conf_mid_bwd{i}")
            gs["conf_pw1_b"][j] = da_sum[0]
            gs["conf_dw_w"][j] = dw3.swapaxes(0, 1).reshape(dw3.shape[1], D)
            gs["conf_dw_b"][j], gs["conf_ln_g"][j], gs["conf_ln_b"][j] = ddb[0], dlg[0], dlb[0]
            gb["conf_pw1_w"][j] = mm(s["u"], da, "tn", BF16, name=f"conf_pw1_wg{i}")
            du = mm(da, Wb["conf_pw1_w"][j], "nt", F32, name=f"conf_pw1_dg{i}")
        if col_major:
            du = from_scan(du)
        Gr, gs["pre_mix_g"][i], dsh1, dsc1 = prenorm_bwd(s["h"], pre_mix_g[i][None], sh1, sc1, du, Gr, tr, f"prenorm_mix_bwd{i}")
        dmod[i] = jnp.concatenate([t.reshape(2, D) for t in (dsh1, dsc1, dg1, dsh2, dsc2, dg2)], axis=1)
    grad_x = Gr[Lc:][None]

    small_names = list(gs)
    small_local = [jnp.stack([t.reshape(W[n].shape[1:] if n not in SMALL_SHARDED else t.shape) for t in gs[n]]) for n in small_names]
    small_shapes = [t.shape for t in small_local] + [(depth, 2, 6 * D)]
    got = allgather8(_pack(small_local + [jnp.stack(dmod)]), "gather_small_grads")
    summed = _unpack(sum_slabs(got, "sum_small_grads"), small_shapes)
    grads = {}
    for n, t in zip(small_names, summed[:-1]):
        if n in SMALL_SHARDED:
            w = W[n].shape[-1]
            t = lax.dynamic_slice_in_dim(t, chip * w, w, axis=t.ndim - 1)
        grads[n] = t
    grads["mod_b"] = summed[-1][:, 0] + summed[-1][:, 1]
    dm_all = _unpack(got, small_shapes)[-1]
    dm_ctx = sum_slabs(dm_all[:, :, 0], "sum_dmod_ctx")
    dm_rows = jnp.concatenate([dm_all[:, :, 1].swapaxes(0, 1), dm_ctx[:, None], jnp.zeros((depth, 7, 6 * D), F32)], axis=1)
    dm_mine = lax.dynamic_slice_in_dim(dm_rows, chip * ncol, ncol, axis=2)
    grads["mod_w"], dcond = mod_bwd(cond, mod_w, dm_mine, "mod_bwd")
    dcc = sum_slabs(dcond[:, 8:9], "sum_dcond_layers")
    dcc_all = allgather8(jnp.pad(dcc, ((0, 7), (0, 0))), "gather_dcond")
    dcc_sum = sum_slabs(dcc_all[0::2, 0:1], "sum_dcond_chips")
    grads["c_ctx"] = silu_grad(dcc_sum, c_ctx.reshape(1, D), "c_ctx_grad").reshape(D)

    slabs = [_chip_slabs(jnp.stack(gb[n]), BIG[n]) for n in big_names]
    theirs = sibling_swap_halves(slabs, "sibling_swap_grads")
    chip_part = []
    for n, s, t in zip(big_names, slabs, theirs):
        hl = t.shape[1]
        own = lax.dynamic_slice_in_dim(s, mc * hl, hl, axis=1)
        chip_part.append(add_pair(_view2d(own), _view2d(t), f"add_cores_{n}").reshape(t.shape))
    recv = chip_exchange(chip_part, "scatter_grads")
    recv = [lax.dynamic_update_slice(r, lax.dynamic_slice_in_dim(p, chip, 1, axis=0), (chip,) + (0,) * (r.ndim - 1))
            for r, p in zip(recv, chip_part)]
    half = [sum_slabs(r.reshape((4, -1, r.shape[-1])), f"sum_grads_{n}") for n, r in zip(big_names, recv)]
    for n, mine, theirs in zip(big_names, half, sibling_exchange(half, "sibling_grads")):
        grads[n] = (mine, theirs)

    res = {}
    for n in W_NAMES:
        w2 = _view2d(W[n])
        cols = w2.shape[1]
        g = grads[n] if isinstance(grads[n], tuple) else _view2d(grads[n])
        outs = adamw(w2, g, _view2d(given["m_" + n]), _view2d(given["v_" + n]), f"adamw_{n}", tr=max(8, (262144 // cols) // 8 * 8))
        res[n] = [o.reshape(W[n].shape) for o in outs]
    return (loss, grad_x, *[res[n][0] for n in W_NAMES], *[res[n][1] for n in W_NAMES], *[res[n][2] for n in W_NAMES],
            *[res[n][3] for n in W_NAMES])
```

```python
import functools

import jax
import jax.numpy as jnp
from jax import lax
from jax.experimental import pallas as pl
from jax.experimental.pallas import tpu as pltpu

GRID_W = 64
CHUNK = 128
N_STATE = 128
EPS = 1e-6
ADAM_LR, ADAM_B1, ADAM_B2, ADAM_EPS, ADAM_WD, ADAM_STEP = 0.001, 0.9, 0.999, 1e-08, 0.01, 10
VMEM_LIMIT = 56 * 1024 * 1024
MM_OPERAND_BYTES = 44 * 1024 * 1024
F32, BF16 = jnp.float32, jnp.bfloat16
HI = lax.Precision.HIGHEST
SPREAD = lax.Precision.HIGH
MESH = pl.DeviceIdType.MESH


def _cp(n_grid):
    return pltpu.CompilerParams(dimension_semantics=("arbitrary",) * n_grid, vmem_limit_bytes=VMEM_LIMIT)


def _sds(shape, dtype):
    return jax.ShapeDtypeStruct(tuple(shape), dtype)


def _div_tile(n, target, unit=128):
    best = None
    t = unit
    while t <= min(n, target):
        if n % t == 0:
            best = t
        t += unit
    return best if best is not None else n


def _rms(x, g):
    return x * lax.rsqrt(jnp.mean(x * x, axis=-1, keepdims=True) + EPS) * g


def _silu(x):
    return x * jax.nn.sigmoid(x)


def mm(a, b, mode, out_dtype, *, bias=None, relu2=False, mul_relu=None, name, tm=768, tn=1152, tk=2048):
    if mode == "nn":
        (M, C), (_, N) = a.shape, b.shape
    elif mode == "nt":
        (M, C), (N, _) = a.shape, b.shape
    else:
        (C, M), (_, N) = a.shape, b.shape
    if mode == "tn":
        tm, tn, tk = _div_tile(M, 1024), _div_tile(N, tn), _div_tile(C, 4224, 8)
    else:
        tm = _div_tile(M, tm, 8)
        if C > tk and C % 128 == 0 and 4 * C * (tm + 512) <= MM_OPERAND_BYTES:
            tn, tk = _div_tile(N, 512), C
        else:
            tn, tk = _div_tile(N, tn), _div_tile(C, 2 * tk if C > tk else tk)
    nk = C // tk
    a_spec = {"nn": pl.BlockSpec((tm, tk), lambda i, j, k: (i, k)), "nt": pl.BlockSpec((tm, tk), lambda i, j, k: (i, k)),
              "tn": pl.BlockSpec((tk, tm), lambda i, j, k: (k, i))}[mode]
    b_spec = {"nn": pl.BlockSpec((tk, tn), lambda i, j, k: (k, j)), "nt": pl.BlockSpec((tn, tk), lambda i, j, k: (j, k)),
              "tn": pl.BlockSpec((tk, tn), lambda i, j, k: (k, j))}[mode]
    dims = {"nn": (((1,), (0,)), ((), ())), "nt": (((1,), (1,)), ((), ())), "tn": (((0,), (0,)), ((), ()))}[mode]
    ins, specs = [a, b], [a_spec, b_spec]
    if bias is not None:
        ins.append(bias)
        specs.append(pl.BlockSpec((1, tn), lambda i, j, k: (0, j)))
    if mul_relu is not None:
        ins.append(mul_relu)
        specs.append(pl.BlockSpec((tm, tn), lambda i, j, k: (i, j)))
    o_spec = pl.BlockSpec((tm, tn), lambda i, j, k: (i, j))
    outs, out_specs = [_sds((M, N), out_dtype)], [o_spec]
    if relu2:
        outs.append(_sds((M, N), BF16))
        out_specs.append(o_spec)

    def body(*refs):
        a_ref, b_ref = refs[0], refs[1]
        pos = 2
        bias_ref = mr_ref = None
        if bias is not None:
            bias_ref = refs[pos]
            pos += 1
        if mul_relu is not None:
            mr_ref = refs[pos]
            pos += 1
        o_ref = refs[pos]
        o2_ref = refs[pos + 1] if relu2 else None
        part = lax.dot_general(a_ref[...].astype(BF16), b_ref[...].astype(BF16), dims, preferred_element_type=F32)

        def finish(r):
            if bias_ref is not None:
                r = r + bias_ref[...]
            if mr_ref is not None:
                r = r * (2.0 * jnp.maximum(mr_ref[...], 0.0))
            o_ref[...] = r.astype(o_ref.dtype)
            if o2_ref is not None:
                q = jnp.maximum(r, 0.0)
                o2_ref[...] = (q * q).astype(BF16)

        if nk == 1:
            finish(part)
        else:
            acc_ref = refs[-1]
            k = pl.program_id(2)

            @pl.when(k == 0)
            def _():
                acc_ref[...] = part

            @pl.when(jnp.logical_and(k > 0, k < nk - 1))
            def _():
                acc_ref[...] += part

            @pl.when(k == nk - 1)
            def _():
                finish(acc_ref[...] + part)

    res = pl.pallas_call(body, out_shape=outs, grid=(M // tm, N // tn, nk), in_specs=specs, out_specs=out_specs,
                         scratch_shapes=[pltpu.VMEM((tm, tn), F32)] if nk > 1 else [], compiler_params=_cp(3), name=name)(*ins)
    return res if relu2 else res[0]


def _seg_spec(D):
    return pl.BlockSpec((None, 1, D), lambda i: (jnp.minimum(i, 1), 0, 0))


def _prenorm_fn(h, g, sh, sc):
    return _rms(h, g) * (1.0 + sc) + sh


def prenorm_fwd(h, g, sh, sc, tr, name):
    T, D = h.shape
    row = pl.BlockSpec((tr, D), lambda i: (i, 0))
    vec = pl.BlockSpec((1, D), lambda i: (0, 0))

    def body(h_ref, g_ref, sh_ref, sc_ref, u_ref):
        u_ref[...] = _prenorm_fn(h_ref[...], g_ref[...], sh_ref[...], sc_ref[...]).astype(BF16)

    return pl.pallas_call(body, out_shape=_sds((T, D), BF16), grid=(T // tr,), in_specs=[row, vec, _seg_spec(D), _seg_spec(D)],
                          out_specs=row, compiler_params=_cp(1), name=name)(h, g, sh, sc)


def _acc(ref, val, first):
    @pl.when(first)
    def _():
        ref[...] = val

    @pl.when(jnp.logical_not(first))
    def _():
        ref[...] += val


def prenorm_bwd(h, g, sh, sc, du, G, tr, name):
    T, D = h.shape
    row = pl.BlockSpec((tr, D), lambda i: (i, 0))
    vec = pl.BlockSpec((1, D), lambda i: (0, 0))

    def body(h_ref, g_ref, sh_ref, sc_ref, du_ref, G_ref, Go_ref, dg_ref, dsh_ref, dsc_ref):
        i = pl.program_id(0)
        _, vjp = jax.vjp(_prenorm_fn, h_ref[...], g_ref[...], sh_ref[...], sc_ref[...])
        dh, dg, dsh, dsc = vjp(du_ref[...].astype(F32))
        Go_ref[...] = G_ref[...] + dh
        _acc(dg_ref, dg, i == 0)
        _acc(dsh_ref, dsh, i <= 1)
        _acc(dsc_ref, dsc, i <= 1)

    return pl.pallas_call(
        body, out_shape=[_sds((T, D), F32), _sds((1, D), F32), _sds((2, 1, D), F32), _sds((2, 1, D), F32)], grid=(T // tr,),
        in_specs=[row, vec, _seg_spec(D), _seg_spec(D), row, row], out_specs=[row, vec, _seg_spec(D), _seg_spec(D)],
        compiler_params=_cp(1), name=name)(h, g, sh, sc, du, G)


def _post_fn(y, gp, gate):
    return gate * _rms(y, gp)


def post_fwd(h, y, gp, gate, tr, name):
    T, D = h.shape
    row = pl.BlockSpec((tr, D), lambda i: (i, 0))
    vec = pl.BlockSpec((1, D), lambda i: (0, 0))

    def body(h_ref, y_ref, gp_ref, gate_ref, o_ref):
        o_ref[...] = h_ref[...] + _post_fn(y_ref[...], gp_ref[...], gate_ref[...])

    return pl.pallas_call(body, out_shape=_sds((T, D), F32), grid=(T // tr,), in_specs=[row, row, vec, _seg_spec(D)],
                          out_specs=row, compiler_params=_cp(1), name=name)(h, y, gp, gate)


def post_bwd(y, gp, gate, G, tr, name):
    T, D = y.shape
    row = pl.BlockSpec((tr, D), lambda i: (i, 0))
    vec = pl.BlockSpec((1, D), lambda i: (0, 0))

    def body(y_ref, gp_ref, gate_ref, G_ref, dy_ref, dgp_ref, dgate_ref, dsum_ref):
        i = pl.program_id(0)
        _, vjp = jax.vjp(_post_fn, y_ref[...], gp_ref[...], gate_ref[...])
        dy, dgp, dgate = vjp(G_ref[...])
        dy_ref[...] = dy.astype(BF16)
        _acc(dgp_ref, dgp, i == 0)
        _acc(dgate_ref, dgate, i <= 1)
        _acc(dsum_ref, jnp.sum(dy, axis=0, keepdims=True), i == 0)

    return pl.pallas_call(
        body, out_shape=[_sds((T, D), BF16), _sds((1, D), F32), _sds((2, 1, D), F32), _sds((1, D), F32)], grid=(T // tr,),
        in_specs=[row, vec, _seg_spec(D), row], out_specs=[row, vec, _seg_spec(D), vec], compiler_params=_cp(1), name=name)(y, gp, gate, G)


def loss_head(h, target, tr, name):
    T, D = h.shape
    row = pl.BlockSpec((tr, D), lambda i: (i, 0))
    trow = pl.BlockSpec((tr, D), lambda i: (jnp.maximum(i - 1, 0), 0))

    def body(h_ref, t_ref, loss_ref, G_ref):
        i = pl.program_id(0)

        @pl.when(i == 0)
        def _():
            loss_ref[...] = jnp.zeros_like(loss_ref)
            G_ref[...] = jnp.zeros_like(G_ref)

        @pl.when(i > 0)
        def _():
            e = h_ref[...] - t_ref[...]
            G_ref[...] = e * (1.0 / D)
            loss_ref[...] += jnp.sum(e * e) * (0.5 / D)

    return pl.pallas_call(body, out_shape=[_sds((8, 128), F32), _sds((T, D), F32)], grid=(T // tr,), in_specs=[row, trow],
                          out_specs=[pl.BlockSpec((8, 128), lambda i: (0, 0)), row], compiler_params=_cp(1), name=name)(h, target)


def _halo_specs(tr, tc, T, col0, lead=()):
    n8 = tr // 8
    nl = len(lead)
    cur = pl.BlockSpec(lead + (tr, tc), lambda j, i: (0,) * nl + (i, col0 + j))
    prev = pl.BlockSpec(lead + (8, tc), lambda j, i: (0,) * nl + (jnp.maximum(i * n8 - 1, 0), col0 + j))
    nxt = pl.BlockSpec(lead + (8, tc), lambda j, i: (0,) * nl + (jnp.minimum((i + 1) * n8, T // 8 - 1), col0 + j))
    return [cur, prev, nxt]


def _with_halo(cur, prev, nxt, i, nt):
    keep_prev = (i >= 2).astype(cur.dtype)
    keep_next = jnp.logical_and(i >= 1, i < nt - 1).astype(cur.dtype)
    return jnp.concatenate([prev * keep_prev, cur, nxt * keep_next], axis=0)


def _shift_rows(ext, o, tr):
    n = ext.shape[0]
    return pltpu.roll(ext, (-o) % n, 0)[8:8 + tr]


def conv5_fwd(proj, w, b, col0, tr, tc, name):
    T = proj.shape[0]
    K, C = w.shape
    nt = T // tr

    def body(x_ref, xp_ref, xn_ref, w_ref, b_ref, pre_ref, act_ref):
        i = pl.program_id(1)
        ext = _with_halo(x_ref[...], xp_ref[...], xn_ref[...], i, nt)
        wv = w_ref[...]
        acc = jnp.zeros((tr, tc), F32) + b_ref[...]
        for k in range(K):
            acc = acc + wv[k:k + 1, :] * _shift_rows(ext, k - K // 2, tr)
        pre_ref[...] = acc
        act_ref[...] = _silu(acc)

    out = pl.BlockSpec((tr, tc), lambda j, i: (i, j))
    return pl.pallas_call(
        body, out_shape=[_sds((T, C), F32), _sds((T, C), F32)], grid=(C // tc, nt),
        in_specs=_halo_specs(tr, tc, T, col0) + [pl.BlockSpec((K, tc), lambda j, i: (0, j)), pl.BlockSpec((1, tc), lambda j, i: (0, j))],
        out_specs=[out, out], compiler_params=_cp(2), name=name)(proj, proj, proj, w, b)


def conv5_bwd(dact, pre, proj, w, colp, colx, tr, tc, name):
    _, T, Cp = dact.shape
    K = w.shape[0]
    nt = T // tr

    def body(d_ref, dp_ref, dn_ref, p_ref, pp_ref, pn_ref, x_ref, xp_ref, xn_ref, w_ref, dx_ref, dw_ref, db_ref):
        i = pl.program_id(1)

        def dpre_of(d, p):
            s = jax.nn.sigmoid(p)
            return (d[0] + d[1]) * (s * (1.0 + p * (1.0 - s)))

        dext = _with_halo(dpre_of(d_ref[...], p_ref[...]), dpre_of(dp_ref[...], pp_ref[...]), dpre_of(dn_ref[...], pn_ref[...]), i, nt)
        xext = _with_halo(x_ref[...], xp_ref[...], xn_ref[...], i, nt)
        dcur = dext[8:8 + tr]
        wv = w_ref[...]
        dx = jnp.zeros((tr, tc), F32)
        for k in range(K):
            o = k - K // 2
            dx = dx + wv[k:k + 1, :] * _shift_rows(dext, -o, tr)
            _acc(dw_ref.at[k:k + 1, :], jnp.sum(dcur * _shift_rows(xext, o, tr), axis=0, keepdims=True), i == 0)
        dx_ref[...] = dx.astype(BF16)
        _acc(db_ref, jnp.sum(dcur, axis=0, keepdims=True), i == 0)

    out = pl.BlockSpec((tr, tc), lambda j, i: (i, j))
    return pl.pallas_call(
        body, out_shape=[_sds((T, Cp), BF16), _sds((K, Cp), F32), _sds((1, Cp), F32)], grid=(Cp // tc, nt),
        in_specs=_halo_specs(tr, tc, T, 0, lead=(2,)) + _halo_specs(tr, tc, T, colp) + _halo_specs(tr, tc, T, colx)
        + [pl.BlockSpec((K, tc), lambda j, i: (0, colp + j))],
        out_specs=[out, pl.BlockSpec((K, tc), lambda j, i: (0, j)), pl.BlockSpec((1, tc), lambda j, i: (0, j))],
        compiler_params=_cp(2), name=name)(dact, dact, dact, pre, pre, pre, proj, proj, proj, w)


def _head_blocks(nh, Q):
    return (lax.broadcasted_iota(jnp.int32, (nh, nh * Q), 0) == lax.broadcasted_iota(jnp.int32, (nh, nh * Q), 1) // Q).astype(F32)


@jax.custom_vjp
def _seg_all(cs):
    Q, nh = cs.shape
    blk = _head_blocks(nh, Q)
    lhs = jnp.concatenate([cs, jnp.ones((Q, nh), F32)], axis=1)
    rhs = jnp.concatenate([blk, -blk * jnp.concatenate([cs.T] * nh, axis=1)], axis=0)
    return jnp.dot(lhs, rhs, precision=SPREAD, preferred_element_type=F32)


def _seg_all_fwd(cs):
    return _seg_all(cs), None


def _seg_all_bwd(_, d):
    Q, nq = d.shape
    nh = nq // Q
    d_row = lax.dot_general(d, _head_blocks(nh, Q), (((1,), (1,)), ((), ())), precision=SPREAD, preferred_element_type=F32)
    col = jnp.sum(d, axis=0, keepdims=True)
    d_col = jnp.concatenate([col[:, j * Q:(j + 1) * Q] for j in range(nh)], axis=0).T
    return (d_row - d_col,)


_seg_all.defvjp(_seg_all_fwd, _seg_all_bwd)


def _ssd_chunk(xg, bg, cg, dtr, hin, bias, alog, dsk, rev):
    Q, P8 = xg.shape
    nh = dtr.shape[1]
    P = P8 // nh
    N = bg.shape[1]
    dt = jax.nn.softplus(dtr + bias)
    da = dt * (-jnp.exp(alog))
    r_i = lax.broadcasted_iota(jnp.int32, (Q, Q), 0)
    c_i = lax.broadcasted_iota(jnp.int32, (Q, Q), 1)
    mask = jnp.where(rev, c_i - r_i, r_i - c_i) >= 0
    cs = jnp.dot(mask.astype(F32), da, precision=HI, preferred_element_type=F32)
    cs_t = cs.T
    expand = (lax.broadcasted_iota(jnp.int32, (nh, P8), 0) == lax.broadcasted_iota(jnp.int32, (nh, P8), 1) // P).astype(F32)

    def over_lanes(v):
        return jnp.dot(v, expand, precision=SPREAD, preferred_element_type=F32)

    tot = jnp.where(rev, cs[0:1, :], cs[Q - 1:Q, :])
    dt_x, cs_x, tot_x, dsk_x = over_lanes(dt), over_lanes(cs), over_lanes(tot), over_lanes(dsk)
    decay = jnp.exp(jnp.where(jnp.concatenate([mask] * nh, axis=1), _seg_all(cs), -jnp.inf))
    scores = lax.dot_general(cg.astype(BF16), bg.astype(BF16), (((1,), (1,)), ((), ())), preferred_element_type=F32)
    m_all = (jnp.concatenate([scores] * nh, axis=1) * decay).astype(BF16)
    xdt = xg * dt_x
    xdt_b = xdt.astype(BF16)
    xde = (xdt * jnp.exp(tot_x - cs_x)).astype(BF16)
    e_in = jnp.exp(cs_x)
    low = lax.broadcasted_iota(jnp.int32, (1, 2 * P), 1) < P
    cb, bb = cg.astype(BF16), bg.astype(BF16)
    zero = jnp.zeros((), BF16)
    ys, sts = [], []
    for p in range(nh // 2):
        sl = slice(2 * p * P, 2 * (p + 1) * P)
        xp = xdt_b[:, sl]
        y = jnp.dot(m_all[:, 2 * p * Q:(2 * p + 1) * Q], jnp.where(low, xp, zero), preferred_element_type=F32)
        y = y + jnp.dot(m_all[:, (2 * p + 1) * Q:(2 * p + 2) * Q], jnp.where(low, zero, xp), preferred_element_type=F32)
        y = y + lax.dot_general(cb, hin[sl, :].astype(BF16), (((1,), (1,)), ((), ())), preferred_element_type=F32) * e_in[:, sl]
        ys.append(y)
        sts.append(lax.dot_general(xde[:, sl], bb, (((0,), (0,)), ((), ())), preferred_element_type=F32))
    y = jnp.concatenate(ys, axis=1) + dsk_x * xg
    tot_c = jnp.where(rev, cs_t[:, 0:1], cs_t[:, Q - 1:Q])
    etot = lax.dot_general(expand, jnp.broadcast_to(jnp.exp(tot_c), (nh, N)), (((0,), (0,)), ((), ())), precision=SPREAD,
                           preferred_element_type=F32)
    return y, etot * hin + jnp.concatenate(sts, axis=0)


SSD_GROUPS_PER_STEP = 8


def _ssd_specs(Q, P8, N, ncc, NC, gpb, d_inner, GN, back):
    def chunk(d, s):
        s = (NC - 1 - s) if back else s
        return jnp.where(d == 0, s, jnp.where(s < ncc, ncc - 1 - s, ncc + NC - 1 - s))

    def step(s):
        return (NC - 1 - s) if back else s

    nb = gpb * N
    xcol_b, xcol_c = d_inner // nb, (d_inner + GN) // nb
    x = pl.BlockSpec((Q, gpb * P8), lambda d, g, s: (chunk(d, s), g))
    bsp = pl.BlockSpec((Q, nb), lambda d, g, s: (chunk(d, s), xcol_b + g))
    csp = pl.BlockSpec((Q, nb), lambda d, g, s: (chunk(d, s), xcol_c + g))
    dt = pl.BlockSpec((None, gpb, Q, 8), lambda d, g, s: (d, g, chunk(d, s), 0))
    par = pl.BlockSpec((None, gpb, 1, 8), lambda d, g, s: (d, g, 0, 0))
    hst = pl.BlockSpec((None, gpb, None, P8, N), lambda d, g, s: (d, g, step(s), 0, 0))
    yd = pl.BlockSpec((None, Q, gpb * P8), lambda d, g, s: (d, chunk(d, s), g))
    bd = pl.BlockSpec((None, Q, nb), lambda d, g, s: (d, chunk(d, s), g))
    return x, bsp, csp, dt, par, hst, yd, bd


def _ssd_dims(act, dtr, d_inner):
    T, G = act.shape[0], dtr.shape[1]
    gpb = min(SSD_GROUPS_PER_STEP, G)
    GN = G * N_STATE
    assert G % gpb == 0 and d_inner % (gpb * N_STATE) == 0 and GN % (gpb * N_STATE) == 0
    return T, G, gpb, GN, T // CHUNK, d_inner // G


def ssd_fwd(act, dtr, bias, alog, dsk, d_inner, ncc, name):
    Q, N = CHUNK, N_STATE
    T, G, gpb, GN, NC, P8 = _ssd_dims(act, dtr, d_inner)
    x, bsp, csp, dt, par, hst, yd, _ = _ssd_specs(Q, P8, N, ncc, NC, gpb, d_inner, GN, False)

    def body(x_ref, b_ref, c_ref, dt_ref, bias_ref, alog_ref, dsk_ref, y_ref, h_ref, st_ref):
        d, s = pl.program_id(0), pl.program_id(2)

        @pl.when(s == 0)
        def _():
            st_ref[...] = jnp.zeros_like(st_ref)

        for q in range(gpb):
            xs, ns = slice(q * P8, (q + 1) * P8), slice(q * N, (q + 1) * N)
            hin = st_ref[q]
            h_ref[q] = hin
            y, ho = _ssd_chunk(x_ref[:, xs], b_ref[:, ns], c_ref[:, ns], dt_ref[q], hin, bias_ref[q], alog_ref[q], dsk_ref[q], d == 1)
            y_ref[:, xs] = y
            st_ref[q] = ho

    return pl.pallas_call(
        body, out_shape=[_sds((2, T, d_inner), F32), _sds((2, G, NC, P8, N), F32)], grid=(2, G // gpb, NC),
        in_specs=[x, bsp, csp, dt, par, par, par], out_specs=[yd, hst], scratch_shapes=[pltpu.VMEM((gpb, P8, N), F32)],
        compiler_params=_cp(3), name=name)(act, act, act, dtr, bias, alog, dsk)


def ssd_bwd(act, dtr, bias, alog, dsk, hsave, dy, d_inner, ncc, name):
    Q, N = CHUNK, N_STATE
    T, G, gpb, GN, NC, P8 = _ssd_dims(act, dtr, d_inner)
    x, bsp, csp, dt, par, hst, yd, bd = _ssd_specs(Q, P8, N, ncc, NC, gpb, d_inner, GN, True)
    dysp = pl.BlockSpec((Q, gpb * P8), x.index_map)

    def body(x_ref, b_ref, c_ref, dt_ref, bias_ref, alog_ref, dsk_ref, h_ref, dy_ref,
             dx_ref, db_ref, dc_ref, ddt_ref, dbias_ref, dalog_ref, ddsk_ref, dh_ref):
        d, s = pl.program_id(0), pl.program_id(2)

        @pl.when(s == 0)
        def _():
            dh_ref[...] = jnp.zeros_like(dh_ref)

        for q in range(gpb):
            xs, ns = slice(q * P8, (q + 1) * P8), slice(q * N, (q + 1) * N)
            args = (x_ref[:, xs], b_ref[:, ns], c_ref[:, ns], dt_ref[q], h_ref[q], bias_ref[q], alog_ref[q], dsk_ref[q])
            _, vjp = jax.vjp(functools.partial(_ssd_chunk, rev=d == 1), *args)
            dx, db, dc, ddt, dhin, dbias, dalog, ddsk = vjp((dy_ref[:, xs], dh_ref[q]))
            dx_ref[:, xs] = dx
            db_ref[:, ns] = db
            dc_ref[:, ns] = dc
            ddt_ref[q] = ddt
            dh_ref[q] = dhin
            _acc(dbias_ref.at[q], dbias, s == 0)
            _acc(dalog_ref.at[q], dalog, s == 0)
            _acc(ddsk_ref.at[q], ddsk, s == 0)

    return pl.pallas_call(
        body,
        out_shape=[_sds((2, T, d_inner), F32), _sds((2, T, GN), F32), _sds((2, T, GN), F32), _sds(dtr.shape, F32),
                   _sds(bias.shape, F32), _sds(bias.shape, F32), _sds(bias.shape, F32)],
        grid=(2, G // gpb, NC), in_specs=[x, bsp, csp, dt, par, par, par, hst, dysp], out_specs=[yd, bd, bd, dt, par, par, par],
        scratch_shapes=[pltpu.VMEM((gpb, P8, N), F32)], compiler_params=_cp(3), name=name)(act, act, act, dtr, bias, alog, dsk, hsave, dy)


def _gnorm_fn(yf, yb, z, g):
    return _rms((yf + yb) * _silu(z), g)


def gnorm_fwd(y2, proj, g, tr, name):
    _, T, C = y2.shape
    yf = pl.BlockSpec((None, tr, C), lambda i: (0, i, 0))
    yb = pl.BlockSpec((None, tr, C), lambda i: (1, i, 0))
    row = pl.BlockSpec((tr, C), lambda i: (i, 0))
    vec = pl.BlockSpec((1, C), lambda i: (0, 0))

    def body(yf_ref, yb_ref, z_ref, g_ref, o_ref):
        o_ref[...] = _gnorm_fn(yf_ref[...], yb_ref[...], z_ref[...], g_ref[...]).astype(BF16)

    return pl.pallas_call(body, out_shape=_sds((T, C), BF16), grid=(T // tr,), in_specs=[yf, yb, row, vec], out_specs=row,
                          compiler_params=_cp(1), name=name)(y2, y2, proj, g)


def gnorm_bwd(y2, proj, g, dyn, tr, name):
    _, T, C = y2.shape
    yf = pl.BlockSpec((None, tr, C), lambda i: (0, i, 0))
    yb = pl.BlockSpec((None, tr, C), lambda i: (1, i, 0))
    row = pl.BlockSpec((tr, C), lambda i: (i, 0))
    vec = pl.BlockSpec((1, C), lambda i: (0, 0))

    def body(yf_ref, yb_ref, z_ref, g_ref, d_ref, dy_ref, dz_ref, dg_ref):
        i = pl.program_id(0)
        _, vjp = jax.vjp(_gnorm_fn, yf_ref[...], yb_ref[...], z_ref[...], g_ref[...])
        dyf, _, dz, dg = vjp(d_ref[...].astype(F32))
        dy_ref[...] = dyf
        dz_ref[...] = dz.astype(BF16)
        _acc(dg_ref, dg, i == 0)

    return pl.pallas_call(body, out_shape=[_sds((T, C), F32), _sds((T, C), BF16), _sds((1, C), F32)], grid=(T // tr,),
                          in_specs=[yf, yb, row, vec, row], out_specs=[row, row, vec], compiler_params=_cp(1), name=name)(y2, y2, proj, g, dyn)


def _glu_fn(a):
    D = a.shape[1] // 2
    return a[:, :D] * jax.nn.sigmoid(a[:, D:])


def _ln_swish_fn(v, g, b):
    mu = jnp.mean(v, axis=-1, keepdims=True)
    xc = v - mu
    var = jnp.mean(xc * xc, axis=-1, keepdims=True)
    y = xc * lax.rsqrt(var + EPS) * g + b
    return y * jax.nn.sigmoid(y)


def _seg_pos(tr, seg, i):
    p = lax.broadcasted_iota(jnp.int32, (tr, 1), 0)
    s = jnp.where(i == 0, tr, seg)
    return p & (s - 1), s


def _dw_taps(v, w, pos, s, sign):
    tr = v.shape[0]
    K = w.shape[0]
    acc = jnp.zeros_like(v)
    for k in range(K):
        o = sign * (k - K // 2)
        q = pos + o
        ok = jnp.logical_and(q >= 0, q < s).astype(v.dtype)
        acc = acc + w[k:k + 1, :] * (pltpu.roll(v, (-o) % tr, 0) * ok)
    return acc


def _lane_blocks(v, ref):
    for c in range(v.shape[1] // 128):
        ref[c] = v[:, c * 128:(c + 1) * 128]


def _from_lane_blocks(ref):
    return jnp.concatenate([ref[c] for c in range(ref.shape[0])], axis=1)


def confmid_fwd(a, w3, b, lg, lb, seg, tr, name):
    T, D2 = a.shape
    D = D2 // 2
    nb, K, _ = w3.shape
    vec = pl.BlockSpec((1, D), lambda i: (0, 0))
    row = pl.BlockSpec((tr, D), lambda i: (i, 0))

    def body(a_ref, w_ref, b_ref, lg_ref, lb_ref, o_ref, v1_ref, s0_ref, s1_ref):
        i = pl.program_id(0)
        pos, s = _seg_pos(tr, seg, i)
        _lane_blocks(_glu_fn(a_ref[...]), s0_ref)

        def blk(c, carry):
            s1_ref[c] = _dw_taps(s0_ref[c], w_ref[c], pos, s, 1)
            return carry

        lax.fori_loop(0, nb, blk, 0)
        v1 = _from_lane_blocks(s1_ref) + b_ref[...]
        v1_ref[...] = v1
        o_ref[...] = _ln_swish_fn(v1, lg_ref[...], lb_ref[...]).astype(BF16)

    return pl.pallas_call(
        body, out_shape=[_sds((T, D), BF16), _sds((T, D), F32)], grid=(T // tr,),
        in_specs=[pl.BlockSpec((tr, D2), lambda i: (i, 0)), pl.BlockSpec((nb, K, 128), lambda i: (0, 0, 0)), vec, vec, vec],
        out_specs=[row, row], scratch_shapes=[pltpu.VMEM((nb, tr, 128), F32)] * 2, compiler_params=_cp(1), name=name)(a, w3, b, lg, lb)


def confmid_bwd(a, v1, w3, lg, lb, dv, seg, tr, name):
    T, D2 = a.shape
    D = D2 // 2
    nb, K, _ = w3.shape
    vec = pl.BlockSpec((1, D), lambda i: (0, 0))
    vec2 = pl.BlockSpec((1, D2), lambda i: (0, 0))
    wsp = pl.BlockSpec((nb, K, 128), lambda i: (0, 0, 0))
    row = pl.BlockSpec((tr, D), lambda i: (i, 0))

    def body(a_ref, v1_ref, w_ref, lg_ref, lb_ref, dv_ref, da_ref, dsum_ref, dw_ref, db_ref, dlg_ref, dlb_ref, s0_ref, s1_ref, s2_ref):
        i = pl.program_id(0)
        first = i == 0
        pos, s = _seg_pos(tr, seg, i)
        v0, glu_vjp = jax.vjp(_glu_fn, a_ref[...])
        _lane_blocks(v0, s0_ref)
        _, ln_vjp = jax.vjp(_ln_swish_fn, v1_ref[...], lg_ref[...], lb_ref[...])
        dv1, dlg, dlb = ln_vjp(dv_ref[...].astype(F32))
        _acc(db_ref, jnp.sum(dv1, axis=0, keepdims=True), first)
        _acc(dlg_ref, dlg, first)
        _acc(dlb_ref, dlb, first)
        _lane_blocks(dv1, s2_ref)

        @pl.when(first)
        def _():
            dw_ref[...] = jnp.zeros_like(dw_ref)

        def conv_t(c, carry):
            d1, v0c = s2_ref[c], s0_ref[c]
            s1_ref[c] = _dw_taps(d1, w_ref[c], pos, s, -1)
            for k in range(K):
                o = k - K // 2
                q = pos + o
                ok = jnp.logical_and(q >= 0, q < s).astype(F32)
                dw_ref[c, k:k + 1, :] += jnp.sum(d1 * (pltpu.roll(v0c, (-o) % tr, 0) * ok), axis=0, keepdims=True)
            return carry

        lax.fori_loop(0, nb, conv_t, 0)
        (da,) = glu_vjp(_from_lane_blocks(s1_ref))
        da_ref[...] = da.astype(BF16)
        _acc(dsum_ref, jnp.sum(da, axis=0, keepdims=True), first)

    return pl.pallas_call(
        body, out_shape=[_sds((T, D2), BF16), _sds((1, D2), F32), _sds((nb, K, 128), F32), _sds((1, D), F32), _sds((1, D), F32), _sds((1, D), F32)],
        grid=(T // tr,), in_specs=[pl.BlockSpec((tr, D2), lambda i: (i, 0)), row, wsp, vec, vec, row],
        out_specs=[pl.BlockSpec((tr, D2), lambda i: (i, 0)), vec2, wsp, vec, vec, vec], scratch_shapes=[pltpu.VMEM((nb, tr, 128), F32)] * 3,
        compiler_params=_cp(1), name=name)(a, v1, w3, lg, lb, dv)


def mod_fwd(rows, w, bsl, name):
    Ly, D, Nc = w.shape
    tn = _div_tile(Nc, 512)

    def body(r_ref, w_ref, b_ref, o_ref):
        s = _silu(r_ref[...]).astype(BF16)
        o_ref[...] = jnp.dot(s, w_ref[...].astype(BF16), preferred_element_type=F32) + b_ref[...]

    return pl.pallas_call(
        body, out_shape=_sds((Ly, 16, Nc), F32), grid=(Ly, Nc // tn),
        in_specs=[pl.BlockSpec((16, D), lambda l, j: (0, 0)), pl.BlockSpec((None, D, tn), lambda l, j: (l, 0, j)),
                  pl.BlockSpec((None, 1, tn), lambda l, j: (l, 0, j))],
        out_specs=pl.BlockSpec((None, 16, tn), lambda l, j: (l, 0, j)), compiler_params=_cp(2), name=name)(rows, w, bsl)


def mod_bwd(rows, w, dm, name):
    Ly, D, Nc = w.shape
    tn = _div_tile(Nc, 512)
    nj = Nc // tn

    def body(r_ref, w_ref, dm_ref, dw_ref, ds_ref):
        j = pl.program_id(1)
        s = _silu(r_ref[...]).astype(BF16)
        dmv = dm_ref[...].astype(BF16)
        dw_ref[...] = lax.dot_general(s, dmv, (((0,), (0,)), ((), ())), preferred_element_type=F32)
        _acc(ds_ref, lax.dot_general(dmv, w_ref[...].astype(BF16), (((1,), (1,)), ((), ())), preferred_element_type=F32), j == 0)

    return pl.pallas_call(
        body, out_shape=[_sds((Ly, D, Nc), F32), _sds((Ly, 16, D), F32)], grid=(Ly, nj),
        in_specs=[pl.BlockSpec((16, D), lambda l, j: (0, 0)), pl.BlockSpec((None, D, tn), lambda l, j: (l, 0, j)),
                  pl.BlockSpec((None, 16, tn), lambda l, j: (l, 0, j))],
        out_specs=[pl.BlockSpec((None, D, tn), lambda l, j: (l, 0, j)), pl.BlockSpec((None, 16, D), lambda l, j: (l, 0, 0))],
        compiler_params=_cp(2), name=name)(rows, w, dm)


def silu_grad(dsc, c, name):
    def body(d_ref, c_ref, o_ref):
        x = c_ref[...]
        s = jax.nn.sigmoid(x)
        o_ref[...] = d_ref[...] * (s * (1.0 + x * (1.0 - s)))

    return pl.pallas_call(body, out_shape=_sds(c.shape, F32), name=name)(dsc, c)


def _coords():
    return lax.axis_index("x"), lax.axis_index("y"), lax.axis_index("c")


def _flip(v, bit):
    return 1 - v if bit else v


def allgather8(x, name):
    R, C = x.shape

    def body(x_ref, o_ref, send_sems, recv_sems, local_sem):
        mx, my, mc = _coords()
        me = 4 * mx + 2 * my + mc
        mine = pltpu.make_async_copy(x_ref, o_ref.at[me], local_sem)
        mine.start()
        copies = []
        for k in range(1, 8):
            px, py, pc = _flip(mx, k & 4), _flip(my, k & 2), _flip(mc, k & 1)
            cp = pltpu.make_async_remote_copy(src_ref=x_ref, dst_ref=o_ref.at[me], send_sem=send_sems.at[k - 1],
                                              recv_sem=recv_sems.at[k - 1], device_id=(px, py, pc), device_id_type=MESH)
            cp.start()
            copies.append((cp, 4 * px + 2 * py + pc))
        for k, (cp, peer) in enumerate(copies):
            pltpu.make_async_remote_copy(src_ref=x_ref, dst_ref=o_ref.at[peer], send_sem=send_sems.at[k], recv_sem=recv_sems.at[k],
                                         device_id=(mx, my, mc), device_id_type=MESH).wait_recv()
        for cp, _ in copies:
            cp.wait_send()
        mine.wait()

    return pl.pallas_call(
        body, out_shape=_sds((8, R, C), F32), in_specs=[pl.BlockSpec(memory_space=pltpu.VMEM)],
        out_specs=pl.BlockSpec(memory_space=pltpu.VMEM),
        scratch_shapes=[pltpu.SemaphoreType.DMA((7,)), pltpu.SemaphoreType.DMA((7,)), pltpu.SemaphoreType.DMA],
        compiler_params=pltpu.CompilerParams(vmem_limit_bytes=VMEM_LIMIT), name=name)(x)


def chip_exchange(arrs, name):
    n = len(arrs)

    def src(ref, k):
        return ref.at[k]

    def body(*refs):
        ins, outs = refs[:n], refs[n:2 * n]
        send_sems, recv_sems = refs[2 * n:]
        mx, my, mc = _coords()
        me = 2 * mx + my
        sends = []
        for a in range(n):
            for k in range(1, 4):
                px, py = _flip(mx, k & 2), _flip(my, k & 1)
                cp = pltpu.make_async_remote_copy(src_ref=src(ins[a], 2 * px + py), dst_ref=outs[a].at[me], send_sem=send_sems.at[3 * a + k - 1],
                                                  recv_sem=recv_sems.at[3 * a + k - 1], device_id=(px, py, mc), device_id_type=MESH)
                cp.start()
                sends.append((cp, a, k, 2 * px + py))
        for cp, a, k, peer in sends:
            pltpu.make_async_remote_copy(src_ref=src(ins[a], me), dst_ref=outs[a].at[peer], send_sem=send_sems.at[3 * a + k - 1],
                                         recv_sem=recv_sems.at[3 * a + k - 1], device_id=(mx, my, mc), device_id_type=MESH).wait_recv()
        for cp, *_ in sends:
            cp.wait_send()

    hbm = pl.BlockSpec(memory_space=pl.ANY)
    return pl.pallas_call(
        body, out_shape=[_sds(a.shape, a.dtype) for a in arrs], in_specs=[hbm] * n, out_specs=[hbm] * n,
        scratch_shapes=[pltpu.SemaphoreType.DMA((3 * n,)), pltpu.SemaphoreType.DMA((3 * n,))], name=name)(*arrs)


def chip_allgather(arrs, name):
    n = len(arrs)

    def body(*refs):
        ins, outs, passed_outs = refs[:n], refs[n:2 * n], refs[2 * n:5 * n]
        send_sems, recv_sems, pass_send, pass_recv = refs[5 * n:]
        mx, my, mc = _coords()
        me = 2 * mx + my
        sends = []
        for a in range(n):
            for k in range(1, 4):
                px, py = _flip(mx, k & 2), _flip(my, k & 1)
                cp = pltpu.make_async_remote_copy(src_ref=ins[a].at[mc], dst_ref=outs[a].at[me, mc],
                                                  send_sem=send_sems.at[3 * a + k - 1], recv_sem=recv_sems.at[3 * a + k - 1],
                                                  device_id=(px, py, mc), device_id_type=MESH)
                cp.start()
                sends.append((cp, a, 3 * a + k - 1, 2 * px + py))
        passed = []
        for cp, a, s, peer in sends:
            got = outs[a].at[peer, mc]
            pltpu.make_async_remote_copy(src_ref=got, dst_ref=got, send_sem=send_sems.at[s], recv_sem=recv_sems.at[s],
                                         device_id=(mx, my, mc), device_id_type=MESH).wait_recv()
            fw = pltpu.make_async_remote_copy(src_ref=got, dst_ref=passed_outs[s], send_sem=pass_send.at[s], recv_sem=pass_recv.at[s],
                                              device_id=(mx, my, 1 - mc), device_id_type=MESH)
            fw.start()
            passed.append(fw)
        for cp, a, s, peer in sends:
            pltpu.make_async_remote_copy(src_ref=passed_outs[s], dst_ref=passed_outs[s], send_sem=pass_send.at[s], recv_sem=pass_recv.at[s],
                                         device_id=(mx, my, mc), device_id_type=MESH).wait_recv()
        for cp, *_ in sends:
            cp.wait_send()
        for fw in passed:
            fw.wait_send()

    hbm = pl.BlockSpec(memory_space=pl.ANY)
    sems = pltpu.SemaphoreType.DMA((3 * n,))
    res = pl.pallas_call(
        body, out_shape=[_sds((4,) + a.shape, a.dtype) for a in arrs] + [_sds(a.shape[1:], a.dtype) for a in arrs for _ in range(3)],
        in_specs=[hbm] * n, out_specs=[hbm] * (4 * n), scratch_shapes=[sems, sems, sems, sems], name=name)(*arrs)
    return res[:n], [res[n + 3 * a:n + 3 * a + 3] for a in range(n)]


def sibling_swap_halves(arrs, name):
    n = len(arrs)

    def body(*refs):
        ins, outs = refs[:n], refs[n:2 * n]
        send_sems, recv_sems = refs[2 * n:]
        mx, my, mc = _coords()
        cps = []
        for a in range(n):
            hl = arrs[a].shape[1] // 2
            cp = pltpu.make_async_remote_copy(src_ref=ins[a].at[pl.ds(0, 4), pl.ds((1 - mc) * hl, hl)], dst_ref=outs[a],
                                              send_sem=send_sems.at[a], recv_sem=recv_sems.at[a],
                                              device_id=(mx, my, 1 - mc), device_id_type=MESH)
            cp.start()
            cps.append(cp)
        for cp in cps:
            cp.wait()

    hbm = pl.BlockSpec(memory_space=pl.ANY)
    return pl.pallas_call(body, out_shape=[_sds((4, a.shape[1] // 2) + a.shape[2:], a.dtype) for a in arrs], in_specs=[hbm] * n,
                          out_specs=[hbm] * n, scratch_shapes=[pltpu.SemaphoreType.DMA((n,)), pltpu.SemaphoreType.DMA((n,))], name=name)(*arrs)


def sibling_exchange(arrs, name):
    n = len(arrs)

    def body(*refs):
        ins, outs = refs[:n], refs[n:2 * n]
        send_sems, recv_sems = refs[2 * n:]
        mx, my, mc = _coords()
        cps = []
        for a in range(n):
            cp = pltpu.make_async_remote_copy(src_ref=ins[a], dst_ref=outs[a], send_sem=send_sems.at[a], recv_sem=recv_sems.at[a],
                                              device_id=(mx, my, 1 - mc), device_id_type=MESH)
            cp.start()
            cps.append(cp)
        for cp in cps:
            cp.wait()

    hbm = pl.BlockSpec(memory_space=pl.ANY)
    return pl.pallas_call(body, out_shape=[_sds(a.shape, a.dtype) for a in arrs], in_specs=[hbm] * n, out_specs=[hbm] * n,
                          scratch_shapes=[pltpu.SemaphoreType.DMA((n,)), pltpu.SemaphoreType.DMA((n,))], name=name)(*arrs)


def add_pair(a, b, name, tr=512):
    R, C = a.shape
    tr = _div_tile(R, tr, 8)
    row = pl.BlockSpec((tr, C), lambda i: (i, 0))

    def body(a_ref, b_ref, o_ref):
        o_ref[...] = (a_ref[...].astype(F32) + b_ref[...].astype(F32)).astype(BF16)

    return pl.pallas_call(body, out_shape=_sds((R, C), BF16), grid=(R // tr,), in_specs=[row, row], out_specs=row,
                          compiler_params=_cp(1), name=name)(a, b)


def sum_slabs(x, name, tr=256):
    n, R, C = x.shape
    tr = _div_tile(R, tr, 8)

    def body(x_ref, o_ref):
        acc = x_ref[0].astype(F32)
        for k in range(1, n):
            acc = acc + x_ref[k].astype(F32)
        o_ref[...] = acc

    return pl.pallas_call(body, out_shape=_sds((R, C), F32), grid=(R // tr,), in_specs=[pl.BlockSpec((n, tr, C), lambda i: (0, i, 0))],
                          out_specs=pl.BlockSpec((tr, C), lambda i: (i, 0)), compiler_params=_cp(1), name=name)(x)


def adamw(w, g, m, v, name, tr=256):
    R, C = w.shape
    split = isinstance(g, tuple)
    nh = 2 if split else 1
    tr = _div_tile(R // nh, tr, 8)
    nt = R // nh // tr
    row = pl.BlockSpec((tr, C), lambda h, i: (h * nt + i, 0))
    part = pl.BlockSpec((tr, C), lambda h, i: (i, 0))
    ins = [w] + (list(g) if split else [g]) + [m, v]

    def body(*refs):
        w_ref = refs[0]
        if split:
            mine = pl.program_id(0) == lax.axis_index("c")
            g = jnp.where(mine, refs[1][...], refs[2][...])
        else:
            g = refs[1][...]
        m_ref, v_ref, go_ref, d_ref, mo_ref, vo_ref = refs[nh + 1:]
        mn = ADAM_B1 * m_ref[...] + (1.0 - ADAM_B1) * g
        vn = ADAM_B2 * v_ref[...] + (1.0 - ADAM_B2) * (g * g)
        m_hat = mn / (1.0 - ADAM_B1 ** ADAM_STEP)
        v_hat = vn / (1.0 - ADAM_B2 ** ADAM_STEP)
        go_ref[...] = g
        d_ref[...] = -ADAM_LR * (m_hat / (jnp.sqrt(v_hat) + ADAM_EPS) + ADAM_WD * w_ref[...])
        mo_ref[...] = mn
        vo_ref[...] = vn

    return pl.pallas_call(body, out_shape=[_sds((R, C), F32)] * 4, grid=(nh, nt), in_specs=[row] + [part] * nh + [row, row],
                          out_specs=[row] * 4, compiler_params=_cp(2), name=name)(*ins)


W_NAMES = ("c_ctx", "mod_w", "mod_b", "pre_mix_g", "post_mix_g", "pre_mlp_g", "post_mlp_g", "mlp_w1", "mlp_w2", "ssm_in_w",
           "ssm_conv_w", "ssm_conv_b", "ssm_a_log_f", "ssm_dt_bias_f", "ssm_d_f", "ssm_a_log_b", "ssm_dt_bias_b", "ssm_d_b",
           "ssm_norm_g", "ssm_out_w", "conf_pw1_w", "conf_pw1_b", "conf_dw_w", "conf_dw_b", "conf_ln_g", "conf_ln_b",
           "conf_pw2_w", "conf_pw2_b")
BIG = {"mlp_w1": "col", "mlp_w2": "row", "ssm_in_w": "col", "ssm_out_w": "row", "conf_pw1_w": "col", "conf_pw2_w": "row"}
SMALL_SHARDED = ("ssm_conv_w", "conf_pw1_b", "conf_dw_w", "conf_dw_b", "conf_ln_g", "conf_ln_b", "conf_pw2_b")
PACK_W = 1024


def _pack(arrs):
    flat = jnp.concatenate([a.reshape(-1).astype(F32) for a in arrs])
    n = flat.shape[0]
    tot = -(-n // (8 * PACK_W)) * (8 * PACK_W)
    return jnp.pad(flat, (0, tot - n)).reshape(tot // PACK_W, PACK_W)


def _unpack(buf, shapes):
    lead = buf.shape[:-2]
    flat = buf.reshape(lead + (-1,))
    out, off = [], 0
    for shp in shapes:
        n = 1
        for d in shp:
            n *= d
        out.append(flat[..., off:off + n].reshape(lead + tuple(shp)))
        off += n
    return out


def _full_from_chips(g, kind):
    if kind == "col":
        return jnp.moveaxis(g, 0, -2).reshape(g.shape[1:-1] + (4 * g.shape[-1],))
    return jnp.moveaxis(g, 0, 1).reshape((g.shape[1], 4 * g.shape[2]) + g.shape[3:])


def _chip_slabs(full, kind):
    if kind == "col":
        return jnp.moveaxis(full.reshape(full.shape[:-1] + (4, full.shape[-1] // 4)), -2, 0)
    return jnp.moveaxis(full.reshape((full.shape[0], 4, full.shape[1] // 4) + full.shape[2:]), 1, 0)


def _view2d(a):
    if a.ndim == 1:
        return a.reshape(1, -1)
    return a.reshape(-1, a.shape[-1])


def kernel(x, c, ctx, c_ctx, mod_w, mod_b, pre_mix_g, post_mix_g, pre_mlp_g, post_mlp_g, mlp_w1, mlp_w2, ssm_in_w, ssm_conv_w, ssm_conv_b, ssm_a_log_f, ssm_dt_bias_f, ssm_d_f, ssm_a_log_b, ssm_dt_bias_b, ssm_d_b, ssm_norm_g, ssm_out_w, conf_pw1_w, conf_pw1_b, conf_dw_w, conf_dw_b, conf_ln_g, conf_ln_b, conf_pw2_w, conf_pw2_b, loss_target, m_c_ctx, m_mod_w, m_mod_b, m_pre_mix_g, m_post_mix_g, m_pre_mlp_g, m_post_mlp_g, m_mlp_w1, m_mlp_w2, m_ssm_in_w, m_ssm_conv_w, m_ssm_conv_b, m_ssm_a_log_f, m_ssm_dt_bias_f, m_ssm_d_f, m_ssm_a_log_b, m_ssm_dt_bias_b, m_ssm_d_b, m_ssm_norm_g, m_ssm_out_w, m_conf_pw1_w, m_conf_pw1_b, m_conf_dw_w, m_conf_dw_b, m_conf_ln_g, m_conf_ln_b, m_conf_pw2_w, m_conf_pw2_b, v_c_ctx, v_mod_w, v_mod_b, v_pre_mix_g, v_post_mix_g, v_pre_mlp_g, v_post_mlp_g, v_mlp_w1, v_mlp_w2, v_ssm_in_w, v_ssm_conv_w, v_ssm_conv_b, v_ssm_a_log_f, v_ssm_dt_bias_f, v_ssm_d_f, v_ssm_a_log_b, v_ssm_dt_bias_b, v_ssm_d_b, v_ssm_norm_g, v_ssm_out_w, v_conf_pw1_w, v_conf_pw1_b, v_conf_dw_w, v_conf_dw_b, v_conf_ln_g, v_conf_ln_b, v_conf_pw2_w, v_conf_pw2_b):
    given = dict(locals())
    W = {n: given[n] for n in W_NAMES}
    L, D = x.shape[1], x.shape[2]
    Lc = ctx.shape[1]
    T = Lc + L
    depth = mod_w.shape[0]
    d_inner = ssm_norm_g.shape[1]
    H = ssm_a_log_f.shape[1]
    xbc = ssm_conv_b.shape[1]
    GN = (xbc - d_inner) // 2
    G = GN // N_STATE
    rows_grid = L // GRID_W
    tr = Lc
    ncc = Lc // CHUNK
    assert H == 8 * G and Lc % CHUNK == 0 and L % Lc == 0 and tr % GRID_W == 0 and tr % rows_grid == 0
    tc = _div_tile(GN, 1024)
    assert d_inner % tc == 0
    mx, my, mc = _coords()
    chip = 2 * mx + my
    dev = 4 * mx + 2 * my + mc

    small_shapes = [(1, D)] + [W[n].shape for n in SMALL_SHARDED]
    got = allgather8(_pack([c] + [W[n] for n in SMALL_SHARDED]), "gather_small")
    parts = _unpack(got, small_shapes)
    c_all = parts[0].reshape(8, D)
    full_small = {n: jnp.concatenate([p[2 * k] for k in range(4)], axis=-1) for n, p in zip(SMALL_SHARDED, parts[1:])}

    cond = jnp.concatenate([c_all, c_ctx.reshape(1, D), jnp.zeros((7, D), F32)], axis=0)
    ncol = mod_w.shape[2]
    bsl = lax.dynamic_slice(mod_b, (0, chip * ncol), (depth, ncol)).reshape(depth, 1, ncol)
    m_loc = mod_fwd(cond, mod_w, bsl, "mod_fwd")
    m_all = allgather8(m_loc.reshape(depth * 16, ncol), "gather_mod").reshape(8, depth, 16, ncol)
    m_full = jnp.concatenate([m_all[2 * k] for k in range(4)], axis=-1)
    m_lat = lax.dynamic_slice(m_full, (0, dev, 0), (depth, 1, 6 * D))
    m2 = jnp.concatenate([m_full[:, 8:9], m_lat], axis=1)

    def six(i):
        return [m2[i, :, k * D:(k + 1) * D].reshape(2, 1, D) for k in range(6)]

    big_names = list(BIG)
    shards = [W[n].astype(BF16).reshape((2, W[n].shape[0] // 2) + W[n].shape[1:]) for n in big_names]
    gathered, passed_on = chip_allgather(shards, "gather_weights")
    Wb = {}
    for n, own, g, ps in zip(big_names, shards, gathered, passed_on):
        g = lax.dynamic_update_slice(g, own[None], (chip,) + (0,) * own.ndim)
        for k in (1, 2, 3):
            at = (2 * _flip(mx, k & 2) + _flip(my, k & 1), 1 - mc) + (0,) * (g.ndim - 2)
            g = lax.dynamic_update_slice(g, ps[k - 1][None, None], at)
        g = g.reshape((4,) + W[n].shape)
        Wb[n] = [_full_from_chips(g[:, j:j + 1], BIG[n])[0] for j in range(W[n].shape[0])]

    def to_scan(u):
        lat = u[Lc:].reshape(rows_grid, GRID_W, u.shape[1]).swapaxes(0, 1).reshape(L, u.shape[1])
        return jnp.concatenate([u[:Lc], lat], axis=0)

    def from_scan(u):
        lat = u[Lc:].reshape(GRID_W, rows_grid, u.shape[1]).swapaxes(0, 1).reshape(L, u.shape[1])
        return jnp.concatenate([u[:Lc], lat], axis=0)

    def ssm_params(j):
        def two(f, b):
            return jnp.stack([f[j], b[j]]).reshape(2, G, 1, 8)
        return two(ssm_dt_bias_f, ssm_dt_bias_b), two(ssm_a_log_f, ssm_a_log_b), two(ssm_d_f, ssm_d_b)

    def dw3_of(j):
        w = full_small["conf_dw_w"][j]
        return w.reshape(w.shape[0], D // 128, 128).swapaxes(0, 1)

    h = jnp.concatenate([ctx[0], x[0]], axis=0)
    saved = []
    for i in range(depth):
        kind, j = i % 2, i // 2
        col_major = (j % 2) == 1
        sh1, sc1, g1, sh2, sc2, g2 = six(i)
        s = {"h": h}
        u = prenorm_fwd(h, pre_mix_g[i][None], sh1, sc1, tr, f"prenorm_mix{i}")
        if col_major:
            u = to_scan(u)
        s["u"] = u
        if kind == 0:
            proj = mm(u, Wb["ssm_in_w"][j], "nn", F32, name=f"ssm_in{i}")
            pre, act = conv5_fwd(proj, full_small["ssm_conv_w"][j], ssm_conv_b[j][None], d_inner // tc, tr, tc, f"ssm_conv{i}")
            dtr = proj[:, d_inner + xbc:].reshape(T, 2, G, 8).transpose(1, 2, 0, 3)
            bias, alog, dsk = ssm_params(j)
            y2, hsave = ssd_fwd(act, dtr, bias, alog, dsk, d_inner, ncc, f"ssd_fwd{i}")
            yn = gnorm_fwd(y2, proj, ssm_norm_g[j][None], CHUNK, f"ssm_gnorm{i}")
            out = mm(yn, Wb["ssm_out_w"][j], "nn", F32, name=f"ssm_out{i}")
            s.update(proj=proj, pre=pre, act=act, dtr=dtr, y2=y2, hsave=hsave, yn=yn)
        else:
            seg = rows_grid if col_major else GRID_W
            a = mm(u, Wb["conf_pw1_w"][j], "nn", F32, bias=full_small["conf_pw1_b"][j][None], name=f"conf_pw1_{i}")
            v, v1 = confmid_fwd(a, dw3_of(j), full_small["conf_dw_b"][j][None], full_small["conf_ln_g"][j][None],
                                full_small["conf_ln_b"][j][None], seg, tr, f"conf_mid{i}")
            out = mm(v, Wb["conf_pw2_w"][j], "nn", F32, bias=full_small["conf_pw2_b"][j][None], name=f"conf_pw2_{i}")
            s.update(a=a, v=v, v1=v1, seg=seg)
        if col_major:
            out = from_scan(out)
        h1 = post_fwd(h, out, post_mix_g[i][None], g1, tr, f"post_mix{i}")
        u2 = prenorm_fwd(h1, pre_mlp_g[i][None], sh2, sc2, tr, f"prenorm_mlp{i}")
        hid, actm = mm(u2, Wb["mlp_w1"][i], "nn", F32, relu2=True, name=f"mlp_up{i}")
        f = mm(actm, Wb["mlp_w2"][i], "nn", F32, name=f"mlp_down{i}")
        h = post_fwd(h1, f, post_mlp_g[i][None], g2, tr, f"post_mlp{i}")
        s.update(out=out, h1=h1, u2=u2, hid=hid, actm=actm, f=f)
        saved.append(s)

    loss_blk, Gr = loss_head(h, loss_target[0], tr, "loss_head")
    loss = lax.psum(loss_blk[0, 0], ("x", "y", "c"))

    gb = {n: [None] * W[n].shape[0] for n in BIG}
    gs = {n: [None] * W[n].shape[0] for n in W_NAMES if n not in BIG and n not in ("c_ctx", "mod_w", "mod_b")}
    dmod = [None] * depth
    for i in reversed(range(depth)):
        kind, j = i % 2, i // 2
        col_major = (j % 2) == 1
        sh1, sc1, g1, sh2, sc2, g2 = six(i)
        s = saved[i]
        df, gs["post_mlp_g"][i], dg2, _ = post_bwd(s["f"], post_mlp_g[i][None], g2, Gr, tr, f"post_mlp_bwd{i}")
        gb["mlp_w2"][i] = mm(s["actm"], df, "tn", BF16, name=f"mlp_down_wg{i}")
        dhid = mm(df, Wb["mlp_w2"][i], "nt", BF16, mul_relu=s["hid"], name=f"mlp_down_dg{i}")
        gb["mlp_w1"][i] = mm(s["u2"], dhid, "tn", BF16, name=f"mlp_up_wg{i}")
        du2 = mm(dhid, Wb["mlp_w1"][i], "nt", F32, name=f"mlp_up_dg{i}")
        Gr, gs["pre_mlp_g"][i], dsh2, dsc2 = prenorm_bwd(s["h1"], pre_mlp_g[i][None], sh2, sc2, du2, Gr, tr, f"prenorm_mlp_bwd{i}")
        dout, gs["post_mix_g"][i], dg1, dout_sum = post_bwd(s["out"], post_mix_g[i][None], g1, Gr, tr, f"post_mix_bwd{i}")
        if col_major:
            dout = to_scan(dout)
        if kind == 0:
            gb["ssm_out_w"][j] = mm(s["yn"], dout, "tn", BF16, name=f"ssm_out_wg{i}")
            dyn = mm(dout, Wb["ssm_out_w"][j], "nt", F32, name=f"ssm_out_dg{i}")
            dys, dz, gs["ssm_norm_g"][j] = gnorm_bwd(s["y2"], s["proj"], ssm_norm_g[j][None], dyn, CHUNK, f"ssm_gnorm_bwd{i}")
            bias, alog, dsk = ssm_params(j)
            dx2, db2, dc2, ddtr, dbias, dalog, ddsk = ssd_bwd(s["act"], s["dtr"], bias, alog, dsk, s["hsave"], dys, d_inner, ncc, f"ssd_bwd{i}")
            cw = full_small["ssm_conv_w"][j]
            nx, nb_ = d_inner // tc, GN // tc
            dxx, dwx, dbx = conv5_bwd(dx2, s["pre"], s["proj"], cw, 0, nx, tr, tc, f"ssm_conv_bwd_x{i}")
            dxb, dwb, dbb = conv5_bwd(db2, s["pre"], s["proj"], cw, nx, 2 * nx, tr, tc, f"ssm_conv_bwd_b{i}")
            dxc, dwc, dbc = conv5_bwd(dc2, s["pre"], s["proj"], cw, nx + nb_, 2 * nx + nb_, tr, tc, f"ssm_conv_bwd_c{i}")
            gs["ssm_conv_w"][j] = jnp.concatenate([dwx, dwb, dwc], axis=1)
            gs["ssm_conv_b"][j] = jnp.concatenate([dbx, dbb, dbc], axis=1)[0]
            ddt = ddtr.transpose(2, 0, 1, 3).reshape(T, 2 * H).astype(BF16)
            dproj = jnp.concatenate([dz, dxx, dxb, dxc, ddt], axis=1)
            gb["ssm_in_w"][j] = mm(s["u"], dproj, "tn", BF16, name=f"ssm_in_wg{i}")
            du = mm(dproj, Wb["ssm_in_w"][j], "nt", F32, name=f"ssm_in_dg{i}")
            for nm, val in (("ssm_dt_bias", dbias), ("ssm_a_log", dalog), ("ssm_d", ddsk)):
                gs[nm + "_f"][j] = val[0].reshape(H)
                gs[nm + "_b"][j] = val[1].reshape(H)
        else:
            gb["conf_pw2_w"][j] = mm(s["v"], dout, "tn", BF16, name=f"conf_pw2_wg{i}")
            gs["conf_pw2_b"][j] = dout_sum[0]
            dv = mm(dout, Wb["conf_pw2_w"][j], "nt", F32, name=f"conf_pw2_dg{i}")
            da, da_sum, dw3, ddb, dlg, dlb = confmid_bwd(s["a"], s["v1"], dw3_of(j), full_small["conf_ln_g"][j][None],
                                                          full_small["conf_ln_b"][j][None], dv, s["seg"], tr, f"conf_mid_bwd{i}")
            gs["conf_pw1_b"][j] = da_sum[0]
            gs["conf_dw_w"][j] = dw3.swapaxes(0, 1).reshape(dw3.shape[1], D)
            gs["conf_dw_b"][j], gs["conf_ln_g"][j], gs["conf_ln_b"][j] = ddb[0], dlg[0], dlb[0]
            gb["conf_pw1_w"][j] = mm(s["u"], da, "tn", BF16, name=f"conf_pw1_wg{i}")
            du = mm(da, Wb["conf_pw1_w"][j], "nt", F32, name=f"conf_pw1_dg{i}")
        if col_major:
            du = from_scan(du)
        Gr, gs["pre_mix_g"][i], dsh1, dsc1 = prenorm_bwd(s["h"], pre_mix_g[i][None], sh1, sc1, du, Gr, tr, f"prenorm_mix_bwd{i}")
        dmod[i] = jnp.concatenate([t.reshape(2, D) for t in (dsh1, dsc1, dg1, dsh2, dsc2, dg2)], axis=1)
    grad_x = Gr[Lc:][None]

    small_names = list(gs)
    small_local = [jnp.stack([t.reshape(W[n].shape[1:] if n not in SMALL_SHARDED else t.shape) for t in gs[n]]) for n in small_names]
    small_shapes = [t.shape for t in small_local] + [(depth, 2, 6 * D)]
    got = allgather8(_pack(small_local + [jnp.stack(dmod)]), "gather_small_grads")
    summed = _unpack(sum_slabs(got, "sum_small_grads"), small_shapes)
    grads = {}
    for n, t in zip(small_names, summed[:-1]):
        if n in SMALL_SHARDED:
            w = W[n].shape[-1]
            t = lax.dynamic_slice_in_dim(t, chip * w, w, axis=t.ndim - 1)
        grads[n] = t
    grads["mod_b"] = summed[-1][:, 0] + summed[-1][:, 1]
    dm_all = _unpack(got, small_shapes)[-1]
    dm_ctx = sum_slabs(dm_all[:, :, 0], "sum_dmod_ctx")
    dm_rows = jnp.concatenate([dm_all[:, :, 1].swapaxes(0, 1), dm_ctx[:, None], jnp.zeros((depth, 7, 6 * D), F32)], axis=1)
    dm_mine = lax.dynamic_slice_in_dim(dm_rows, chip * ncol, ncol, axis=2)
    grads["mod_w"], dcond = mod_bwd(cond, mod_w, dm_mine, "mod_bwd")
    dcc = sum_slabs(dcond[:, 8:9], "sum_dcond_layers")
    dcc_all = allgather8(jnp.pad(dcc, ((0, 7), (0, 0))), "gather_dcond")
    dcc_sum = sum_slabs(dcc_all[0::2, 0:1], "sum_dcond_chips")
    grads["c_ctx"] = silu_grad(dcc_sum, c_ctx.reshape(1, D), "c_ctx_grad").reshape(D)

    slabs = [_chip_slabs(jnp.stack(gb[n]), BIG[n]) for n in big_names]
    theirs = sibling_swap_halves(slabs, "sibling_swap_grads")
    chip_part = []
    for n, s, t in zip(big_names, slabs, theirs):
        hl = t.shape[1]
        own = lax.dynamic_slice_in_dim(s, mc * hl, hl, axis=1)
        chip_part.append(add_pair(_view2d(own), _view2d(t), f"add_cores_{n}").reshape(t.shape))
    recv = chip_exchange(chip_part, "scatter_grads")
    recv = [lax.dynamic_update_slice(r, lax.dynamic_slice_in_dim(p, chip, 1, axis=0), (chip,) + (0,) * (r.ndim - 1))
            for r, p in zip(recv, chip_part)]
    half = [sum_slabs(r.reshape((4, -1, r.shape[-1])), f"sum_grads_{n}") for n, r in zip(big_names, recv)]
    for n, mine, theirs in zip(big_names, half, sibling_exchange(half, "sibling_grads")):
        grads[n] = (mine, theirs)

    res = {}
    for n in W_NAMES:
        w2 = _view2d(W[n])
        cols = w2.shape[1]
        g = grads[n] if isinstance(grads[n], tuple) else _view2d(grads[n])
        outs = adamw(w2, g, _view2d(given["m_" + n]), _view2d(given["v_" + n]), f"adamw_{n}", tr=max(8, (262144 // cols) // 8 * 8))
        res[n] = [o.reshape(W[n].shape) for o in outs]
    return (loss, grad_x, *[res[n][0] for n in W_NAMES], *[res[n][1] for n in W_NAMES], *[res[n][2] for n in W_NAMES],
            *[res[n][3] for n in W_NAMES])
```
